```python
import jax, jax.numpy as jnp
from jax import lax
import numpy as np

D_MODEL = 1024
BATCH = 8
SEQ = 4096
DEPTH = 1

MEM_LEN = 256
SGU_GROUPS = 4
SGU_GROUP_DIM = 128
SGU_WIDTH = SGU_GROUPS * SGU_GROUP_DIM
SGU_CHUNK = 128
MOBA_HEADS = 8
MOBA_HEAD_DIM = 64
MOBA_WIDTH = MOBA_HEADS * MOBA_HEAD_DIM
MOBA_BLOCK = 256
MOBA_TOPK = 3
Q_BLOCK = 128
MEM_HEADS = 4
MEM_HEAD_DIM = 128
MEM_WIDTH = MEM_HEADS * MEM_HEAD_DIM
N_BRANCHES = 3
SPLIT_SIZES = (SGU_WIDTH, SGU_WIDTH, MOBA_WIDTH, MOBA_WIDTH, MOBA_WIDTH, MEM_WIDTH, N_BRANCHES * D_MODEL)
IN_WIDTH = sum(SPLIT_SIZES)
N_EXPERTS = 32
TOP_K = 4
D_FF = D_MODEL
SWIGLU_ALPHA = 1.702
SWIGLU_LIMIT = 7.0
EXPERT_BLOCK = 128
EPS = 1e-6
NEG_INF = -1e30

kernel_name = 'hybrid_sgu_moba_memxattn_moe_block'


def rms_norm(x, g):
    xf = x.astype(jnp.float32)
    y = xf * lax.rsqrt(jnp.mean(xf * xf, axis=-1, keepdims=True) + EPS)
    return (y * g.astype(jnp.float32)).astype(x.dtype)


def alibi_slopes(n):
    return jnp.asarray(2.0 ** (-8.0 * np.arange(1, n + 1, dtype=np.float32) / n), dtype=jnp.float32)


def sgu_mixer(u, v, g_v, w_spatial, b_spatial):
    B, S, _ = u.shape
    nc = S // SGU_CHUNK
    v = rms_norm(v, g_v)
    vc = v.reshape(B, nc, SGU_CHUNK, SGU_GROUPS, SGU_GROUP_DIM)
    causal = jnp.tril(jnp.ones((SGU_CHUNK, SGU_CHUNK), dtype=bool))
    ws = jnp.where(causal[None], w_spatial, 0.0).astype(v.dtype)
    mixed = jnp.einsum('gts,bcsgd->bctgd', ws, vc) + b_spatial.T[None, None, :, :, None].astype(v.dtype)
    return u * mixed.reshape(B, S, SGU_WIDTH)


def moba_attention(q, k, v, g_q, g_k, slopes):
    B, S, _ = q.shape
    H, Dh, BS = MOBA_HEADS, MOBA_HEAD_DIM, MOBA_BLOCK
    scale = Dh ** -0.5
    q = rms_norm(q.reshape(B, S, H, Dh), g_q).transpose(0, 2, 1, 3)
    k = rms_norm(k.reshape(B, S, H, Dh), g_k).transpose(0, 2, 1, 3)
    v = v.reshape(B, S, H, Dh).transpose(0, 2, 1, 3)
    nb = -(-S // BS)
    pad = nb * BS - S
    k_blocks = jnp.pad(k, ((0, 0), (0, 0), (0, pad), (0, 0))).reshape(B, H, nb, BS, Dh)
    v_blocks = jnp.pad(v, ((0, 0), (0, 0), (0, pad), (0, 0))).reshape(B, H, nb, BS, Dh)
    k_mean = jnp.mean(k_blocks.astype(jnp.float32), axis=3)
    gate = jnp.einsum('bhtd,bhnd->bhtn', q.astype(jnp.float32), k_mean)
    t_all = jnp.arange(S)
    eligible = jnp.arange(nb)[None, :] < (t_all // BS)[:, None]
    gate = jnp.where(eligible, gate, NEG_INF)
    k_sel = min(MOBA_TOPK, nb)
    top_val, top_idx = lax.top_k(gate, k_sel)
    top_valid = top_val > 0.5 * NEG_INF
    nq = S // Q_BLOCK
    h_ix = jnp.arange(H)[:, None, None]

    def step(bc):
        b = bc // nq
        t0 = (bc % nq) * Q_BLOCK
        t = t0 + jnp.arange(Q_BLOCK)
        qb = lax.dynamic_slice_in_dim(lax.dynamic_index_in_dim(q, b, 0, keepdims=False), t0, Q_BLOCK, axis=1)
        idx = lax.dynamic_slice_in_dim(lax.dynamic_index_in_dim(top_idx, b, 0, keepdims=False), t0, Q_BLOCK, axis=1)
        valid = lax.dynamic_slice_in_dim(lax.dynamic_index_in_dim(top_valid, b, 0, keepdims=False), t0, Q_BLOCK, axis=1)
        kb = lax.dynamic_index_in_dim(k_blocks, b, 0, keepdims=False)
        vb = lax.dynamic_index_in_dim(v_blocks, b, 0, keepdims=False)
        k_g = kb[h_ix, idx]
        v_g = vb[h_ix, idx]
        s_sel = jnp.einsum('hqd,hqnkd->hqnk', qb, k_g).astype(jnp.float32) * scale
        key_pos = idx[..., None] * BS + jnp.arange(BS)
        s_sel = s_sel - slopes[:, None, None, None] * (t[None, :, None, None] - key_pos)
        s_sel = jnp.where(valid[..., None], s_sel, NEG_INF)
        own = t0 // BS
        k_own = lax.dynamic_index_in_dim(kb, own, 1, keepdims=False)
        v_own = lax.dynamic_index_in_dim(vb, own, 1, keepdims=False)
        s_own = jnp.einsum('hqd,hkd->hqk', qb, k_own).astype(jnp.float32) * scale
        own_pos = own * BS + jnp.arange(BS)
        s_own = s_own - slopes[:, None, None] * (t[:, None] - own_pos[None, :])
        s_own = jnp.where(own_pos[None, None, :] <= t[None, :, None], s_own, NEG_INF)
        logits = jnp.concatenate([s_sel.reshape(H, Q_BLOCK, k_sel * BS), s_own], axis=-1)
        p = jax.nn.softmax(logits, axis=-1).astype(v.dtype)
        p_sel = p[..., :k_sel * BS].reshape(H, Q_BLOCK, k_sel, BS)
        p_own = p[..., k_sel * BS:]
        return jnp.einsum('hqnk,hqnkd->hqd', p_sel, v_g) + jnp.einsum('hqk,hkd->hqd', p_own, v_own)

    out = lax.map(step, jnp.arange(B * nq))
    out = out.reshape(B, nq, H, Q_BLOCK, Dh).transpose(0, 1, 3, 2, 4)
    return out.reshape(B, S, MOBA_WIDTH)


def memory_attention(q, mem_n, w_mem_kv, g_q, g_k):
    B, S, _ = q.shape
    M = mem_n.shape[1]
    k, v = jnp.split(mem_n @ w_mem_kv, 2, axis=-1)
    q = rms_norm(q.reshape(B, S, MEM_HEADS, MEM_HEAD_DIM), g_q)
    k = rms_norm(k.reshape(B, M, MEM_HEADS, MEM_HEAD_DIM), g_k)
    v = v.reshape(B, M, MEM_HEADS, MEM_HEAD_DIM)
    s = jnp.einsum('bthd,bmhd->bhtm', q, k).astype(jnp.float32) * (MEM_HEAD_DIM ** -0.5)
    p = jax.nn.softmax(s, axis=-1).astype(v.dtype)
    return jnp.einsum('bhtm,bmhd->bthd', p, v).reshape(B, S, MEM_WIDTH)


def swiglu_clamped(hh):
    glu, lin = jnp.split(hh, 2, axis=-1)
    glu = jnp.minimum(glu, SWIGLU_LIMIT)
    lin = jnp.clip(lin, -SWIGLU_LIMIT, SWIGLU_LIMIT)
    return glu * jax.nn.sigmoid(SWIGLU_ALPHA * glu) * (lin + 1.0)


def moe_ffn(h, w_router, b_router, w_gate_up, b_gate_up, w_down, b_down):
    B, S, D = h.shape
    T = B * S
    TK = T * TOP_K
    Mb = EXPERT_BLOCK
    xt = h.reshape(T, D)
    logits = (xt @ w_router + b_router).astype(jnp.float32)
    top_logit, top_e = lax.top_k(logits, TOP_K)
    top_w = jax.nn.softmax(top_logit, axis=-1)
    flat_e = top_e.reshape(TK)
    flat_tok = (jnp.arange(TK) // TOP_K).astype(jnp.int32)
    flat_w = top_w.reshape(TK)
    order = jnp.argsort(flat_e)
    sorted_e = flat_e[order]
    counts = jnp.bincount(flat_e, length=N_EXPERTS)
    starts = jnp.cumsum(counts) - counts
    padded = (counts + Mb - 1) // Mb * Mb
    pad_ends = jnp.cumsum(padded)
    pad_starts = pad_ends - padded
    dest = pad_starts[sorted_e] + (jnp.arange(TK) - starts[sorted_e])
    n_blocks = -(-(TK + N_EXPERTS * (Mb - 1)) // Mb)
    P = n_blocks * Mb
    buf_tok = jnp.full((P,), T, jnp.int32).at[dest].set(flat_tok[order])
    buf_w = jnp.zeros((P,), jnp.float32).at[dest].set(flat_w[order])
    block_e = jnp.minimum(jnp.searchsorted(pad_ends, jnp.arange(n_blocks) * Mb, side='right'), N_EXPERTS - 1)
    x_pad = jnp.concatenate([xt, jnp.zeros((1, D), xt.dtype)], axis=0)

    def expert_block(args):
        tok, e = args
        xb = x_pad[tok]
        w1 = lax.dynamic_index_in_dim(w_gate_up, e, 0, keepdims=False)
        b1 = lax.dynamic_index_in_dim(b_gate_up, e, 0, keepdims=False)
        w2 = lax.dynamic_index_in_dim(w_down, e, 0, keepdims=False)
        b2 = lax.dynamic_index_in_dim(b_down, e, 0, keepdims=False)
        return swiglu_clamped(xb @ w1 + b1) @ w2 + b2

    y = lax.map(expert_block, (buf_tok.reshape(n_blocks, Mb), block_e)).reshape(P, D)
    y = y * buf_w[:, None].astype(y.dtype)
    out = jnp.zeros((T + 1, D), y.dtype).at[buf_tok].add(y)[:T]
    return out.reshape(B, S, D)


def setup_inputs(seed: int = 0) -> dict:
    key = jax.random.key(seed)
    ks = jax.random.split(key, 26)
    L, D, E, F = DEPTH, D_MODEL, N_EXPERTS, D_FF
    nrm = lambda k, shape, fan_in: jax.random.normal(k, shape, jnp.float32) * (fan_in ** -0.5)
    gain = lambda k, shape: 1.0 + 0.02 * jax.random.normal(k, shape, jnp.float32)
    small = lambda k, shape: 0.01 * jax.random.normal(k, shape, jnp.float32)
    return {
        'x': jax.random.normal(ks[0], (BATCH, SEQ, D), jnp.float32),
        'mem': jax.random.normal(ks[1], (BATCH, MEM_LEN, D), jnp.float32),
        'g_mix': gain(ks[2], (L, D)),
        'w_in': nrm(ks[3], (L, D, IN_WIDTH), D),
        'g_sgu_v': gain(ks[4], (L, SGU_WIDTH)),
        'w_spatial': nrm(ks[5], (L, SGU_GROUPS, SGU_CHUNK, SGU_CHUNK), SGU_CHUNK),
        'b_spatial': 1.0 + 0.1 * jax.random.normal(ks[6], (L, SGU_GROUPS, SGU_CHUNK), jnp.float32),
        'g_q_moba': gain(ks[7], (L, MOBA_HEAD_DIM)),
        'g_k_moba': gain(ks[8], (L, MOBA_HEAD_DIM)),
        'g_mem': gain(ks[9], (L, D)),
        'w_mem_kv': nrm(ks[10], (L, D, 2 * MEM_WIDTH), D),
        'g_q_mem': gain(ks[11], (L, MEM_HEAD_DIM)),
        'g_k_mem': gain(ks[12], (L, MEM_HEAD_DIM)),
        'w_br_sgu': nrm(ks[13], (L, SGU_WIDTH, D), SGU_WIDTH),
        'w_br_moba': nrm(ks[14], (L, MOBA_WIDTH, D), MOBA_WIDTH),
        'w_br_mem': nrm(ks[15], (L, MEM_WIDTH, D), MEM_WIDTH),
        'w_out': nrm(ks[16], (L, D, D), D),
        'g_ffn': gain(ks[17], (L, D)),
        'w_router': nrm(ks[18], (L, D, E), D),
        'b_router': small(ks[19], (L, E)),
        'w_gate_up': nrm(ks[20], (L, E, D, 2 * F), D),
        'b_gate_up': small(ks[21], (L, E, 2 * F)),
        'w_down': nrm(ks[22], (L, E, F, D), F),
        'b_down': small(ks[23], (L, E, D)),
    }


def reference(x, mem, g_mix, w_in, g_sgu_v, w_spatial, b_spatial, g_q_moba, g_k_moba,
              g_mem, w_mem_kv, g_q_mem, g_k_mem, w_br_sgu, w_br_moba, w_br_mem, w_out,
              g_ffn, w_router, b_router, w_gate_up, b_gate_up, w_down, b_down):
    slopes = alibi_slopes(MOBA_HEADS)
    offsets = [int(o) for o in np.cumsum(SPLIT_SIZES)[:-1]]
    for l in range(DEPTH):
        h = rms_norm(x, g_mix[l])
        proj = h @ w_in[l]
        u, v_s, q_b, k_b, v_b, q_m, gate_logits = jnp.split(proj, offsets, axis=-1)
        y_a = sgu_mixer(jax.nn.gelu(u, approximate=False), jax.nn.gelu(v_s, approximate=False),
                        g_sgu_v[l], w_spatial[l], b_spatial[l])
        y_b = moba_attention(q_b, k_b, v_b, g_q_moba[l], g_k_moba[l], slopes)
        y_c = memory_attention(q_m, rms_norm(mem, g_mem[l]), w_mem_kv[l], g_q_mem[l], g_k_mem[l])
        g_a, g_b, g_c = jnp.split(jax.nn.sigmoid(gate_logits), N_BRANCHES, axis=-1)
        merged = g_a * (y_a @ w_br_sgu[l]) + g_b * (y_b @ w_br_moba[l]) + g_c * (y_c @ w_br_mem[l])
        x = x + merged @ w_out[l]
        x = x + moe_ffn(rms_norm(x, g_ffn[l]), w_router[l], b_router[l], w_gate_up[l],
                        b_gate_up[l], w_down[l], b_down[l])
    return x
```

```python
import functools

import numpy as np
import jax
import jax.numpy as jnp
from jax import lax
from jax.experimental import pallas as pl
from jax.experimental.pallas import tpu as pltpu

F32 = jnp.float32
BF16 = jnp.bfloat16

D_MODEL = 1024
MEM_LEN = 256
SGU_GROUPS = 4
SGU_CHUNK = 128
SGU_WIDTH = 512
MOBA_HEADS = 8
MOBA_HEAD_DIM = 64
MOBA_BLOCK = 256
MOBA_TOPK = 3
MEM_HEADS = 4
MEM_HEAD_DIM = 128
SPLIT = 512
IN_WIDTH = 6144
N_SPLITS = IN_WIDTH // SPLIT
N_EXPERTS = 32
TOP_K = 4
D_FF = 1024
SWIGLU_ALPHA = 1.702
SWIGLU_LIMIT = 7.0
EPS = 1e-6
NEG_INF = -1e30
LANES = 128

VMEM_LIMIT = 56 * 1024 * 1024

INPROJ_TM = 256
MERGE_TM = 256
ROUTE_TM = 256
EXPERT_BM = 256


def _gelu(x):
    return 0.5 * x * (1.0 + lax.erf(x * np.float32(np.sqrt(0.5))))


def _rms(x, g):
    return x * lax.rsqrt(jnp.mean(x * x, axis=-1, keepdims=True) + EPS) * g


def _inproj_kernel(x_ref, gmix_ref, w_ref, gsgu_ref, o_ref):
    h = _rms(x_ref[...], gmix_ref[...]).astype(BF16)
    for c in range(N_SPLITS):
        sl = slice(c * SPLIT, (c + 1) * SPLIT)
        acc = jnp.dot(h, w_ref[:, sl], preferred_element_type=F32)
        if c == 0:
            val = _gelu(acc)
        elif c == 1:
            val = _rms(_gelu(acc), gsgu_ref[...])
        elif c >= 6:
            val = jax.nn.sigmoid(acc)
        else:
            val = acc
        o_ref[:, sl] = val.astype(BF16)


def _inproj(x2, g_mix, w_in, g_sgu_v):
    T = x2.shape[0]
    tm = INPROJ_TM
    return pl.pallas_call(
        _inproj_kernel,
        out_shape=jax.ShapeDtypeStruct((T, IN_WIDTH), BF16),
        grid=(T // tm,),
        in_specs=[
            pl.BlockSpec((tm, D_MODEL), lambda i: (i, 0)),
            pl.BlockSpec((1, D_MODEL), lambda i: (0, 0)),
            pl.BlockSpec((D_MODEL, IN_WIDTH), lambda i: (0, 0)),
            pl.BlockSpec((1, SGU_WIDTH), lambda i: (0, 0)),
        ],
        out_specs=pl.BlockSpec((tm, IN_WIDTH), lambda i: (i, 0)),
        compiler_params=pltpu.CompilerParams(
            dimension_semantics=("arbitrary",), vmem_limit_bytes=VMEM_LIMIT),
        name="inproj",
    )(x2, g_mix, w_in, g_sgu_v)


def _pair_norm(x, g, head0):
    sq = x * x
    s0 = jnp.sum(jnp.where(head0, sq, 0.0), axis=-1, keepdims=True)
    s1 = jnp.sum(jnp.where(head0, 0.0, sq), axis=-1, keepdims=True)
    inv = jnp.where(head0, lax.rsqrt(s0 * (1.0 / MOBA_HEAD_DIM) + EPS),
                    lax.rsqrt(s1 * (1.0 / MOBA_HEAD_DIM) + EPS))
    return x * inv * g


def _moba_kernel(slopes_ref, q_ref, k_ref, v_ref, gq_ref, gk_ref, bias_ref, o_ref,
                 kn_sc, vt_sc, kmean_sc, mrow_sc, *, nb):
    hp = pl.program_id(1)
    qi = pl.program_id(2)
    bs = MOBA_BLOCK
    head0 = lax.broadcasted_iota(jnp.int32, (1, LANES), 1) < MOBA_HEAD_DIM

    @pl.when(qi == 0)
    def _prepare_kv():
        def body(j, carry):
            rows = pl.ds(pl.multiple_of(j * bs, bs), bs)
            kn = _pair_norm(k_ref[rows, :].astype(F32), gk_ref[...], head0)
            kn_sc[j] = kn.astype(BF16)
            kmean_sc[pl.ds(j, 1), :] = jnp.mean(kn, axis=0, keepdims=True)
            vt_sc[j] = v_ref[rows, :].astype(F32).T.astype(BF16)
            return carry
        lax.fori_loop(0, nb, body, 0)

    qn = _pair_norm(q_ref[...].astype(F32), gq_ref[...], head0)
    blk = lax.broadcasted_iota(jnp.int32, (nb, 1), 0)
    q_heads = []
    for hh in range(2):
        hmask = head0 if hh == 0 else jnp.logical_not(head0)
        qh = jnp.where(hmask, qn, 0.0)
        gate = lax.dot_general(kmean_sc[...], qh, (((1,), (1,)), ((), ())),
                               precision=lax.Precision.HIGHEST, preferred_element_type=F32)
        rank = jnp.zeros((nb, bs), F32)
        for i in range(nb):
            gi = gate[i:i + 1, :]
            beats = jnp.logical_or(gi > gate, jnp.logical_and(gi == gate, blk > i))
            rank = rank + jnp.where(jnp.logical_and(beats, i < qi), 1.0, 0.0)
        sel = jnp.logical_and(blk < qi, rank < float(MOBA_TOPK))
        slope = slopes_ref[2 * hp + hh]
        dist = ((qi - blk) * bs).astype(F32)
        mrow_sc[hh] = jnp.where(sel, -slope * dist, NEG_INF)
        q_heads.append((qh * (MOBA_HEAD_DIM ** -0.5)).astype(BF16))

    def attend(j, bias_idx, use_rows, carry):
        kj = kn_sc[j]
        vtj = vt_sc[j]
        out = []
        for hh in range(2):
            m, l, acc = carry[hh]
            s = lax.dot_general(kj, q_heads[hh], (((1,), (1,)), ((), ())),
                                preferred_element_type=F32)
            s = s + bias_ref[hh, bias_idx]
            if use_rows:
                s = s + mrow_sc[hh, pl.ds(j, 1), :]
            m_new = jnp.maximum(m, jnp.max(s, axis=0, keepdims=True))
            alpha = jnp.exp(m - m_new)
            p = jnp.exp(s - m_new)
            l = alpha * l + jnp.sum(p, axis=0, keepdims=True)
            acc = alpha * acc + jnp.dot(vtj, p.astype(BF16), preferred_element_type=F32)
            out.append((m_new, l, acc))
        return tuple(out)

    init = tuple((jnp.full((1, bs), NEG_INF, F32), jnp.zeros((1, bs), F32),
                  jnp.zeros((LANES, bs), F32)) for _ in range(2))
    carry = attend(qi, 1, False, init)
    carry = lax.fori_loop(0, qi, lambda j, c: attend(j, 0, True, c), carry)
    (_, l0, a0), (_, l1, a1) = carry
    row = lax.broadcasted_iota(jnp.int32, (LANES, 1), 0)
    out_t = jnp.where(row < MOBA_HEAD_DIM, a0 / l0, a1 / l1)
    o_ref[...] = out_t.T.astype(BF16)


def _moba_bias_tiles():
    slopes = 2.0 ** (-8.0 * np.arange(1, MOBA_HEADS + 1, dtype=np.float32) / MOBA_HEADS)
    kk = np.arange(MOBA_BLOCK)[:, None]
    qq = np.arange(MOBA_BLOCK)[None, :]
    rel = (qq - kk).astype(np.float32)
    past = -slopes[:, None, None] * rel[None]
    own = np.where((kk <= qq)[None], past, np.float32(NEG_INF))
    return slopes.astype(np.float32), np.stack([past, own], axis=1).astype(np.float32)


def _moba(proj, g_q, g_k, B, S):
    nb = S // MOBA_BLOCK
    slopes, bias = _moba_bias_tiles()
    gq2 = jnp.tile(g_q.reshape(1, MOBA_HEAD_DIM), (1, 2))
    gk2 = jnp.tile(g_k.reshape(1, MOBA_HEAD_DIM), (1, 2))
    q_col, k_col, v_col = 2 * SPLIT // LANES, 3 * SPLIT // LANES, 4 * SPLIT // LANES
    grid_spec = pltpu.PrefetchScalarGridSpec(
        num_scalar_prefetch=1,
        grid=(B, MOBA_HEADS // 2, nb),
        in_specs=[
            pl.BlockSpec((MOBA_BLOCK, LANES), lambda b, h, q, s: (b * nb + q, q_col + h)),
            pl.BlockSpec((S, LANES), lambda b, h, q, s: (b, k_col + h)),
            pl.BlockSpec((S, LANES), lambda b, h, q, s: (b, v_col + h)),
            pl.BlockSpec((1, LANES), lambda b, h, q, s: (0, 0)),
            pl.BlockSpec((1, LANES), lambda b, h, q, s: (0, 0)),
            pl.BlockSpec((2, 2, MOBA_BLOCK, MOBA_BLOCK), lambda b, h, q, s: (h, 0, 0, 0)),
        ],
        out_specs=pl.BlockSpec((MOBA_BLOCK, LANES), lambda b, h, q, s: (b * nb + q, h)),
        scratch_shapes=[
            pltpu.VMEM((nb, MOBA_BLOCK, LANES), BF16),
            pltpu.VMEM((nb, LANES, MOBA_BLOCK), BF16),
            pltpu.VMEM((nb, LANES), F32),
            pltpu.VMEM((2, nb, MOBA_BLOCK), F32),
        ],
    )
    return pl.pallas_call(
        functools.partial(_moba_kernel, nb=nb),
        out_shape=jax.ShapeDtypeStruct((B * S, MOBA_HEADS * MOBA_HEAD_DIM), BF16),
        grid_spec=grid_spec,
        compiler_params=pltpu.CompilerParams(
            dimension_semantics=("arbitrary", "arbitrary", "arbitrary"),
            vmem_limit_bytes=VMEM_LIMIT),
        name="moba",
    )(jnp.asarray(slopes), proj, proj, proj, gq2, gk2, jnp.asarray(bias))


def _memkv_kernel(mem_ref, gmem_ref, w_ref, gk_ref, o_ref):
    mn = _rms(mem_ref[...], gmem_ref[...]).astype(BF16)
    kv = jnp.dot(mn, w_ref[...], preferred_element_type=F32)
    width = MEM_HEADS * MEM_HEAD_DIM
    for h in range(MEM_HEADS):
        sl = slice(h * MEM_HEAD_DIM, (h + 1) * MEM_HEAD_DIM)
        o_ref[:, sl] = _rms(kv[:, sl], gk_ref[...]).astype(BF16)
    o_ref[:, width:] = kv[:, width:].astype(BF16)


def _memkv(mem2, g_mem, w_mem_kv, g_k_mem, B):
    width = MEM_HEADS * MEM_HEAD_DIM
    return pl.pallas_call(
        _memkv_kernel,
        out_shape=jax.ShapeDtypeStruct((B * MEM_LEN, 2 * width), BF16),
        grid=(B,),
        in_specs=[
            pl.BlockSpec((MEM_LEN, D_MODEL), lambda b: (b, 0)),
            pl.BlockSpec((1, D_MODEL), lambda b: (0, 0)),
            pl.BlockSpec((D_MODEL, 2 * width), lambda b: (0, 0)),
            pl.BlockSpec((1, MEM_HEAD_DIM), lambda b: (0, 0)),
        ],
        out_specs=pl.BlockSpec((MEM_LEN, 2 * width), lambda b: (b, 0)),
        compiler_params=pltpu.CompilerParams(
            dimension_semantics=("arbitrary",), vmem_limit_bytes=VMEM_LIMIT),
        name="memkv",
    )(mem2, g_mem, w_mem_kv, g_k_mem)


def _merge_kernel(gu_ref, vn_ref, yb_ref, qm_ref, ga_ref, gb_ref, gc_ref, x_ref, kv_ref,
                  ws_ref, bsp_ref, gqm_ref, wa_ref, wb_ref, wc_ref, wo_ref, gffn_ref,
                  wr_ref, br_ref,
                  x1_ref, h2_ref, tope_ref, topw_ref, rank_ref, cnt_ref, run_sc):
    i = pl.program_id(0)
    tm = x_ref.shape[0]

    @pl.when(i == 0)
    def _():
        run_sc[...] = jnp.zeros_like(run_sc)

    ya_chunks = []
    for c in range(tm // SGU_CHUNK):
        rows = slice(c * SGU_CHUNK, (c + 1) * SGU_CHUNK)
        groups = []
        for g in range(SGU_GROUPS):
            cols = slice(g * LANES, (g + 1) * LANES)
            groups.append(jnp.dot(ws_ref[g], vn_ref[rows, cols], preferred_element_type=F32))
        mixed = jnp.concatenate(groups, axis=1) + bsp_ref[...]
        ya_chunks.append(gu_ref[rows, :].astype(F32) * mixed)
    y_a = jnp.concatenate(ya_chunks, axis=0).astype(BF16)

    width = MEM_HEADS * MEM_HEAD_DIM
    yc_heads = []
    for h in range(MEM_HEADS):
        cols = slice(h * MEM_HEAD_DIM, (h + 1) * MEM_HEAD_DIM)
        qn = _rms(qm_ref[:, cols].astype(F32), gqm_ref[...]).astype(BF16)
        s = lax.dot_general(qn, kv_ref[:, cols], (((1,), (1,)), ((), ())),
                            preferred_element_type=F32) * (MEM_HEAD_DIM ** -0.5)
        s = s - jnp.max(s, axis=-1, keepdims=True)
        p = jnp.exp(s)
        p = p / jnp.sum(p, axis=-1, keepdims=True)
        vcols = slice(width + h * MEM_HEAD_DIM, width + (h + 1) * MEM_HEAD_DIM)
        yc_heads.append(jnp.dot(p.astype(BF16), kv_ref[:, vcols], preferred_element_type=F32))
    y_c = jnp.concatenate(yc_heads, axis=1).astype(BF16)

    merged = ga_ref[...].astype(F32) * jnp.dot(y_a, wa_ref[...], preferred_element_type=F32)
    merged = merged + gb_ref[...].astype(F32) * jnp.dot(yb_ref[...], wb_ref[...],
                                                       preferred_element_type=F32)
    merged = merged + gc_ref[...].astype(F32) * jnp.dot(y_c, wc_ref[...],
                                                       preferred_element_type=F32)
    x1 = x_ref[...] + jnp.dot(merged.astype(BF16), wo_ref[...], preferred_element_type=F32)
    x1_ref[...] = x1
    h2 = _rms(x1, gffn_ref[...])
    h2_ref[...] = h2

    logits = jnp.dot(h2, wr_ref[...], precision=lax.Precision.HIGHEST,
                     preferred_element_type=F32) + br_ref[...]
    col = lax.broadcasted_iota(jnp.int32, (tm, N_EXPERTS), 1)
    work = logits
    top_e, top_l = [], []
    for _ in range(TOP_K):
        mx = jnp.max(work, axis=-1, keepdims=True)
        idx = jnp.min(jnp.where(work == mx, col, N_EXPERTS), axis=-1, keepdims=True)
        top_e.append(idx)
        top_l.append(mx)
        work = jnp.where(col == idx, -jnp.inf, work)
    exps = [jnp.exp(v - top_l[0]) for v in top_l]
    denom = exps[0] + exps[1] + exps[2] + exps[3]

    onehots = [col == e for e in top_e]
    multi = jnp.zeros((tm, N_EXPERTS), F32)
    for oh in onehots:
        multi = multi + jnp.where(oh, 1.0, 0.0)
    r_i = lax.broadcasted_iota(jnp.int32, (tm, tm), 0)
    c_i = lax.broadcasted_iota(jnp.int32, (tm, tm), 1)
    lower = jnp.where(c_i < r_i, 1.0, 0.0).astype(BF16)
    pos = run_sc[...] + jnp.dot(lower, multi.astype(BF16), preferred_element_type=F32)
    run_sc[...] = run_sc[...] + jnp.sum(multi, axis=0, keepdims=True)
    cnt_ref[...] = run_sc[...]

    col4 = lax.broadcasted_iota(jnp.int32, (tm, TOP_K), 1)
    e_out = jnp.zeros((tm, TOP_K), jnp.int32)
    w_out = jnp.zeros((tm, TOP_K), F32)
    r_out = jnp.zeros((tm, TOP_K), F32)
    for k in range(TOP_K):
        rk = jnp.sum(jnp.where(onehots[k], pos, 0.0), axis=-1, keepdims=True)
        e_out = jnp.where(col4 == k, top_e[k], e_out)
        w_out = jnp.where(col4 == k, exps[k] / denom, w_out)
        r_out = jnp.where(col4 == k, rk, r_out)
    tope_ref[...] = e_out
    topw_ref[...] = w_out
    rank_ref[...] = r_out.astype(jnp.int32)


def _merge(proj, yb, x2, kvmem, ws, bsp, g_q_mem, wa, wb, wc, wo, g_ffn, w_router, b_router, S):
    T = x2.shape[0]
    tm = MERGE_TM
    per_b = S // tm
    full = lambda shape: pl.BlockSpec(shape, lambda i: (0,) * len(shape))
    in_specs = [
        pl.BlockSpec((tm, SPLIT), lambda i: (i, 0)),
        pl.BlockSpec((tm, SPLIT), lambda i: (i, 1)),
        pl.BlockSpec((tm, SPLIT), lambda i: (i, 0)),
        pl.BlockSpec((tm, SPLIT), lambda i: (i, 5)),
        pl.BlockSpec((tm, D_MODEL), lambda i: (i, 3)),
        pl.BlockSpec((tm, D_MODEL), lambda i: (i, 4)),
        pl.BlockSpec((tm, D_MODEL), lambda i: (i, 5)),
        pl.BlockSpec((tm, D_MODEL), lambda i: (i, 0)),
        pl.BlockSpec((MEM_LEN, 2 * MEM_HEADS * MEM_HEAD_DIM), lambda i: (i // per_b, 0)),
        full((SGU_GROUPS, SGU_CHUNK, SGU_CHUNK)),
        full((SGU_CHUNK, SGU_WIDTH)),
        full((1, MEM_HEAD_DIM)),
        full((SGU_WIDTH, D_MODEL)),
        full((SGU_WIDTH, D_MODEL)),
        full((SGU_WIDTH, D_MODEL)),
        full((D_MODEL, D_MODEL)),
        full((1, D_MODEL)),
        full((D_MODEL, N_EXPERTS)),
        full((1, N_EXPERTS)),
    ]
    out_shape = (
        jax.ShapeDtypeStruct((T, D_MODEL), F32),
        jax.ShapeDtypeStruct((T, D_MODEL), F32),
        jax.ShapeDtypeStruct((T, TOP_K), jnp.int32),
        jax.ShapeDtypeStruct((T, TOP_K), F32),
        jax.ShapeDtypeStruct((T, TOP_K), jnp.int32),
        jax.ShapeDtypeStruct((1, N_EXPERTS), F32),
    )
    out_specs = (
        pl.BlockSpec((tm, D_MODEL), lambda i: (i, 0)),
        pl.BlockSpec((tm, D_MODEL), lambda i: (i, 0)),
        pl.BlockSpec((tm, TOP_K), lambda i: (i, 0)),
        pl.BlockSpec((tm, TOP_K), lambda i: (i, 0)),
        pl.BlockSpec((tm, TOP_K), lambda i: (i, 0)),
        pl.BlockSpec((1, N_EXPERTS), lambda i: (0, 0)),
    )
    return pl.pallas_call(
        _merge_kernel,
        out_shape=out_shape,
        grid=(T // tm,),
        in_specs=in_specs,
        out_specs=out_specs,
        scratch_shapes=[pltpu.VMEM((1, N_EXPERTS), F32)],
        compiler_params=pltpu.CompilerParams(
            dimension_semantics=("arbitrary",), vmem_limit_bytes=VMEM_LIMIT),
        name="merge",
    )(proj, proj, yb, proj, proj, proj, proj, x2, kvmem, ws, bsp, g_q_mem,
      wa, wb, wc, wo, g_ffn, w_router, b_router)


def _dispatch_kernel(dest_ref, h_ref, xin_ref, xbuf_ref, sem):
    del xin_ref
    tm = h_ref.shape[0]

    def row_copy(r, d):
        return pltpu.make_async_copy(h_ref.at[pl.ds(r, 1)], xbuf_ref.at[pl.ds(d, 1)], sem)

    def start(r, carry):
        for k in range(TOP_K):
            row_copy(r, dest_ref[r * TOP_K + k]).start()
        return carry

    def wait(r, carry):
        for k in range(TOP_K):
            row_copy(r, dest_ref[r * TOP_K + k]).wait()
        return carry

    lax.fori_loop(0, tm, start, 0)
    lax.fori_loop(0, tm, wait, 0)


def _dispatch(dest_flat, h2, n_rows):
    T = h2.shape[0]
    tm = ROUTE_TM
    xbuf0 = jnp.zeros((n_rows, D_MODEL), F32)
    return pl.pallas_call(
        _dispatch_kernel,
        out_shape=jax.ShapeDtypeStruct((n_rows, D_MODEL), F32),
        grid=(T // tm,),
        in_specs=[
            pl.BlockSpec((tm * TOP_K,), lambda i: (i,), memory_space=pltpu.SMEM),
            pl.BlockSpec((tm, D_MODEL), lambda i: (i, 0)),
            pl.BlockSpec(memory_space=pl.ANY),
        ],
        out_specs=pl.BlockSpec(memory_space=pl.ANY),
        scratch_shapes=[pltpu.SemaphoreType.DMA],
        input_output_aliases={2: 0},
        compiler_params=pltpu.CompilerParams(
            dimension_semantics=("arbitrary",), vmem_limit_bytes=VMEM_LIMIT),
        name="dispatch",
    )(dest_flat, h2, xbuf0)


def _expert_kernel(be_ref, nused_ref, x_ref, w1_ref, b1_ref, w2_ref, b2_ref, y_ref):
    i = pl.program_id(0)

    @pl.when(i < nused_ref[0])
    def _():
        xb = x_ref[...].astype(BF16)
        hh = jnp.dot(xb, w1_ref[0], preferred_element_type=F32) + b1_ref[0]
        glu = jnp.minimum(hh[:, :D_FF], SWIGLU_LIMIT)
        lin = jnp.clip(hh[:, D_FF:], -SWIGLU_LIMIT, SWIGLU_LIMIT)
        act = glu * jax.nn.sigmoid(SWIGLU_ALPHA * glu) * (lin + 1.0)
        y_ref[...] = jnp.dot(act.astype(BF16), w2_ref[0], preferred_element_type=F32) + b2_ref[0]

    @pl.when(i >= nused_ref[0])
    def _():
        y_ref[...] = jnp.zeros_like(y_ref)


def _experts(block_e, n_used, xbuf, w1, b1, w2, b2):
    n_rows = xbuf.shape[0]
    bm = EXPERT_BM
    grid_spec = pltpu.PrefetchScalarGridSpec(
        num_scalar_prefetch=2,
        grid=(n_rows // bm,),
        in_specs=[
            pl.BlockSpec((bm, D_MODEL), lambda i, be, nu: (i, 0)),
            pl.BlockSpec((1, D_MODEL, 2 * D_FF), lambda i, be, nu: (be[i], 0, 0)),
            pl.BlockSpec((1, 1, 2 * D_FF), lambda i, be, nu: (be[i], 0, 0)),
            pl.BlockSpec((1, D_FF, D_MODEL), lambda i, be, nu: (be[i], 0, 0)),
            pl.BlockSpec((1, 1, D_MODEL), lambda i, be, nu: (be[i], 0, 0)),
        ],
        out_specs=pl.BlockSpec((bm, D_MODEL), lambda i, be, nu: (i, 0)),
    )
    return pl.pallas_call(
        _expert_kernel,
        out_shape=jax.ShapeDtypeStruct((n_rows, D_MODEL), F32),
        grid_spec=grid_spec,
        compiler_params=pltpu.CompilerParams(
            dimension_semantics=("arbitrary",), vmem_limit_bytes=VMEM_LIMIT),
        name="experts",
    )(block_e, n_used, xbuf, w1, b1, w2, b2)


def _combine_kernel(dest_ref, x1_ref, w_ref, ybuf_ref, o_ref, g_sc, sem):
    tm = x1_ref.shape[0]

    def row_copy(r, k, d):
        return pltpu.make_async_copy(ybuf_ref.at[pl.ds(d, 1)], g_sc.at[k, pl.ds(r, 1)], sem)

    def start(r, carry):
        for k in range(TOP_K):
            row_copy(r, k, dest_ref[r * TOP_K + k]).start()
        return carry

    def wait(r, carry):
        for k in range(TOP_K):
            row_copy(r, k, dest_ref[r * TOP_K + k]).wait()
        return carry

    lax.fori_loop(0, tm, start, 0)
    lax.fori_loop(0, tm, wait, 0)
    acc = x1_ref[...]
    w = w_ref[...]
    for k in range(TOP_K):
        acc = acc + w[:, k:k + 1] * g_sc[k]
    o_ref[...] = acc


def _combine(dest_flat, x1, top_w, ybuf):
    T = x1.shape[0]
    tm = ROUTE_TM
    return pl.pallas_call(
        _combine_kernel,
        out_shape=jax.ShapeDtypeStruct((T, D_MODEL), F32),
        grid=(T // tm,),
        in_specs=[
            pl.BlockSpec((tm * TOP_K,), lambda i: (i,), memory_space=pltpu.SMEM),
            pl.BlockSpec((tm, D_MODEL), lambda i: (i, 0)),
            pl.BlockSpec((tm, TOP_K), lambda i: (i, 0)),
            pl.BlockSpec(memory_space=pl.ANY),
        ],
        out_specs=pl.BlockSpec((tm, D_MODEL), lambda i: (i, 0)),
        scratch_shapes=[pltpu.VMEM((TOP_K, tm, D_MODEL), F32), pltpu.SemaphoreType.DMA],
        compiler_params=pltpu.CompilerParams(
            dimension_semantics=("arbitrary",), vmem_limit_bytes=VMEM_LIMIT),
        name="combine",
    )(dest_flat, x1, top_w, ybuf)


def _layer(x2, mem2, B, S, g_mix, w_in, g_sgu_v, w_spatial, b_spatial, g_q_moba, g_k_moba,
           g_mem, w_mem_kv, g_q_mem, g_k_mem, w_br_sgu, w_br_moba, w_br_mem, w_out,
           g_ffn, w_router, b_router, w_gate_up, b_gate_up, w_down, b_down):
    T = B * S
    row = lambda v: v.reshape(1, -1)
    proj = _inproj(x2, row(g_mix), w_in.astype(BF16), row(g_sgu_v))
    yb = _moba(proj, g_q_moba, g_k_moba, B, S)
    kvmem = _memkv(mem2, row(g_mem), w_mem_kv.astype(BF16), row(g_k_mem), B)

    causal = jnp.tril(jnp.ones((SGU_CHUNK, SGU_CHUNK), dtype=bool))
    ws = jnp.where(causal[None], w_spatial, 0.0).astype(BF16)
    bsp = jnp.repeat(b_spatial.T, LANES, axis=1)
    x1, h2, top_e, top_w, rank, counts = _merge(
        proj, yb, x2, kvmem, ws, bsp, row(g_q_mem), w_br_sgu.astype(BF16),
        w_br_moba.astype(BF16), w_br_mem.astype(BF16), w_out.astype(BF16), row(g_ffn),
        w_router, row(b_router), S)

    bm = EXPERT_BM
    n_blocks = -(-(T * TOP_K + N_EXPERTS * (bm - 1)) // bm)
    cnt = counts.reshape(N_EXPERTS).astype(jnp.int32)
    padded = (cnt + bm - 1) // bm * bm
    pad_ends = jnp.cumsum(padded)
    pad_starts = pad_ends - padded
    dest = (pad_starts[top_e] + rank).reshape(T * TOP_K)
    block_e = jnp.minimum(
        jnp.searchsorted(pad_ends, jnp.arange(n_blocks, dtype=jnp.int32) * bm, side='right'),
        N_EXPERTS - 1).astype(jnp.int32)
    n_used = (pad_ends[-1:] // bm).astype(jnp.int32)

    xbuf = _dispatch(dest, h2, n_blocks * bm)
    ybuf = _experts(block_e, n_used, xbuf, w_gate_up.astype(BF16),
                    b_gate_up.reshape(N_EXPERTS, 1, 2 * D_FF), w_down.astype(BF16),
                    b_down.reshape(N_EXPERTS, 1, D_MODEL))
    return _combine(dest, x1, top_w, ybuf)


def kernel(x, mem, g_mix, w_in, g_sgu_v, w_spatial, b_spatial, g_q_moba, g_k_moba, g_mem, w_mem_kv, g_q_mem, g_k_mem, w_br_sgu, w_br_moba, w_br_mem, w_out, g_ffn, w_router, b_router, w_gate_up, b_gate_up, w_down, b_down):
    B, S, D = x.shape
    x2 = x.reshape(B * S, D)
    mem2 = mem.reshape(B * MEM_LEN, D)
    params = (g_mix, w_in, g_sgu_v, w_spatial, b_spatial, g_q_moba, g_k_moba, g_mem, w_mem_kv,
              g_q_mem, g_k_mem, w_br_sgu, w_br_moba, w_br_mem, w_out, g_ffn, w_router, b_router,
              w_gate_up, b_gate_up, w_down, b_down)
    for l in range(g_mix.shape[0]):
        x2 = _layer(x2, mem2, B, S, *(p[l] for p in params))
    return x2.reshape(B, S, D)
```

```python
import functools

import numpy as np
import jax
import jax.numpy as jnp
from jax import lax
from jax.experimental import pallas as pl
from jax.experimental.pallas import tpu as pltpu

F32 = jnp.float32
BF16 = jnp.bfloat16

D_MODEL = 1024
MEM_LEN = 256
SGU_GROUPS = 4
SGU_CHUNK = 128
SGU_WIDTH = 512
MOBA_HEADS = 8
MOBA_HEAD_DIM = 64
MOBA_BLOCK = 256
MOBA_TOPK = 3
MEM_HEADS = 4
MEM_HEAD_DIM = 128
SPLIT = 512
IN_WIDTH = 6144
N_SPLITS = IN_WIDTH // SPLIT
N_EXPERTS = 32
TOP_K = 4
D_FF = 1024
SWIGLU_ALPHA = 1.702
SWIGLU_LIMIT = 7.0
EPS = 1e-6
NEG_INF = -1e30
LANES = 128

VMEM_LIMIT = 56 * 1024 * 1024

INPROJ_TM = 256
MERGE_TM = 256
ROUTE_TM = 256
EXPERT_BM = 256


def _gelu(x):
    return 0.5 * x * (1.0 + lax.erf(x * np.float32(np.sqrt(0.5))))


def _rms(x, g):
    return x * lax.rsqrt(jnp.mean(x * x, axis=-1, keepdims=True) + EPS) * g


def _inproj_kernel(x_ref, gmix_ref, w_ref, gsgu_ref, o_ref):
    h = _rms(x_ref[...], gmix_ref[...]).astype(BF16)
    for c in range(N_SPLITS):
        sl = slice(c * SPLIT, (c + 1) * SPLIT)
        acc = jnp.dot(h, w_ref[:, sl], preferred_element_type=F32)
        if c == 0:
            val = _gelu(acc)
        elif c == 1:
            val = _rms(_gelu(acc), gsgu_ref[...])
        elif c >= 6:
            val = jax.nn.sigmoid(acc)
        else:
            val = acc
        o_ref[:, sl] = val.astype(BF16)


def _inproj(x2, g_mix, w_in, g_sgu_v):
    T = x2.shape[0]
    tm = INPROJ_TM
    return pl.pallas_call(
        _inproj_kernel,
        out_shape=jax.ShapeDtypeStruct((T, IN_WIDTH), BF16),
        grid=(T // tm,),
        in_specs=[
            pl.BlockSpec((tm, D_MODEL), lambda i: (i, 0)),
            pl.BlockSpec((1, D_MODEL), lambda i: (0, 0)),
            pl.BlockSpec((D_MODEL, IN_WIDTH), lambda i: (0, 0)),
            pl.BlockSpec((1, SGU_WIDTH), lambda i: (0, 0)),
        ],
        out_specs=pl.BlockSpec((tm, IN_WIDTH), lambda i: (i, 0)),
        compiler_params=pltpu.CompilerParams(
            dimension_semantics=("arbitrary",), vmem_limit_bytes=VMEM_LIMIT),
        name="inproj",
    )(x2, g_mix, w_in, g_sgu_v)


def _pair_norm(x, g, head0):
    sq = x * x
    s0 = jnp.sum(jnp.where(head0, sq, 0.0), axis=-1, keepdims=True)
    s1 = jnp.sum(jnp.where(head0, 0.0, sq), axis=-1, keepdims=True)
    inv = jnp.where(head0, lax.rsqrt(s0 * (1.0 / MOBA_HEAD_DIM) + EPS),
                    lax.rsqrt(s1 * (1.0 / MOBA_HEAD_DIM) + EPS))
    return x * inv * g


VT_ROWS = MOBA_HEAD_DIM + 16
LOG2E = float(np.log2(np.e))


def _moba_kernel(slopes_ref, q_ref, k_ref, v_ref, gq_ref, gk_ref, bias_ref, o_ref,
                 kn_sc, vt_sc, kmean_sc, mrow_sc, qh_sc, m_sc, acc_sc, *, nb):
    qi = pl.program_id(1)
    bs = MOBA_BLOCK
    pairs = MOBA_HEADS // 2
    head0 = lax.broadcasted_iota(jnp.int32, (1, LANES), 1) < MOBA_HEAD_DIM
    nt = (((1,), (1,)), ((), ()))

    @pl.when(qi == 0)
    def _prepare_kv():
        ones_rows = jnp.where(
            lax.broadcasted_iota(jnp.int32, (VT_ROWS - MOBA_HEAD_DIM, bs), 0) == 0, 1.0, 0.0).astype(BF16)

        def body(j, carry):
            rows = pl.ds(pl.multiple_of(j * bs, bs), bs)
            means = []
            for g in range(pairs):
                cols = slice(g * LANES, (g + 1) * LANES)
                kn = _pair_norm(k_ref[rows, cols].astype(F32), gk_ref[...], head0)
                kn_sc[j, :, cols] = kn.astype(BF16)
                means.append(jnp.mean(kn, axis=0, keepdims=True))
                vt = v_ref[rows, cols].astype(F32).T.astype(BF16)
                for hh in range(2):
                    h = 2 * g + hh
                    vt_sc[j, h, :MOBA_HEAD_DIM, :] = vt[hh * MOBA_HEAD_DIM:(hh + 1) * MOBA_HEAD_DIM]
                    vt_sc[j, h, MOBA_HEAD_DIM:, :] = ones_rows
            kmean_sc[pl.ds(j, 1), :] = jnp.concatenate(means, axis=1)
            return carry
        lax.fori_loop(0, nb, body, 0)

    blk = lax.broadcasted_iota(jnp.int32, (nb, 1), 0)
    eligible = blk < qi
    dist = ((qi - blk) * bs).astype(F32)
    for g in range(pairs):
        cols = slice(g * LANES, (g + 1) * LANES)
        qn = _pair_norm(q_ref[:, cols].astype(F32), gq_ref[...], head0)
        for hh in range(2):
            h = 2 * g + hh
            hmask = head0 if hh == 0 else jnp.logical_not(head0)
            qh = jnp.where(hmask, qn, 0.0)
            gate = lax.dot_general(kmean_sc[:, cols], qh, nt,
                                   precision=lax.Precision.HIGHEST, preferred_element_type=F32)
            gate = jnp.where(eligible, gate, -jnp.inf)
            rank = jnp.zeros((nb, bs), F32)
            for i in range(nb - 1):
                gi = gate[i:i + 1, :]
                beats = jnp.logical_or(gi > gate, jnp.logical_and(gi == gate, blk > i))
                rank = rank + jnp.where(beats, 1.0, 0.0)
            sel = jnp.logical_and(eligible, rank < float(MOBA_TOPK))
            mrow_sc[:, h * bs:(h + 1) * bs] = jnp.where(sel, (-LOG2E * slopes_ref[h]) * dist, NEG_INF)
            qh_sc[g, hh * bs:(hh + 1) * bs, :] = (qh * (LOG2E * MOBA_HEAD_DIM ** -0.5)).astype(BF16)

    def attend(j, own):
        scores = [lax.dot_general(kn_sc[j, :, g * LANES:(g + 1) * LANES], qh_sc[g], nt,
                                  preferred_element_type=F32) for g in range(pairs)]
        t = jnp.concatenate(scores, axis=1) + bias_ref[1 if own else 0]
        cmax = jnp.max(t, axis=0, keepdims=True)
        if own:
            m_new = cmax
            shift = cmax
        else:
            mrow = mrow_sc[pl.ds(j, 1), :]
            m_old = m_sc[...]
            m_new = jnp.maximum(m_old, cmax + mrow)
            shift = m_new - mrow
            alpha = jnp.exp2(m_old - m_new)
        m_sc[...] = m_new
        for h in range(MOBA_HEADS):
            cols = slice(h * bs, (h + 1) * bs)
            p = jnp.exp2(t[:, cols] - shift[:, cols]).astype(BF16)
            pv = jnp.dot(vt_sc[j, h], p, preferred_element_type=F32)
            if own:
                acc_sc[h] = pv
            else:
                acc_sc[h] = alpha[:, cols] * acc_sc[h] + pv

    attend(qi, True)

    def past(j, carry):
        attend(j, False)
        return carry
    lax.fori_loop(0, qi, past, 0)

    for g in range(pairs):
        halves = []
        for hh in range(2):
            acc = acc_sc[2 * g + hh]
            halves.append(acc[:MOBA_HEAD_DIM] / acc[MOBA_HEAD_DIM:MOBA_HEAD_DIM + 1])
        out_t = jnp.concatenate(halves, axis=0)
        o_ref[:, g * LANES:(g + 1) * LANES] = out_t.T.astype(BF16)


def _moba_bias_tiles():
    slopes = 2.0 ** (-8.0 * np.arange(1, MOBA_HEADS + 1, dtype=np.float32) / MOBA_HEADS)
    kk = np.arange(MOBA_BLOCK)[:, None]
    qq = np.arange(MOBA_BLOCK)[None, :]
    rel = (qq - kk).astype(np.float64)
    past = -slopes.astype(np.float64)[:, None, None] * rel[None] * LOG2E
    own = np.where((kk <= qq)[None], past, NEG_INF)
    side_by_side = lambda a: a.transpose(1, 0, 2).reshape(MOBA_BLOCK, MOBA_HEADS * MOBA_BLOCK)
    return slopes.astype(np.float32), np.stack([side_by_side(past), side_by_side(own)]).astype(np.float32)


def _moba(proj, g_q, g_k, B, S):
    nb = S // MOBA_BLOCK
    width = MOBA_HEADS * MOBA_HEAD_DIM
    slopes, bias = _moba_bias_tiles()
    gq2 = jnp.tile(g_q.reshape(1, MOBA_HEAD_DIM), (1, 2))
    gk2 = jnp.tile(g_k.reshape(1, MOBA_HEAD_DIM), (1, 2))
    full = lambda shape: pl.BlockSpec(shape, lambda b, q, s: (0,) * len(shape))
    grid_spec = pltpu.PrefetchScalarGridSpec(
        num_scalar_prefetch=1,
        grid=(B, nb),
        in_specs=[
            pl.BlockSpec((MOBA_BLOCK, width), lambda b, q, s: (b * nb + q, 2)),
            pl.BlockSpec((S, width), lambda b, q, s: (b, 3)),
            pl.BlockSpec((S, width), lambda b, q, s: (b, 4)),
            full((1, LANES)),
            full((1, LANES)),
            full((2, MOBA_BLOCK, MOBA_HEADS * MOBA_BLOCK)),
        ],
        out_specs=pl.BlockSpec((MOBA_BLOCK, width), lambda b, q, s: (b * nb + q, 0)),
        scratch_shapes=[
            pltpu.VMEM((nb, MOBA_BLOCK, width), BF16),
            pltpu.VMEM((nb, MOBA_HEADS, VT_ROWS, MOBA_BLOCK), BF16),
            pltpu.VMEM((nb, width), F32),
            pltpu.VMEM((nb, MOBA_HEADS * MOBA_BLOCK), F32),
            pltpu.VMEM((MOBA_HEADS // 2, 2 * MOBA_BLOCK, LANES), BF16),
            pltpu.VMEM((1, MOBA_HEADS * MOBA_BLOCK), F32),
            pltpu.VMEM((MOBA_HEADS, VT_ROWS, MOBA_BLOCK), F32),
        ],
    )
    return pl.pallas_call(
        functools.partial(_moba_kernel, nb=nb),
        out_shape=jax.ShapeDtypeStruct((B * S, width), BF16),
        grid_spec=grid_spec,
        compiler_params=pltpu.CompilerParams(
            dimension_semantics=("arbitrary", "arbitrary"), vmem_limit_bytes=VMEM_LIMIT),
        name="moba",
    )(jnp.asarray(slopes), proj, proj, proj, gq2, gk2, jnp.asarray(bias))


def _memkv_kernel(mem_ref, gmem_ref, w_ref, gk_ref, o_ref):
    mn = _rms(mem_ref[...], gmem_ref[...]).astype(BF16)
    kv = jnp.dot(mn, w_ref[...], preferred_element_type=F32)
    width = MEM_HEADS * MEM_HEAD_DIM
    for h in range(MEM_HEADS):
        sl = slice(h * MEM_HEAD_DIM, (h + 1) * MEM_HEAD_DIM)
        o_ref[:, sl] = _rms(kv[:, sl], gk_ref[...]).astype(BF16)
    o_ref[:, width:] = kv[:, width:].astype(BF16)


def _memkv(mem2, g_mem, w_mem_kv, g_k_mem, B):
    width = MEM_HEADS * MEM_HEAD_DIM
    return pl.pallas_call(
        _memkv_kernel,
        out_shape=jax.ShapeDtypeStruct((B * MEM_LEN, 2 * width), BF16),
        grid=(B,),
        in_specs=[
            pl.BlockSpec((MEM_LEN, D_MODEL), lambda b: (b, 0)),
            pl.BlockSpec((1, D_MODEL), lambda b: (0, 0)),
            pl.BlockSpec((D_MODEL, 2 * width), lambda b: (0, 0)),
            pl.BlockSpec((1, MEM_HEAD_DIM), lambda b: (0, 0)),
        ],
        out_specs=pl.BlockSpec((MEM_LEN, 2 * width), lambda b: (b, 0)),
        compiler_params=pltpu.CompilerParams(
            dimension_semantics=("arbitrary",), vmem_limit_bytes=VMEM_LIMIT),
        name="memkv",
    )(mem2, g_mem, w_mem_kv, g_k_mem)


def _merge_kernel(gu_ref, vn_ref, yb_ref, qm_ref, ga_ref, gb_ref, gc_ref, x_ref, kv_ref,
                  ws_ref, bsp_ref, gqm_ref, wa_ref, wb_ref, wc_ref, wo_ref, gffn_ref,
                  wr_ref, br_ref,
                  x1_ref, h2_ref, tope_ref, topw_ref, rank_ref, cnt_ref, run_sc):
    i = pl.program_id(0)
    tm = x_ref.shape[0]

    @pl.when(i == 0)
    def _():
        run_sc[...] = jnp.zeros_like(run_sc)

    ya_chunks = []
    for c in range(tm // SGU_CHUNK):
        rows = slice(c * SGU_CHUNK, (c + 1) * SGU_CHUNK)
        groups = []
        for g in range(SGU_GROUPS):
            cols = slice(g * LANES, (g + 1) * LANES)
            groups.append(jnp.dot(ws_ref[g], vn_ref[rows, cols], preferred_element_type=F32))
        mixed = jnp.concatenate(groups, axis=1) + bsp_ref[...]
        ya_chunks.append(gu_ref[rows, :].astype(F32) * mixed)
    y_a = jnp.concatenate(ya_chunks, axis=0).astype(BF16)

    width = MEM_HEADS * MEM_HEAD_DIM
    yc_heads = []
    for h in range(MEM_HEADS):
        cols = slice(h * MEM_HEAD_DIM, (h + 1) * MEM_HEAD_DIM)
        qn = _rms(qm_ref[:, cols].astype(F32), gqm_ref[...]).astype(BF16)
        s = lax.dot_general(qn, kv_ref[:, cols], (((1,), (1,)), ((), ())),
                            preferred_element_type=F32) * (MEM_HEAD_DIM ** -0.5)
        s = s - jnp.max(s, axis=-1, keepdims=True)
        p = jnp.exp(s)
        p = p / jnp.sum(p, axis=-1, keepdims=True)
        vcols = slice(width + h * MEM_HEAD_DIM, width + (h + 1) * MEM_HEAD_DIM)
        yc_heads.append(jnp.dot(p.astype(BF16), kv_ref[:, vcols], preferred_element_type=F32))
    y_c = jnp.concatenate(yc_heads, axis=1).astype(BF16)

    merged = ga_ref[...].astype(F32) * jnp.dot(y_a, wa_ref[...], preferred_element_type=F32)
    merged = merged + gb_ref[...].astype(F32) * jnp.dot(yb_ref[...], wb_ref[...],
                                                       preferred_element_type=F32)
    merged = merged + gc_ref[...].astype(F32) * jnp.dot(y_c, wc_ref[...],
                                                       preferred_element_type=F32)
    x1 = x_ref[...] + jnp.dot(merged.astype(BF16), wo_ref[...], preferred_element_type=F32)
    x1_ref[...] = x1
    h2 = _rms(x1, gffn_ref[...])
    h2_ref[...] = h2

    logits = jnp.dot(h2, wr_ref[...], precision=lax.Precision.HIGHEST,
                     preferred_element_type=F32) + br_ref[...]
    col = lax.broadcasted_iota(jnp.int32, (tm, N_EXPERTS), 1)
    work = logits
    top_e, top_l = [], []
    for _ in range(TOP_K):
        mx = jnp.max(work, axis=-1, keepdims=True)
        idx = jnp.min(jnp.where(work == mx, col, N_EXPERTS), axis=-1, keepdims=True)
        top_e.append(idx)
        top_l.append(mx)
        work = jnp.where(col == idx, -jnp.inf, work)
    exps = [jnp.exp(v - top_l[0]) for v in top_l]
    denom = exps[0] + exps[1] + exps[2] + exps[3]

    onehots = [col == e for e in top_e]
    multi = jnp.zeros((tm, N_EXPERTS), F32)
    for oh in onehots:
        multi = multi + jnp.where(oh, 1.0, 0.0)
    r_i = lax.broadcasted_iota(jnp.int32, (tm, tm), 0)
    c_i = lax.broadcasted_iota(jnp.int32, (tm, tm), 1)
    lower = jnp.where(c_i < r_i, 1.0, 0.0).astype(BF16)
    pos = run_sc[...] + jnp.dot(lower, multi.astype(BF16), preferred_element_type=F32)
    run_sc[...] = run_sc[...] + jnp.sum(multi, axis=0, keepdims=True)
    cnt_ref[...] = run_sc[...]

    col4 = lax.broadcasted_iota(jnp.int32, (tm, TOP_K), 1)
    e_out = jnp.zeros((tm, TOP_K), jnp.int32)
    w_out = jnp.zeros((tm, TOP_K), F32)
    r_out = jnp.zeros((tm, TOP_K), F32)
    for k in range(TOP_K):
        rk = jnp.sum(jnp.where(onehots[k], pos, 0.0), axis=-1, keepdims=True)
        e_out = jnp.where(col4 == k, top_e[k], e_out)
        w_out = jnp.where(col4 == k, exps[k] / denom, w_out)
        r_out = jnp.where(col4 == k, rk, r_out)
    tope_ref[...] = e_out
    topw_ref[...] = w_out
    rank_ref[...] = r_out.astype(jnp.int32)


def _merge(proj, yb, x2, kvmem, ws, bsp, g_q_mem, wa, wb, wc, wo, g_ffn, w_router, b_router, S):
    T = x2.shape[0]
    tm = MERGE_TM
    per_b = S // tm
    full = lambda shape: pl.BlockSpec(shape, lambda i: (0,) * len(shape))
    in_specs = [
        pl.BlockSpec((tm, SPLIT), lambda i: (i, 0)),
        pl.BlockSpec((tm, SPLIT), lambda i: (i, 1)),
        pl.BlockSpec((tm, SPLIT), lambda i: (i, 0)),
        pl.BlockSpec((tm, SPLIT), lambda i: (i, 5)),
        pl.BlockSpec((tm, D_MODEL), lambda i: (i, 3)),
        pl.BlockSpec((tm, D_MODEL), lambda i: (i, 4)),
        pl.BlockSpec((tm, D_MODEL), lambda i: (i, 5)),
        pl.BlockSpec((tm, D_MODEL), lambda i: (i, 0)),
        pl.BlockSpec((MEM_LEN, 2 * MEM_HEADS * MEM_HEAD_DIM), lambda i: (i // per_b, 0)),
        full((SGU_GROUPS, SGU_CHUNK, SGU_CHUNK)),
        full((SGU_CHUNK, SGU_WIDTH)),
        full((1, MEM_HEAD_DIM)),
        full((SGU_WIDTH, D_MODEL)),
        full((SGU_WIDTH, D_MODEL)),
        full((SGU_WIDTH, D_MODEL)),
        full((D_MODEL, D_MODEL)),
        full((1, D_MODEL)),
        full((D_MODEL, N_EXPERTS)),
        full((1, N_EXPERTS)),
    ]
    out_shape = (
        jax.ShapeDtypeStruct((T, D_MODEL), F32),
        jax.ShapeDtypeStruct((T, D_MODEL), F32),
        jax.ShapeDtypeStruct((T, TOP_K), jnp.int32),
        jax.ShapeDtypeStruct((T, TOP_K), F32),
        jax.ShapeDtypeStruct((T, TOP_K), jnp.int32),
        jax.ShapeDtypeStruct((1, N_EXPERTS), F32),
    )
    out_specs = (
        pl.BlockSpec((tm, D_MODEL), lambda i: (i, 0)),
        pl.BlockSpec((tm, D_MODEL), lambda i: (i, 0)),
        pl.BlockSpec((tm, TOP_K), lambda i: (i, 0)),
        pl.BlockSpec((tm, TOP_K), lambda i: (i, 0)),
        pl.BlockSpec((tm, TOP_K), lambda i: (i, 0)),
        pl.BlockSpec((1, N_EXPERTS), lambda i: (0, 0)),
    )
    return pl.pallas_call(
        _merge_kernel,
        out_shape=out_shape,
        grid=(T // tm,),
        in_specs=in_specs,
        out_specs=out_specs,
        scratch_shapes=[pltpu.VMEM((1, N_EXPERTS), F32)],
        compiler_params=pltpu.CompilerParams(
            dimension_semantics=("arbitrary",), vmem_limit_bytes=VMEM_LIMIT),
        name="merge",
    )(proj, proj, yb, proj, proj, proj, proj, x2, kvmem, ws, bsp, g_q_mem,
      wa, wb, wc, wo, g_ffn, w_router, b_router)


def _dispatch_kernel(dest_ref, h_ref, xin_ref, xbuf_ref, sem):
    del xin_ref
    tm = h_ref.shape[0]

    def row_copy(r, d):
        return pltpu.make_async_copy(h_ref.at[pl.ds(r, 1)], xbuf_ref.at[pl.ds(d, 1)], sem)

    def start(r, carry):
        for k in range(TOP_K):
            row_copy(r, dest_ref[r * TOP_K + k]).start(priority=k % 2)
        return carry

    def wait(r, carry):
        for k in range(TOP_K):
            row_copy(r, dest_ref[r * TOP_K + k]).wait()
        return carry

    lax.fori_loop(0, tm, start, 0)
    lax.fori_loop(0, tm, wait, 0)


def _dispatch(dest_flat, h2, n_rows):
    T = h2.shape[0]
    tm = ROUTE_TM
    xbuf0 = jnp.zeros((n_rows, D_MODEL), F32)
    return pl.pallas_call(
        _dispatch_kernel,
        out_shape=jax.ShapeDtypeStruct((n_rows, D_MODEL), F32),
        grid=(T // tm,),
        in_specs=[
            pl.BlockSpec((tm * TOP_K,), lambda i: (i,), memory_space=pltpu.SMEM),
            pl.BlockSpec((tm, D_MODEL), lambda i: (i, 0)),
            pl.BlockSpec(memory_space=pl.ANY),
        ],
        out_specs=pl.BlockSpec(memory_space=pl.ANY),
        scratch_shapes=[pltpu.SemaphoreType.DMA],
        input_output_aliases={2: 0},
        compiler_params=pltpu.CompilerParams(
            dimension_semantics=("arbitrary",), vmem_limit_bytes=VMEM_LIMIT),
        name="dispatch",
    )(dest_flat, h2, xbuf0)


def _expert_kernel(be_ref, nused_ref, x_ref, w1_ref, b1_ref, w2_ref, b2_ref, y_ref):
    i = pl.program_id(0)

    @pl.when(i < nused_ref[0])
    def _():
        xb = x_ref[...].astype(BF16)
        hh = jnp.dot(xb, w1_ref[0], preferred_element_type=F32) + b1_ref[0]
        glu = jnp.minimum(hh[:, :D_FF], SWIGLU_LIMIT)
        lin = jnp.clip(hh[:, D_FF:], -SWIGLU_LIMIT, SWIGLU_LIMIT)
        act = glu * jax.nn.sigmoid(SWIGLU_ALPHA * glu) * (lin + 1.0)
        y_ref[...] = jnp.dot(act.astype(BF16), w2_ref[0], preferred_element_type=F32) + b2_ref[0]

    @pl.when(i >= nused_ref[0])
    def _():
        y_ref[...] = jnp.zeros_like(y_ref)


def _experts(block_e, n_used, xbuf, w1, b1, w2, b2):
    n_rows = xbuf.shape[0]
    bm = EXPERT_BM
    grid_spec = pltpu.PrefetchScalarGridSpec(
        num_scalar_prefetch=2,
        grid=(n_rows // bm,),
        in_specs=[
            pl.BlockSpec((bm, D_MODEL), lambda i, be, nu: (i, 0)),
            pl.BlockSpec((1, D_MODEL, 2 * D_FF), lambda i, be, nu: (be[i], 0, 0)),
            pl.BlockSpec((1, 1, 2 * D_FF), lambda i, be, nu: (be[i], 0, 0)),
            pl.BlockSpec((1, D_FF, D_MODEL), lambda i, be, nu: (be[i], 0, 0)),
            pl.BlockSpec((1, 1, D_MODEL), lambda i, be, nu: (be[i], 0, 0)),
        ],
        out_specs=pl.BlockSpec((bm, D_MODEL), lambda i, be, nu: (i, 0)),
    )
    return pl.pallas_call(
        _expert_kernel,
        out_shape=jax.ShapeDtypeStruct((n_rows, D_MODEL), F32),
        grid_spec=grid_spec,
        compiler_params=pltpu.CompilerParams(
            dimension_semantics=("arbitrary",), vmem_limit_bytes=VMEM_LIMIT),
        name="experts",
    )(block_e, n_used, xbuf, w1, b1, w2, b2)


def _combine_kernel(dest_ref, x1_ref, w_ref, ybuf_ref, o_ref, g_sc, sem):
    tm = x1_ref.shape[0]

    def row_copy(r, k, d):
        return pltpu.make_async_copy(ybuf_ref.at[pl.ds(d, 1)], g_sc.at[k, pl.ds(r, 1)], sem)

    def start(r, carry):
        for k in range(TOP_K):
            row_copy(r, k, dest_ref[r * TOP_K + k]).start(priority=k % 2)
        return carry

    def wait(r, carry):
        for k in range(TOP_K):
            row_copy(r, k, dest_ref[r * TOP_K + k]).wait()
        return carry

    lax.fori_loop(0, tm, start, 0)
    lax.fori_loop(0, tm, wait, 0)
    acc = x1_ref[...]
    w = w_ref[...]
    for k in range(TOP_K):
        acc = acc + w[:, k:k + 1] * g_sc[k]
    o_ref[...] = acc


def _combine(dest_flat, x1, top_w, ybuf):
    T = x1.shape[0]
    tm = ROUTE_TM
    return pl.pallas_call(
        _combine_kernel,
        out_shape=jax.ShapeDtypeStruct((T, D_MODEL), F32),
        grid=(T // tm,),
        in_specs=[
            pl.BlockSpec((tm * TOP_K,), lambda i: (i,), memory_space=pltpu.SMEM),
            pl.BlockSpec((tm, D_MODEL), lambda i: (i, 0)),
            pl.BlockSpec((tm, TOP_K), lambda i: (i, 0)),
            pl.BlockSpec(memory_space=pl.ANY),
        ],
        out_specs=pl.BlockSpec((tm, D_MODEL), lambda i: (i, 0)),
        scratch_shapes=[pltpu.VMEM((TOP_K, tm, D_MODEL), F32), pltpu.SemaphoreType.DMA],
        compiler_params=pltpu.CompilerParams(
            dimension_semantics=("arbitrary",), vmem_limit_bytes=VMEM_LIMIT),
        name="combine",
    )(dest_flat, x1, top_w, ybuf)


def _layer(x2, mem2, B, S, g_mix, w_in, g_sgu_v, w_spatial, b_spatial, g_q_moba, g_k_moba,
           g_mem, w_mem_kv, g_q_mem, g_k_mem, w_br_sgu, w_br_moba, w_br_mem, w_out,
           g_ffn, w_router, b_router, w_gate_up, b_gate_up, w_down, b_down):
    T = B * S
    row = lambda v: v.reshape(1, -1)
    proj = _inproj(x2, row(g_mix), w_in.astype(BF16), row(g_sgu_v))
    yb = _moba(proj, g_q_moba, g_k_moba, B, S)
    kvmem = _memkv(mem2, row(g_mem), w_mem_kv.astype(BF16), row(g_k_mem), B)

    causal = jnp.tril(jnp.ones((SGU_CHUNK, SGU_CHUNK), dtype=bool))
    ws = jnp.where(causal[None], w_spatial, 0.0).astype(BF16)
    bsp = jnp.repeat(b_spatial.T, LANES, axis=1)
    x1, h2, top_e, top_w, rank, counts = _merge(
        proj, yb, x2, kvmem, ws, bsp, row(g_q_mem), w_br_sgu.astype(BF16),
        w_br_moba.astype(BF16), w_br_mem.astype(BF16), w_out.astype(BF16), row(g_ffn),
        w_router, row(b_router), S)

    bm = EXPERT_BM
    n_blocks = -(-(T * TOP_K + N_EXPERTS * (bm - 1)) // bm)
    cnt = counts.reshape(N_EXPERTS).astype(jnp.int32)
    padded = (cnt + bm - 1) // bm * bm
    pad_ends = jnp.cumsum(padded)
    pad_starts = pad_ends - padded
    dest = (pad_starts[top_e] + rank).reshape(T * TOP_K)
    block_row0 = jnp.arange(n_blocks, dtype=jnp.int32) * bm
    block_e = jnp.minimum(
        jnp.sum((pad_ends[None, :] <= block_row0[:, None]).astype(jnp.int32), axis=1), N_EXPERTS - 1)
    n_used = (pad_ends[-1:] // bm).astype(jnp.int32)

    xbuf = _dispatch(dest, h2, n_blocks * bm)
    ybuf = _experts(block_e, n_used, xbuf, w_gate_up.astype(BF16),
                    b_gate_up.reshape(N_EXPERTS, 1, 2 * D_FF), w_down.astype(BF16),
                    b_down.reshape(N_EXPERTS, 1, D_MODEL))
    return _combine(dest, x1, top_w, ybuf)


def kernel(x, mem, g_mix, w_in, g_sgu_v, w_spatial, b_spatial, g_q_moba, g_k_moba, g_mem, w_mem_kv, g_q_mem, g_k_mem, w_br_sgu, w_br_moba, w_br_mem, w_out, g_ffn, w_router, b_router, w_gate_up, b_gate_up, w_down, b_down):
    B, S, D = x.shape
    x2 = x.reshape(B * S, D)
    mem2 = mem.reshape(B * MEM_LEN, D)
    params = (g_mix, w_in, g_sgu_v, w_spatial, b_spatial, g_q_moba, g_k_moba, g_mem, w_mem_kv,
              g_q_mem, g_k_mem, w_br_sgu, w_br_moba, w_br_mem, w_out, g_ffn, w_router, b_router,
              w_gate_up, b_gate_up, w_down, b_down)
    for l in range(g_mix.shape[0]):
        x2 = _layer(x2, mem2, B, S, *(p[l] for p in params))
    return x2.reshape(B, S, D)
```

```python
import functools

import numpy as np
import jax
import jax.numpy as jnp
from jax import lax
from jax.experimental import pallas as pl
from jax.experimental.pallas import tpu as pltpu

F32 = jnp.float32
BF16 = jnp.bfloat16

D_MODEL = 1024
MEM_LEN = 256
SGU_GROUPS = 4
SGU_CHUNK = 128
SGU_WIDTH = 512
MOBA_HEADS = 8
MOBA_HEAD_DIM = 64
MOBA_BLOCK = 256
MOBA_TOPK = 3
MEM_HEADS = 4
MEM_HEAD_DIM = 128
SPLIT = 512
IN_WIDTH = 6144
N_SPLITS = IN_WIDTH // SPLIT
N_EXPERTS = 32
TOP_K = 4
D_FF = 1024
SWIGLU_ALPHA = 1.702
SWIGLU_LIMIT = 7.0
EPS = 1e-6
NEG_INF = -1e30
LANES = 128

VMEM_LIMIT = 56 * 1024 * 1024

INPROJ_TM = 256
MERGE_TM = 512
ROUTE_TM = 256
EXPERT_BM = 512


def _gelu(x):
    return 0.5 * x * (1.0 + lax.erf(x * np.float32(np.sqrt(0.5))))


def _rms(x, g):
    return x * lax.rsqrt(jnp.mean(x * x, axis=-1, keepdims=True) + EPS) * g


def _inproj_kernel(x_ref, gmix_ref, w_ref, gsgu_ref, o_ref):
    h = _rms(x_ref[...], gmix_ref[...]).astype(BF16)
    for c in range(N_SPLITS):
        sl = slice(c * SPLIT, (c + 1) * SPLIT)
        acc = jnp.dot(h, w_ref[:, sl], preferred_element_type=F32)
        if c == 0:
            val = _gelu(acc)
        elif c == 1:
            val = _rms(_gelu(acc), gsgu_ref[...])
        elif c >= 6:
            val = jax.nn.sigmoid(acc)
        else:
            val = acc
        o_ref[:, sl] = val.astype(BF16)


def _inproj(x2, g_mix, w_in, g_sgu_v):
    T = x2.shape[0]
    tm = INPROJ_TM
    return pl.pallas_call(
        _inproj_kernel,
        out_shape=jax.ShapeDtypeStruct((T, IN_WIDTH), BF16),
        grid=(T // tm,),
        in_specs=[
            pl.BlockSpec((tm, D_MODEL), lambda i: (i, 0)),
            pl.BlockSpec((1, D_MODEL), lambda i: (0, 0)),
            pl.BlockSpec((D_MODEL, IN_WIDTH), lambda i: (0, 0)),
            pl.BlockSpec((1, SGU_WIDTH), lambda i: (0, 0)),
        ],
        out_specs=pl.BlockSpec((tm, IN_WIDTH), lambda i: (i, 0)),
        compiler_params=pltpu.CompilerParams(
            dimension_semantics=("arbitrary",), vmem_limit_bytes=VMEM_LIMIT),
        name="inproj",
    )(x2, g_mix, w_in, g_sgu_v)


def _pair_norm(x, g, head0):
    sq = x * x
    s0 = jnp.sum(jnp.where(head0, sq, 0.0), axis=-1, keepdims=True)
    s1 = jnp.sum(jnp.where(head0, 0.0, sq), axis=-1, keepdims=True)
    inv = jnp.where(head0, lax.rsqrt(s0 * (1.0 / MOBA_HEAD_DIM) + EPS),
                    lax.rsqrt(s1 * (1.0 / MOBA_HEAD_DIM) + EPS))
    return x * inv * g


VT_ROWS = MOBA_HEAD_DIM + 16
LOG2E = float(np.log2(np.e))


def _moba_kernel(slopes_ref, q_ref, k_ref, v_ref, gq_ref, gk_ref, bias_ref, o_ref,
                 kn_sc, vt_sc, kmean_sc, mrow_sc, qh_sc, m_sc, acc_sc, *, nb):
    qi = pl.program_id(1)
    bs = MOBA_BLOCK
    pairs = MOBA_HEADS // 2
    head0 = lax.broadcasted_iota(jnp.int32, (1, LANES), 1) < MOBA_HEAD_DIM
    nt = (((1,), (1,)), ((), ()))

    @pl.when(qi == 0)
    def _prepare_kv():
        ones_rows = jnp.where(
            lax.broadcasted_iota(jnp.int32, (VT_ROWS - MOBA_HEAD_DIM, bs), 0) == 0, 1.0, 0.0).astype(BF16)

        def body(j, carry):
            rows = pl.ds(pl.multiple_of(j * bs, bs), bs)
            means = []
            for g in range(pairs):
                cols = slice(g * LANES, (g + 1) * LANES)
                kn = _pair_norm(k_ref[rows, cols].astype(F32), gk_ref[...], head0)
                kn_sc[j, :, cols] = kn.astype(BF16)
                means.append(jnp.mean(kn, axis=0, keepdims=True))
                vt = v_ref[rows, cols].astype(F32).T.astype(BF16)
                for hh in range(2):
                    h = 2 * g + hh
                    vt_sc[j, h, :MOBA_HEAD_DIM, :] = vt[hh * MOBA_HEAD_DIM:(hh + 1) * MOBA_HEAD_DIM]
                    vt_sc[j, h, MOBA_HEAD_DIM:, :] = ones_rows
            kmean_sc[pl.ds(j, 1), :] = jnp.concatenate(means, axis=1)
            return carry
        lax.fori_loop(0, nb, body, 0)

    blk = lax.broadcasted_iota(jnp.int32, (nb, 1), 0)
    eligible = blk < qi
    dist = ((qi - blk) * bs).astype(F32)
    for g in range(pairs):
        cols = slice(g * LANES, (g + 1) * LANES)
        qn = _pair_norm(q_ref[:, cols].astype(F32), gq_ref[...], head0)
        for hh in range(2):
            h = 2 * g + hh
            hmask = head0 if hh == 0 else jnp.logical_not(head0)
            qh = jnp.where(hmask, qn, 0.0)
            gate = lax.dot_general(kmean_sc[:, cols], qh, nt,
                                   precision=lax.Precision.HIGHEST, preferred_element_type=F32)
            gate = jnp.where(eligible, gate, -jnp.inf)
            rank = jnp.zeros((nb, bs), F32)
            for i in range(nb - 1):
                gi = gate[i:i + 1, :]
                beats = jnp.logical_or(gi > gate, jnp.logical_and(gi == gate, blk > i))
                rank = rank + jnp.where(beats, 1.0, 0.0)
            sel = jnp.logical_and(eligible, rank < float(MOBA_TOPK))
            mrow_sc[:, h * bs:(h + 1) * bs] = jnp.where(sel, (-LOG2E * slopes_ref[h]) * dist, NEG_INF)
            qh_sc[g, hh * bs:(hh + 1) * bs, :] = (qh * (LOG2E * MOBA_HEAD_DIM ** -0.5)).astype(BF16)

    def attend(j, own):
        scores = [lax.dot_general(kn_sc[j, :, g * LANES:(g + 1) * LANES], qh_sc[g], nt,
                                  preferred_element_type=F32) for g in range(pairs)]
        t = jnp.concatenate(scores, axis=1) + bias_ref[1 if own else 0]
        cmax = jnp.max(t, axis=0, keepdims=True)
        if own:
            m_new = cmax
            shift = cmax
        else:
            mrow = mrow_sc[pl.ds(j, 1), :]
            m_old = m_sc[...]
            m_new = jnp.maximum(m_old, cmax + mrow)
            shift = m_new - mrow
            alpha = jnp.exp2(m_old - m_new)
        m_sc[...] = m_new
        for h in range(MOBA_HEADS):
            cols = slice(h * bs, (h + 1) * bs)
            p = jnp.exp2(t[:, cols] - shift[:, cols]).astype(BF16)
            pv = jnp.dot(vt_sc[j, h], p, preferred_element_type=F32)
            if own:
                acc_sc[h] = pv
            else:
                acc_sc[h] = alpha[:, cols] * acc_sc[h] + pv

    attend(qi, True)

    def past(j, carry):
        attend(j, False)
        return carry
    lax.fori_loop(0, qi, past, 0)

    for g in range(pairs):
        halves = []
        for hh in range(2):
            acc = acc_sc[2 * g + hh]
            halves.append(acc[:MOBA_HEAD_DIM] / acc[MOBA_HEAD_DIM:MOBA_HEAD_DIM + 1])
        out_t = jnp.concatenate(halves, axis=0)
        o_ref[:, g * LANES:(g + 1) * LANES] = out_t.T.astype(BF16)


def _moba_bias_tiles():
    slopes = 2.0 ** (-8.0 * np.arange(1, MOBA_HEADS + 1, dtype=np.float32) / MOBA_HEADS)
    kk = np.arange(MOBA_BLOCK)[:, None]
    qq = np.arange(MOBA_BLOCK)[None, :]
    rel = (qq - kk).astype(np.float64)
    past = -slopes.astype(np.float64)[:, None, None] * rel[None] * LOG2E
    own = np.where((kk <= qq)[None], past, NEG_INF)
    side_by_side = lambda a: a.transpose(1, 0, 2).reshape(MOBA_BLOCK, MOBA_HEADS * MOBA_BLOCK)
    return slopes.astype(np.float32), np.stack([side_by_side(past), side_by_side(own)]).astype(np.float32)


def _moba(proj, g_q, g_k, B, S):
    nb = S // MOBA_BLOCK
    width = MOBA_HEADS * MOBA_HEAD_DIM
    slopes, bias = _moba_bias_tiles()
    gq2 = jnp.tile(g_q.reshape(1, MOBA_HEAD_DIM), (1, 2))
    gk2 = jnp.tile(g_k.reshape(1, MOBA_HEAD_DIM), (1, 2))
    full = lambda shape: pl.BlockSpec(shape, lambda b, q, s: (0,) * len(shape))
    grid_spec = pltpu.PrefetchScalarGridSpec(
        num_scalar_prefetch=1,
        grid=(B, nb),
        in_specs=[
            pl.BlockSpec((MOBA_BLOCK, width), lambda b, q, s: (b * nb + q, 2)),
            pl.BlockSpec((S, width), lambda b, q, s: (b, 3)),
            pl.BlockSpec((S, width), lambda b, q, s: (b, 4)),
            full((1, LANES)),
            full((1, LANES)),
            full((2, MOBA_BLOCK, MOBA_HEADS * MOBA_BLOCK)),
        ],
        out_specs=pl.BlockSpec((MOBA_BLOCK, width), lambda b, q, s: (b * nb + q, 0)),
        scratch_shapes=[
            pltpu.VMEM((nb, MOBA_BLOCK, width), BF16),
            pltpu.VMEM((nb, MOBA_HEADS, VT_ROWS, MOBA_BLOCK), BF16),
            pltpu.VMEM((nb, width), F32),
            pltpu.VMEM((nb, MOBA_HEADS * MOBA_BLOCK), F32),
            pltpu.VMEM((MOBA_HEADS // 2, 2 * MOBA_BLOCK, LANES), BF16),
            pltpu.VMEM((1, MOBA_HEADS * MOBA_BLOCK), F32),
            pltpu.VMEM((MOBA_HEADS, VT_ROWS, MOBA_BLOCK), F32),
        ],
    )
    return pl.pallas_call(
        functools.partial(_moba_kernel, nb=nb),
        out_shape=jax.ShapeDtypeStruct((B * S, width), BF16),
        grid_spec=grid_spec,
        compiler_params=pltpu.CompilerParams(
            dimension_semantics=("arbitrary", "arbitrary"), vmem_limit_bytes=VMEM_LIMIT),
        name="moba",
    )(jnp.asarray(slopes), proj, proj, proj, gq2, gk2, jnp.asarray(bias))


def _memkv_kernel(mem_ref, gmem_ref, w_ref, gk_ref, o_ref):
    mn = _rms(mem_ref[...], gmem_ref[...]).astype(BF16)
    kv = jnp.dot(mn, w_ref[...], preferred_element_type=F32)
    width = MEM_HEADS * MEM_HEAD_DIM
    for h in range(MEM_HEADS):
        sl = slice(h * MEM_HEAD_DIM, (h + 1) * MEM_HEAD_DIM)
        o_ref[:, sl] = _rms(kv[:, sl], gk_ref[...]).astype(BF16)
    o_ref[:, width:] = kv[:, width:].astype(BF16)


def _memkv(mem2, g_mem, w_mem_kv, g_k_mem, B):
    width = MEM_HEADS * MEM_HEAD_DIM
    return pl.pallas_call(
        _memkv_kernel,
        out_shape=jax.ShapeDtypeStruct((B * MEM_LEN, 2 * width), BF16),
        grid=(B,),
        in_specs=[
            pl.BlockSpec((MEM_LEN, D_MODEL), lambda b: (b, 0)),
            pl.BlockSpec((1, D_MODEL), lambda b: (0, 0)),
            pl.BlockSpec((D_MODEL, 2 * width), lambda b: (0, 0)),
            pl.BlockSpec((1, MEM_HEAD_DIM), lambda b: (0, 0)),
        ],
        out_specs=pl.BlockSpec((MEM_LEN, 2 * width), lambda b: (b, 0)),
        compiler_params=pltpu.CompilerParams(
            dimension_semantics=("arbitrary",), vmem_limit_bytes=VMEM_LIMIT),
        name="memkv",
    )(mem2, g_mem, w_mem_kv, g_k_mem)


def _merge_kernel(gu_ref, vn_ref, yb_ref, qm_ref, ga_ref, gb_ref, gc_ref, x_ref, kv_ref,
                  ws_ref, bsp_ref, gqm_ref, wa_ref, wb_ref, wc_ref, wo_ref, gffn_ref,
                  wrh_ref, wrl_ref, br_ref,
                  x1_ref, h2_ref, tope_ref, topw_ref, rank_ref, cnt_ref, run_sc):
    i = pl.program_id(0)
    tm = x_ref.shape[0]

    @pl.when(i == 0)
    def _():
        run_sc[...] = jnp.zeros_like(run_sc)

    ya_chunks = []
    for c in range(tm // SGU_CHUNK):
        rows = slice(c * SGU_CHUNK, (c + 1) * SGU_CHUNK)
        groups = []
        for g in range(SGU_GROUPS):
            cols = slice(g * LANES, (g + 1) * LANES)
            groups.append(jnp.dot(ws_ref[g], vn_ref[rows, cols], preferred_element_type=F32))
        mixed = jnp.concatenate(groups, axis=1) + bsp_ref[...]
        ya_chunks.append(gu_ref[rows, :].astype(F32) * mixed)
    y_a = jnp.concatenate(ya_chunks, axis=0).astype(BF16)

    width = MEM_HEADS * MEM_HEAD_DIM
    yc_heads = []
    for h in range(MEM_HEADS):
        cols = slice(h * MEM_HEAD_DIM, (h + 1) * MEM_HEAD_DIM)
        qn = _rms(qm_ref[:, cols].astype(F32), gqm_ref[...]).astype(BF16)
        s = lax.dot_general(qn, kv_ref[:, cols], (((1,), (1,)), ((), ())),
                            preferred_element_type=F32) * (MEM_HEAD_DIM ** -0.5)
        s = s - jnp.max(s, axis=-1, keepdims=True)
        p = jnp.exp(s)
        p = p / jnp.sum(p, axis=-1, keepdims=True)
        vcols = slice(width + h * MEM_HEAD_DIM, width + (h + 1) * MEM_HEAD_DIM)
        yc_heads.append(jnp.dot(p.astype(BF16), kv_ref[:, vcols], preferred_element_type=F32))
    y_c = jnp.concatenate(yc_heads, axis=1).astype(BF16)

    merged = ga_ref[...].astype(F32) * jnp.dot(y_a, wa_ref[...], preferred_element_type=F32)
    merged = merged + gb_ref[...].astype(F32) * jnp.dot(yb_ref[...], wb_ref[...],
                                                       preferred_element_type=F32)
    merged = merged + gc_ref[...].astype(F32) * jnp.dot(y_c, wc_ref[...],
                                                       preferred_element_type=F32)
    x1 = x_ref[...] + jnp.dot(merged.astype(BF16), wo_ref[...], preferred_element_type=F32)
    x1_ref[...] = x1
    h2 = _rms(x1, gffn_ref[...])
    h2_hi = h2.astype(BF16)
    h2_hi32 = h2_hi.astype(F32)
    h2_lo = (h2 - h2_hi32).astype(BF16)
    bits = lax.bitcast_convert_type(h2_hi32, jnp.uint32)
    half = D_MODEL // 2
    h2_ref[...] = (bits[:, :half] >> 16) | (bits[:, half:] & jnp.uint32(0xFFFF0000))

    nt = (((1,), (1,)), ((), ()))
    logits = (lax.dot_general(wrh_ref[...], h2_hi, nt, preferred_element_type=F32)
              + lax.dot_general(wrh_ref[...], h2_lo, nt, preferred_element_type=F32)
              + lax.dot_general(wrl_ref[...], h2_hi, nt, preferred_element_type=F32)
              + br_ref[...])
    row = lax.broadcasted_iota(jnp.int32, (N_EXPERTS, tm), 0)
    work = logits
    top_e, top_l = [], []
    for _ in range(TOP_K):
        mx = jnp.max(work, axis=0, keepdims=True)
        idx = jnp.min(jnp.where(work == mx, row, N_EXPERTS), axis=0, keepdims=True)
        top_e.append(idx)
        top_l.append(mx)
        work = jnp.where(row == idx, -jnp.inf, work)
    exps = [jnp.exp(v - top_l[0]) for v in top_l]
    denom = exps[0] + exps[1] + exps[2] + exps[3]

    onehots = [row == e for e in top_e]
    multi = jnp.zeros((N_EXPERTS, tm), F32)
    for oh in onehots:
        multi = multi + jnp.where(oh, 1.0, 0.0)
    r_i = lax.broadcasted_iota(jnp.int32, (tm, tm), 0)
    c_i = lax.broadcasted_iota(jnp.int32, (tm, tm), 1)
    earlier = jnp.where(r_i < c_i, 1.0, 0.0).astype(BF16)
    pos = run_sc[...] + jnp.dot(multi.astype(BF16), earlier, preferred_element_type=F32)
    run_sc[...] = run_sc[...] + jnp.sum(multi, axis=1, keepdims=True)
    cnt_ref[...] = run_sc[...]

    for k in range(TOP_K):
        tope_ref[k:k + 1, :] = top_e[k]
        topw_ref[k:k + 1, :] = exps[k] / denom
        rank_ref[k:k + 1, :] = jnp.sum(jnp.where(onehots[k], pos, 0.0), axis=0,
                                       keepdims=True).astype(jnp.int32)


def _merge(proj, yb, x2, kvmem, ws, bsp, g_q_mem, wa, wb, wc, wo, g_ffn, w_router, b_router, S):
    T = x2.shape[0]
    tm = MERGE_TM
    per_b = S // tm
    full = lambda shape: pl.BlockSpec(shape, lambda i: (0,) * len(shape))
    in_specs = [
        pl.BlockSpec((tm, SPLIT), lambda i: (i, 0)),
        pl.BlockSpec((tm, SPLIT), lambda i: (i, 1)),
        pl.BlockSpec((tm, SPLIT), lambda i: (i, 0)),
        pl.BlockSpec((tm, SPLIT), lambda i: (i, 5)),
        pl.BlockSpec((tm, D_MODEL), lambda i: (i, 3)),
        pl.BlockSpec((tm, D_MODEL), lambda i: (i, 4)),
        pl.BlockSpec((tm, D_MODEL), lambda i: (i, 5)),
        pl.BlockSpec((tm, D_MODEL), lambda i: (i, 0)),
        pl.BlockSpec((MEM_LEN, 2 * MEM_HEADS * MEM_HEAD_DIM), lambda i: (i // per_b, 0)),
        full((SGU_GROUPS, SGU_CHUNK, SGU_CHUNK)),
        full((SGU_CHUNK, SGU_WIDTH)),
        full((1, MEM_HEAD_DIM)),
        full((SGU_WIDTH, D_MODEL)),
        full((SGU_WIDTH, D_MODEL)),
        full((SGU_WIDTH, D_MODEL)),
        full((D_MODEL, D_MODEL)),
        full((1, D_MODEL)),
        full((N_EXPERTS, D_MODEL)),
        full((N_EXPERTS, D_MODEL)),
        full((N_EXPERTS, 1)),
    ]
    out_shape = (
        jax.ShapeDtypeStruct((T, D_MODEL), F32),
        jax.ShapeDtypeStruct((T, D_MODEL // 2), jnp.uint32),
        jax.ShapeDtypeStruct((TOP_K, T), jnp.int32),
        jax.ShapeDtypeStruct((TOP_K, T), F32),
        jax.ShapeDtypeStruct((TOP_K, T), jnp.int32),
        jax.ShapeDtypeStruct((N_EXPERTS, 1), F32),
    )
    out_specs = (
        pl.BlockSpec((tm, D_MODEL), lambda i: (i, 0)),
        pl.BlockSpec((tm, D_MODEL // 2), lambda i: (i, 0)),
        pl.BlockSpec((TOP_K, tm), lambda i: (0, i)),
        pl.BlockSpec((TOP_K, tm), lambda i: (0, i)),
        pl.BlockSpec((TOP_K, tm), lambda i: (0, i)),
        pl.BlockSpec((N_EXPERTS, 1), lambda i: (0, 0)),
    )
    wr_t = w_router.T
    wr_hi = wr_t.astype(BF16)
    wr_lo = (wr_t - wr_hi.astype(F32)).astype(BF16)
    return pl.pallas_call(
        _merge_kernel,
        out_shape=out_shape,
        grid=(T // tm,),
        in_specs=in_specs,
        out_specs=out_specs,
        scratch_shapes=[pltpu.VMEM((N_EXPERTS, 1), F32)],
        compiler_params=pltpu.CompilerParams(
            dimension_semantics=("arbitrary",), vmem_limit_bytes=VMEM_LIMIT),
        name="merge",
    )(proj, proj, yb, proj, proj, proj, proj, x2, kvmem, ws, bsp, g_q_mem,
      wa, wb, wc, wo, g_ffn, wr_hi, wr_lo, b_router.reshape(N_EXPERTS, 1))


def _dispatch_kernel(dest_ref, h_ref, xin_ref, xbuf_ref, sem):
    del xin_ref
    tm = h_ref.shape[0]

    def row_copy(r, d):
        return pltpu.make_async_copy(h_ref.at[pl.ds(r, 1)], xbuf_ref.at[pl.ds(d, 1)], sem)

    def start(r, carry):
        for k in range(TOP_K):
            row_copy(r, dest_ref[r * TOP_K + k]).start(priority=k % 2)
        return carry

    lax.fori_loop(0, tm, start, 0, unroll=8)
    for _ in range(TOP_K):
        pltpu.make_async_copy(h_ref, xbuf_ref.at[pl.ds(0, tm)], sem).wait()


def _dispatch(dest_flat, h2, n_rows):
    T, width = h2.shape
    tm = ROUTE_TM
    xbuf0 = jnp.zeros((n_rows, width), h2.dtype)
    return pl.pallas_call(
        _dispatch_kernel,
        out_shape=jax.ShapeDtypeStruct((n_rows, width), h2.dtype),
        grid=(T // tm,),
        in_specs=[
            pl.BlockSpec((tm * TOP_K,), lambda i: (i,), memory_space=pltpu.SMEM),
            pl.BlockSpec((tm, width), lambda i: (i, 0)),
            pl.BlockSpec(memory_space=pl.ANY),
        ],
        out_specs=pl.BlockSpec(memory_space=pl.ANY),
        scratch_shapes=[pltpu.SemaphoreType.DMA],
        input_output_aliases={2: 0},
        compiler_params=pltpu.CompilerParams(
            dimension_semantics=("arbitrary",), vmem_limit_bytes=VMEM_LIMIT),
        name="dispatch",
    )(dest_flat, h2, xbuf0)


FF_CHUNK = 512


def _expert_kernel(be_ref, nused_ref, x_ref, w1_ref, b1_ref, w2_ref, b2_ref, y_ref, w1_sc, w2_sc):
    i = pl.program_id(0)

    @pl.when(i < nused_ref[0])
    def _():
        @pl.when(jnp.logical_or(i == 0, be_ref[i] != be_ref[jnp.maximum(i - 1, 0)]))
        def _():
            w1_sc[...] = w1_ref[0].astype(BF16)
            w2_sc[...] = w2_ref[0].astype(BF16)

        packed = x_ref[...]
        lo = lax.bitcast_convert_type(packed << 16, F32)
        hi = lax.bitcast_convert_type(packed & jnp.uint32(0xFFFF0000), F32)
        xb = jnp.concatenate([lo, hi], axis=1).astype(BF16)
        y = b2_ref[0]
        for c in range(D_FF // FF_CHUNK):
            g_cols = slice(c * FF_CHUNK, (c + 1) * FF_CHUNK)
            l_cols = slice(D_FF + c * FF_CHUNK, D_FF + (c + 1) * FF_CHUNK)
            glu = jnp.dot(xb, w1_sc[:, g_cols], preferred_element_type=F32) + b1_ref[0, :, g_cols]
            lin = jnp.dot(xb, w1_sc[:, l_cols], preferred_element_type=F32) + b1_ref[0, :, l_cols]
            glu = jnp.minimum(glu, SWIGLU_LIMIT)
            lin = jnp.clip(lin, -SWIGLU_LIMIT, SWIGLU_LIMIT)
            act = glu * jax.nn.sigmoid(SWIGLU_ALPHA * glu) * (lin + 1.0)
            y = y + jnp.dot(act.astype(BF16), w2_sc[g_cols, :], preferred_element_type=F32)
        y_ref[...] = y

    @pl.when(i >= nused_ref[0])
    def _():
        y_ref[...] = jnp.zeros_like(y_ref)


def _experts(block_e, n_used, xbuf, w1, b1, w2, b2):
    n_rows, width = xbuf.shape
    bm = EXPERT_BM
    grid_spec = pltpu.PrefetchScalarGridSpec(
        num_scalar_prefetch=2,
        grid=(n_rows // bm,),
        in_specs=[
            pl.BlockSpec((bm, width), lambda i, be, nu: (i, 0)),
            pl.BlockSpec((1, D_MODEL, 2 * D_FF), lambda i, be, nu: (be[i], 0, 0)),
            pl.BlockSpec((1, 1, 2 * D_FF), lambda i, be, nu: (be[i], 0, 0)),
            pl.BlockSpec((1, D_FF, D_MODEL), lambda i, be, nu: (be[i], 0, 0)),
            pl.BlockSpec((1, 1, D_MODEL), lambda i, be, nu: (be[i], 0, 0)),
        ],
        out_specs=pl.BlockSpec((bm, D_MODEL), lambda i, be, nu: (i, 0)),
        scratch_shapes=[pltpu.VMEM((D_MODEL, 2 * D_FF), BF16), pltpu.VMEM((D_FF, D_MODEL), BF16)],
    )
    return pl.pallas_call(
        _expert_kernel,
        out_shape=jax.ShapeDtypeStruct((n_rows, D_MODEL), F32),
        grid_spec=grid_spec,
        compiler_params=pltpu.CompilerParams(
            dimension_semantics=("arbitrary",), vmem_limit_bytes=VMEM_LIMIT),
        name="experts",
    )(block_e, n_used, xbuf, w1, b1, w2, b2)


def _combine_kernel(dest_ref, x1_ref, w_ref, ybuf_ref, o_ref, g_sc, sem):
    tm = x1_ref.shape[0]

    def row_copy(r, k, d):
        return pltpu.make_async_copy(ybuf_ref.at[pl.ds(d, 1)], g_sc.at[k, pl.ds(r, 1)], sem)

    def start(r, carry):
        for k in range(TOP_K):
            row_copy(r, k, dest_ref[r * TOP_K + k]).start(priority=k % 2)
        return carry

    lax.fori_loop(0, tm, start, 0, unroll=8)
    for k in range(TOP_K):
        pltpu.make_async_copy(ybuf_ref.at[pl.ds(0, tm)], g_sc.at[k], sem).wait()
    acc = x1_ref[...]
    w = w_ref[...]
    for k in range(TOP_K):
        acc = acc + w[:, k:k + 1] * g_sc[k]
    o_ref[...] = acc


def _combine(dest_flat, x1, top_w, ybuf):
    T = x1.shape[0]
    tm = ROUTE_TM
    return pl.pallas_call(
        _combine_kernel,
        out_shape=jax.ShapeDtypeStruct((T, D_MODEL), F32),
        grid=(T // tm,),
        in_specs=[
            pl.BlockSpec((tm * TOP_K,), lambda i: (i,), memory_space=pltpu.SMEM),
            pl.BlockSpec((tm, D_MODEL), lambda i: (i, 0)),
            pl.BlockSpec((tm, TOP_K), lambda i: (i, 0)),
            pl.BlockSpec(memory_space=pl.ANY),
        ],
        out_specs=pl.BlockSpec((tm, D_MODEL), lambda i: (i, 0)),
        scratch_shapes=[pltpu.VMEM((TOP_K, tm, D_MODEL), F32), pltpu.SemaphoreType.DMA],
        compiler_params=pltpu.CompilerParams(
            dimension_semantics=("arbitrary",), vmem_limit_bytes=VMEM_LIMIT),
        name="combine",
    )(dest_flat, x1, top_w, ybuf)


def _layer(x2, mem2, B, S, g_mix, w_in, g_sgu_v, w_spatial, b_spatial, g_q_moba, g_k_moba,
           g_mem, w_mem_kv, g_q_mem, g_k_mem, w_br_sgu, w_br_moba, w_br_mem, w_out,
           g_ffn, w_router, b_router, w_gate_up, b_gate_up, w_down, b_down):
    T = B * S
    row = lambda v: v.reshape(1, -1)
    proj = _inproj(x2, row(g_mix), w_in.astype(BF16), row(g_sgu_v))
    yb = _moba(proj, g_q_moba, g_k_moba, B, S)
    kvmem = _memkv(mem2, row(g_mem), w_mem_kv.astype(BF16), row(g_k_mem), B)

    causal = jnp.tril(jnp.ones((SGU_CHUNK, SGU_CHUNK), dtype=bool))
    ws = jnp.where(causal[None], w_spatial, 0.0).astype(BF16)
    bsp = jnp.repeat(b_spatial.T, LANES, axis=1)
    x1, h2, top_e, top_w, rank, counts = _merge(
        proj, yb, x2, kvmem, ws, bsp, row(g_q_mem), w_br_sgu.astype(BF16),
        w_br_moba.astype(BF16), w_br_mem.astype(BF16), w_out.astype(BF16), row(g_ffn),
        w_router, b_router, S)

    bm = EXPERT_BM
    n_blocks = -(-(T * TOP_K + N_EXPERTS * (bm - 1)) // bm)
    cnt = counts.reshape(N_EXPERTS).astype(jnp.int32)
    padded = (cnt + bm - 1) // bm * bm
    pad_ends = jnp.cumsum(padded)
    pad_starts = pad_ends - padded
    dest = (pad_starts[top_e] + rank).T.reshape(T * TOP_K)
    top_w = top_w.T
    block_row0 = jnp.arange(n_blocks, dtype=jnp.int32) * bm
    block_e = jnp.minimum(
        jnp.sum((pad_ends[None, :] <= block_row0[:, None]).astype(jnp.int32), axis=1), N_EXPERTS - 1)
    n_used = (pad_ends[-1:] // bm).astype(jnp.int32)

    xbuf = _dispatch(dest, h2, n_blocks * bm)
    ybuf = _experts(block_e, n_used, xbuf, w_gate_up, b_gate_up.reshape(N_EXPERTS, 1, 2 * D_FF),
                    w_down, b_down.reshape(N_EXPERTS, 1, D_MODEL))
    return _combine(dest, x1, top_w, ybuf)


def kernel(x, mem, g_mix, w_in, g_sgu_v, w_spatial, b_spatial, g_q_moba, g_k_moba, g_mem, w_mem_kv, g_q_mem, g_k_mem, w_br_sgu, w_br_moba, w_br_mem, w_out, g_ffn, w_router, b_router, w_gate_up, b_gate_up, w_down, b_down):
    B, S, D = x.shape
    x2 = x.reshape(B * S, D)
    mem2 = mem.reshape(B * MEM_LEN, D)
    params = (g_mix, w_in, g_sgu_v, w_spatial, b_spatial, g_q_moba, g_k_moba, g_mem, w_mem_kv,
              g_q_mem, g_k_mem, w_br_sgu, w_br_moba, w_br_mem, w_out, g_ffn, w_router, b_router,
              w_gate_up, b_gate_up, w_down, b_down)
    for l in range(g_mix.shape[0]):
        x2 = _layer(x2, mem2, B, S, *(p[l] for p in params))
    return x2.reshape(B, S, D)
```

```python
import functools

import numpy as np
import jax
import jax.numpy as jnp
from jax import lax
from jax.experimental import pallas as pl
from jax.experimental.pallas import tpu as pltpu

F32 = jnp.float32
BF16 = jnp.bfloat16

D_MODEL = 1024
MEM_LEN = 256
SGU_GROUPS = 4
SGU_CHUNK = 128
SGU_WIDTH = 512
MOBA_HEADS = 8
MOBA_HEAD_DIM = 64
MOBA_BLOCK = 256
MOBA_TOPK = 3
MEM_HEADS = 4
MEM_HEAD_DIM = 128
SPLIT = 512
IN_WIDTH = 6144
N_SPLITS = IN_WIDTH // SPLIT
N_EXPERTS = 32
TOP_K = 4
D_FF = 1024
SWIGLU_ALPHA = 1.702
SWIGLU_LIMIT = 7.0
EPS = 1e-6
NEG_INF = -1e30
LANES = 128

VMEM_LIMIT = 56 * 1024 * 1024

INPROJ_TM = 512
MERGE_TM = 512
ROUTE_TM = 256
EXPERT_BM = 512


def _gelu(x):
    return 0.5 * x * (1.0 + lax.erf(x * np.float32(np.sqrt(0.5))))


def _rms(x, g):
    return x * lax.rsqrt(jnp.mean(x * x, axis=-1, keepdims=True) + EPS) * g


def _inproj_kernel(x_ref, gmix_ref, w_ref, gsgu_ref, o_ref):
    h = _rms(x_ref[...], gmix_ref[...]).astype(BF16)
    for c in range(N_SPLITS):
        sl = slice(c * SPLIT, (c + 1) * SPLIT)
        acc = jnp.dot(h, w_ref[:, sl], preferred_element_type=F32)
        if c == 0:
            val = _gelu(acc)
        elif c == 1:
            val = _rms(_gelu(acc), gsgu_ref[...])
        elif c >= 6:
            val = jax.nn.sigmoid(acc)
        else:
            val = acc
        o_ref[:, sl] = val.astype(BF16)


def _inproj(x2, g_mix, w_in, g_sgu_v):
    T = x2.shape[0]
    tm = INPROJ_TM
    return pl.pallas_call(
        _inproj_kernel,
        out_shape=jax.ShapeDtypeStruct((T, IN_WIDTH), BF16),
        grid=(T // tm,),
        in_specs=[
            pl.BlockSpec((tm, D_MODEL), lambda i: (i, 0)),
            pl.BlockSpec((1, D_MODEL), lambda i: (0, 0)),
            pl.BlockSpec((D_MODEL, IN_WIDTH), lambda i: (0, 0)),
            pl.BlockSpec((1, SGU_WIDTH), lambda i: (0, 0)),
        ],
        out_specs=pl.BlockSpec((tm, IN_WIDTH), lambda i: (i, 0)),
        compiler_params=pltpu.CompilerParams(
            dimension_semantics=("arbitrary",), vmem_limit_bytes=VMEM_LIMIT),
        name="inproj",
    )(x2, g_mix, w_in, g_sgu_v)


def _pair_norm(x, g, head0):
    sq = x * x
    s0 = jnp.sum(jnp.where(head0, sq, 0.0), axis=-1, keepdims=True)
    s1 = jnp.sum(jnp.where(head0, 0.0, sq), axis=-1, keepdims=True)
    inv = jnp.where(head0, lax.rsqrt(s0 * (1.0 / MOBA_HEAD_DIM) + EPS),
                    lax.rsqrt(s1 * (1.0 / MOBA_HEAD_DIM) + EPS))
    return x * inv * g


VT_ROWS = MOBA_HEAD_DIM + 16
LOG2E = float(np.log2(np.e))


def _moba_kernel(slopes_ref, q_ref, k_ref, v_ref, gq_ref, gk_ref, bias_ref, o_ref,
                 kn_sc, vt_sc, kmean_sc, mrow_sc, qh_sc, m_sc, acc_sc, t_sc, cmax_sc, *, nb):
    qi = pl.program_id(1)
    bs = MOBA_BLOCK
    pairs = MOBA_HEADS // 2
    head0 = lax.broadcasted_iota(jnp.int32, (1, LANES), 1) < MOBA_HEAD_DIM
    nt = (((1,), (1,)), ((), ()))

    @pl.when(qi == 0)
    def _prepare_kv():
        ones_rows = jnp.where(
            lax.broadcasted_iota(jnp.int32, (VT_ROWS - MOBA_HEAD_DIM, bs), 0) == 0, 1.0, 0.0).astype(BF16)

        def body(j, carry):
            rows = pl.ds(pl.multiple_of(j * bs, bs), bs)
            means = []
            for g in range(pairs):
                cols = slice(g * LANES, (g + 1) * LANES)
                kn = _pair_norm(k_ref[rows, cols].astype(F32), gk_ref[...], head0)
                kn_sc[j, :, cols] = kn.astype(BF16)
                means.append(jnp.mean(kn, axis=0, keepdims=True))
                vt = v_ref[rows, cols].astype(F32).T.astype(BF16)
                for hh in range(2):
                    h = 2 * g + hh
                    vt_sc[j, h, :MOBA_HEAD_DIM, :] = vt[hh * MOBA_HEAD_DIM:(hh + 1) * MOBA_HEAD_DIM]
                    vt_sc[j, h, MOBA_HEAD_DIM:, :] = ones_rows
            kmean_sc[pl.ds(j, 1), :] = jnp.concatenate(means, axis=1)
            return carry
        lax.fori_loop(0, nb, body, 0)

    blk = lax.broadcasted_iota(jnp.int32, (nb, 1), 0)
    eligible = blk < qi
    dist = ((qi - blk) * bs).astype(F32)
    for g in range(pairs):
        cols = slice(g * LANES, (g + 1) * LANES)
        qn = _pair_norm(q_ref[:, cols].astype(F32), gq_ref[...], head0)
        qpair = jnp.concatenate([jnp.where(head0, qn, 0.0), jnp.where(head0, 0.0, qn)], axis=0)
        qh_sc[g] = (qpair * (LOG2E * MOBA_HEAD_DIM ** -0.5)).astype(BF16)
        q_hi = qpair.astype(BF16)
        q_lo = (qpair - q_hi.astype(F32)).astype(BF16)
        km = kmean_sc[:, cols]
        km_hi = km.astype(BF16)
        km_lo = (km - km_hi.astype(F32)).astype(BF16)
        gate = (lax.dot_general(km_hi, q_hi, nt, preferred_element_type=F32)
                + lax.dot_general(km_hi, q_lo, nt, preferred_element_type=F32)
                + lax.dot_general(km_lo, q_hi, nt, preferred_element_type=F32))
        gate = jnp.where(eligible, gate, -jnp.inf)
        rank = jnp.zeros((nb, 2 * bs), F32)
        for i in range(nb - 1):
            gi = gate[i:i + 1, :]
            beats = jnp.logical_or(gi > gate, jnp.logical_and(gi == gate, blk > i))
            rank = rank + jnp.where(beats, 1.0, 0.0)
        sel = jnp.logical_and(eligible, rank < float(MOBA_TOPK))
        for hh in range(2):
            h = 2 * g + hh
            mrow_sc[:, h * bs:(h + 1) * bs] = jnp.where(
                blk == qi, 0.0,
                jnp.where(sel[:, hh * bs:(hh + 1) * bs], (-LOG2E * slopes_ref[h]) * dist, NEG_INF))
    m_sc[...] = jnp.full(m_sc.shape, NEG_INF, F32)
    acc_sc[...] = jnp.zeros(acc_sc.shape, F32)

    def score(j, own, slot):
        scores = [lax.dot_general(kn_sc[j, :, g * LANES:(g + 1) * LANES], qh_sc[g], nt,
                                  preferred_element_type=F32) for g in range(pairs)]
        t = jnp.concatenate(scores, axis=1) + bias_ref[1 if own else 0]
        t_sc[slot] = t
        cmax_sc[slot] = jnp.max(t, axis=0, keepdims=True)

    def accumulate(j, slot):
        mrow = mrow_sc[pl.ds(j, 1), :]
        m_old = m_sc[...]
        m_new = jnp.maximum(m_old, cmax_sc[slot] + mrow)
        shift = m_new - mrow
        alpha = jnp.exp2(m_old - m_new)
        m_sc[...] = m_new
        for h in range(MOBA_HEADS):
            cols = slice(h * bs, (h + 1) * bs)
            p = jnp.exp2(t_sc[slot, :, cols] - shift[:, cols]).astype(BF16)
            pv = jnp.dot(vt_sc[j, h], p, preferred_element_type=F32)
            acc_sc[h] = alpha[:, cols] * acc_sc[h] + pv

    n_past = qi
    score(qi, True, 0)

    def two_blocks(p, carry):
        score(2 * p, False, 1)
        accumulate(jnp.where(p == 0, qi, 2 * p - 1), 0)
        score(2 * p + 1, False, 0)
        accumulate(2 * p, 1)
        return carry
    lax.fori_loop(0, n_past // 2, two_blocks, 0)

    @pl.when(n_past % 2 == 0)
    def _():
        accumulate(jnp.where(n_past == 0, qi, n_past - 1), 0)

    @pl.when(n_past % 2 == 1)
    def _():
        score(n_past - 1, False, 1)
        accumulate(jnp.where(n_past == 1, qi, n_past - 2), 0)
        accumulate(n_past - 1, 1)

    for g in range(pairs):
        halves = []
        for hh in range(2):
            acc = acc_sc[2 * g + hh]
            halves.append(acc[:MOBA_HEAD_DIM] / acc[MOBA_HEAD_DIM:MOBA_HEAD_DIM + 1])
        out_t = jnp.concatenate(halves, axis=0)
        o_ref[:, g * LANES:(g + 1) * LANES] = out_t.T.astype(BF16)


def _moba_bias_tiles():
    slopes = 2.0 ** (-8.0 * np.arange(1, MOBA_HEADS + 1, dtype=np.float32) / MOBA_HEADS)
    kk = np.arange(MOBA_BLOCK)[:, None]
    qq = np.arange(MOBA_BLOCK)[None, :]
    rel = (qq - kk).astype(np.float64)
    past = -slopes.astype(np.float64)[:, None, None] * rel[None] * LOG2E
    own = np.where((kk <= qq)[None], past, NEG_INF)
    side_by_side = lambda a: a.transpose(1, 0, 2).reshape(MOBA_BLOCK, MOBA_HEADS * MOBA_BLOCK)
    return slopes.astype(np.float32), np.stack([side_by_side(past), side_by_side(own)]).astype(np.float32)


def _moba(proj, g_q, g_k, B, S):
    nb = S // MOBA_BLOCK
    width = MOBA_HEADS * MOBA_HEAD_DIM
    slopes, bias = _moba_bias_tiles()
    gq2 = jnp.tile(g_q.reshape(1, MOBA_HEAD_DIM), (1, 2))
    gk2 = jnp.tile(g_k.reshape(1, MOBA_HEAD_DIM), (1, 2))
    full = lambda shape: pl.BlockSpec(shape, lambda b, q, s: (0,) * len(shape))
    grid_spec = pltpu.PrefetchScalarGridSpec(
        num_scalar_prefetch=1,
        grid=(B, nb),
        in_specs=[
            pl.BlockSpec((MOBA_BLOCK, width), lambda b, q, s: (b * nb + q, 2)),
            pl.BlockSpec((S, width), lambda b, q, s: (b, 3)),
            pl.BlockSpec((S, width), lambda b, q, s: (b, 4)),
            full((1, LANES)),
            full((1, LANES)),
            full((2, MOBA_BLOCK, MOBA_HEADS * MOBA_BLOCK)),
        ],
        out_specs=pl.BlockSpec((MOBA_BLOCK, width), lambda b, q, s: (b * nb + q, 0)),
        scratch_shapes=[
            pltpu.VMEM((nb, MOBA_BLOCK, width), BF16),
            pltpu.VMEM((nb, MOBA_HEADS, VT_ROWS, MOBA_BLOCK), BF16),
            pltpu.VMEM((nb, width), F32),
            pltpu.VMEM((nb, MOBA_HEADS * MOBA_BLOCK), F32),
            pltpu.VMEM((MOBA_HEADS // 2, 2 * MOBA_BLOCK, LANES), BF16),
            pltpu.VMEM((1, MOBA_HEADS * MOBA_BLOCK), F32),
            pltpu.VMEM((MOBA_HEADS, VT_ROWS, MOBA_BLOCK), F32),
            pltpu.VMEM((2, MOBA_BLOCK, MOBA_HEADS * MOBA_BLOCK), F32),
            pltpu.VMEM((2, 1, MOBA_HEADS * MOBA_BLOCK), F32),
        ],
    )
    return pl.pallas_call(
        functools.partial(_moba_kernel, nb=nb),
        out_shape=jax.ShapeDtypeStruct((B * S, width), BF16),
        grid_spec=grid_spec,
        compiler_params=pltpu.CompilerParams(
            dimension_semantics=("arbitrary", "arbitrary"), vmem_limit_bytes=VMEM_LIMIT),
        name="moba",
    )(jnp.asarray(slopes), proj, proj, proj, gq2, gk2, jnp.asarray(bias))


def _memkv_kernel(mem_ref, gmem_ref, w_ref, gk_ref, o_ref):
    mn = _rms(mem_ref[...], gmem_ref[...]).astype(BF16)
    kv = jnp.dot(mn, w_ref[...], preferred_element_type=F32)
    width = MEM_HEADS * MEM_HEAD_DIM
    for h in range(MEM_HEADS):
        sl = slice(h * MEM_HEAD_DIM, (h + 1) * MEM_HEAD_DIM)
        o_ref[:, sl] = _rms(kv[:, sl], gk_ref[...]).astype(BF16)
    o_ref[:, width:] = kv[:, width:].astype(BF16)


def _memkv(mem2, g_mem, w_mem_kv, g_k_mem, B):
    width = MEM_HEADS * MEM_HEAD_DIM
    return pl.pallas_call(
        _memkv_kernel,
        out_shape=jax.ShapeDtypeStruct((B * MEM_LEN, 2 * width), BF16),
        grid=(B,),
        in_specs=[
            pl.BlockSpec((MEM_LEN, D_MODEL), lambda b: (b, 0)),
            pl.BlockSpec((1, D_MODEL), lambda b: (0, 0)),
            pl.BlockSpec((D_MODEL, 2 * width), lambda b: (0, 0)),
            pl.BlockSpec((1, MEM_HEAD_DIM), lambda b: (0, 0)),
        ],
        out_specs=pl.BlockSpec((MEM_LEN, 2 * width), lambda b: (b, 0)),
        compiler_params=pltpu.CompilerParams(
            dimension_semantics=("arbitrary",), vmem_limit_bytes=VMEM_LIMIT),
        name="memkv",
    )(mem2, g_mem, w_mem_kv, g_k_mem)


def _merge_kernel(gu_ref, vn_ref, yb_ref, qm_ref, ga_ref, gb_ref, gc_ref, x_ref, kv_ref,
                  ws_ref, bsp_ref, gqm_ref, wa_ref, wb_ref, wc_ref, wo_ref, gffn_ref,
                  wrh_ref, wrl_ref, br_ref,
                  x1_ref, h2_ref, tope_ref, topw_ref, rank_ref, cnt_ref, run_sc):
    i = pl.program_id(0)
    tm = x_ref.shape[0]

    @pl.when(i == 0)
    def _():
        run_sc[...] = jnp.zeros_like(run_sc)

    ya_chunks = []
    for c in range(tm // SGU_CHUNK):
        rows = slice(c * SGU_CHUNK, (c + 1) * SGU_CHUNK)
        groups = []
        for g in range(SGU_GROUPS):
            cols = slice(g * LANES, (g + 1) * LANES)
            groups.append(jnp.dot(ws_ref[g], vn_ref[rows, cols], preferred_element_type=F32))
        mixed = jnp.concatenate(groups, axis=1) + bsp_ref[...]
        ya_chunks.append(gu_ref[rows, :].astype(F32) * mixed)
    y_a = jnp.concatenate(ya_chunks, axis=0).astype(BF16)

    width = MEM_HEADS * MEM_HEAD_DIM
    yc_heads = []
    for h in range(MEM_HEADS):
        cols = slice(h * MEM_HEAD_DIM, (h + 1) * MEM_HEAD_DIM)
        qn = _rms(qm_ref[:, cols].astype(F32), gqm_ref[...]).astype(BF16)
        s = lax.dot_general(qn, kv_ref[:, cols], (((1,), (1,)), ((), ())),
                            preferred_element_type=F32) * (MEM_HEAD_DIM ** -0.5)
        s = s - jnp.max(s, axis=-1, keepdims=True)
        p = jnp.exp(s)
        p = p / jnp.sum(p, axis=-1, keepdims=True)
        vcols = slice(width + h * MEM_HEAD_DIM, width + (h + 1) * MEM_HEAD_DIM)
        yc_heads.append(jnp.dot(p.astype(BF16), kv_ref[:, vcols], preferred_element_type=F32))
    y_c = jnp.concatenate(yc_heads, axis=1).astype(BF16)

    merged = ga_ref[...].astype(F32) * jnp.dot(y_a, wa_ref[...], preferred_element_type=F32)
    merged = merged + gb_ref[...].astype(F32) * jnp.dot(yb_ref[...], wb_ref[...],
                                                       preferred_element_type=F32)
    merged = merged + gc_ref[...].astype(F32) * jnp.dot(y_c, wc_ref[...],
                                                       preferred_element_type=F32)
    x1 = x_ref[...] + jnp.dot(merged.astype(BF16), wo_ref[...], preferred_element_type=F32)
    x1_ref[...] = x1
    h2 = _rms(x1, gffn_ref[...])
    h2_hi = h2.astype(BF16)
    h2_hi32 = h2_hi.astype(F32)
    h2_lo = (h2 - h2_hi32).astype(BF16)
    bits = lax.bitcast_convert_type(h2_hi32, jnp.uint32)
    half = D_MODEL // 2
    h2_ref[...] = (bits[:, :half] >> 16) | (bits[:, half:] & jnp.uint32(0xFFFF0000))

    nt = (((1,), (1,)), ((), ()))
    logits = (lax.dot_general(wrh_ref[...], h2_hi, nt, preferred_element_type=F32)
              + lax.dot_general(wrh_ref[...], h2_lo, nt, preferred_element_type=F32)
              + lax.dot_general(wrl_ref[...], h2_hi, nt, preferred_element_type=F32)
              + br_ref[...])
    row = lax.broadcasted_iota(jnp.int32, (N_EXPERTS, tm), 0)
    work = logits
    top_e, top_l = [], []
    for _ in range(TOP_K):
        mx = jnp.max(work, axis=0, keepdims=True)
        idx = jnp.min(jnp.where(work == mx, row, N_EXPERTS), axis=0, keepdims=True)
        top_e.append(idx)
        top_l.append(mx)
        work = jnp.where(row == idx, -jnp.inf, work)
    exps = [jnp.exp(v - top_l[0]) for v in top_l]
    denom = exps[0] + exps[1] + exps[2] + exps[3]

    onehots = [row == e for e in top_e]
    multi = jnp.zeros((N_EXPERTS, tm), F32)
    for oh in onehots:
        multi = multi + jnp.where(oh, 1.0, 0.0)
    r_i = lax.broadcasted_iota(jnp.int32, (tm, tm), 0)
    c_i = lax.broadcasted_iota(jnp.int32, (tm, tm), 1)
    earlier = jnp.where(r_i < c_i, 1.0, 0.0).astype(BF16)
    pos = run_sc[...] + jnp.dot(multi.astype(BF16), earlier, preferred_element_type=F32)
    run_sc[...] = run_sc[...] + jnp.sum(multi, axis=1, keepdims=True)
    cnt_ref[...] = run_sc[...]

    for k in range(TOP_K):
        tope_ref[k:k + 1, :] = top_e[k]
        topw_ref[k:k + 1, :] = exps[k] / denom
        rank_ref[k:k + 1, :] = jnp.sum(jnp.where(onehots[k], pos, 0.0), axis=0,
                                       keepdims=True).astype(jnp.int32)


def _merge(proj, yb, x2, kvmem, ws, bsp, g_q_mem, wa, wb, wc, wo, g_ffn, w_router, b_router, S):
    T = x2.shape[0]
    tm = MERGE_TM
    per_b = S // tm
    full = lambda shape: pl.BlockSpec(shape, lambda i: (0,) * len(shape))
    in_specs = [
        pl.BlockSpec((tm, SPLIT), lambda i: (i, 0)),
        pl.BlockSpec((tm, SPLIT), lambda i: (i, 1)),
        pl.BlockSpec((tm, SPLIT), lambda i: (i, 0)),
        pl.BlockSpec((tm, SPLIT), lambda i: (i, 5)),
        pl.BlockSpec((tm, D_MODEL), lambda i: (i, 3)),
        pl.BlockSpec((tm, D_MODEL), lambda i: (i, 4)),
        pl.BlockSpec((tm, D_MODEL), lambda i: (i, 5)),
        pl.BlockSpec((tm, D_MODEL), lambda i: (i, 0)),
        pl.BlockSpec((MEM_LEN, 2 * MEM_HEADS * MEM_HEAD_DIM), lambda i: (i // per_b, 0)),
        full((SGU_GROUPS, SGU_CHUNK, SGU_CHUNK)),
        full((SGU_CHUNK, SGU_WIDTH)),
        full((1, MEM_HEAD_DIM)),
        full((SGU_WIDTH, D_MODEL)),
        full((SGU_WIDTH, D_MODEL)),
        full((SGU_WIDTH, D_MODEL)),
        full((D_MODEL, D_MODEL)),
        full((1, D_MODEL)),
        full((N_EXPERTS, D_MODEL)),
        full((N_EXPERTS, D_MODEL)),
        full((N_EXPERTS, 1)),
    ]
    out_shape = (
        jax.ShapeDtypeStruct((T, D_MODEL), F32),
        jax.ShapeDtypeStruct((T, D_MODEL // 2), jnp.uint32),
        jax.ShapeDtypeStruct((TOP_K, T), jnp.int32),
        jax.ShapeDtypeStruct((TOP_K, T), F32),
        jax.ShapeDtypeStruct((TOP_K, T), jnp.int32),
        jax.ShapeDtypeStruct((N_EXPERTS, 1), F32),
    )
    out_specs = (
        pl.BlockSpec((tm, D_MODEL), lambda i: (i, 0)),
        pl.BlockSpec((tm, D_MODEL // 2), lambda i: (i, 0)),
        pl.BlockSpec((TOP_K, tm), lambda i: (0, i)),
        pl.BlockSpec((TOP_K, tm), lambda i: (0, i)),
        pl.BlockSpec((TOP_K, tm), lambda i: (0, i)),
        pl.BlockSpec((N_EXPERTS, 1), lambda i: (0, 0)),
    )
    wr_t = w_router.T
    wr_hi = wr_t.astype(BF16)
    wr_lo = (wr_t - wr_hi.astype(F32)).astype(BF16)
    return pl.pallas_call(
        _merge_kernel,
        out_shape=out_shape,
        grid=(T // tm,),
        in_specs=in_specs,
        out_specs=out_specs,
        scratch_shapes=[pltpu.VMEM((N_EXPERTS, 1), F32)],
        compiler_params=pltpu.CompilerParams(
            dimension_semantics=("arbitrary",), vmem_limit_bytes=VMEM_LIMIT),
        name="merge",
    )(proj, proj, yb, proj, proj, proj, proj, x2, kvmem, ws, bsp, g_q_mem,
      wa, wb, wc, wo, g_ffn, wr_hi, wr_lo, b_router.reshape(N_EXPERTS, 1))


def _dispatch_kernel(dest_ref, h_ref, xin_ref, xbuf_ref, sem):
    del xin_ref
    tm = h_ref.shape[0]

    def row_copy(r, d):
        return pltpu.make_async_copy(h_ref.at[pl.ds(r, 1)], xbuf_ref.at[pl.ds(d, 1)], sem)

    def start(r, carry):
        for k in range(TOP_K):
            row_copy(r, dest_ref[r * TOP_K + k]).start(priority=k % 2)
        return carry

    lax.fori_loop(0, tm, start, 0, unroll=8)
    for _ in range(TOP_K):
        pltpu.make_async_copy(h_ref, xbuf_ref.at[pl.ds(0, tm)], sem).wait()


def _dispatch(dest_flat, h2, n_rows):
    T, width = h2.shape
    tm = ROUTE_TM
    xbuf0 = jnp.zeros((n_rows, width), h2.dtype)
    return pl.pallas_call(
        _dispatch_kernel,
        out_shape=jax.ShapeDtypeStruct((n_rows, width), h2.dtype),
        grid=(T // tm,),
        in_specs=[
            pl.BlockSpec((tm * TOP_K,), lambda i: (i,), memory_space=pltpu.SMEM),
            pl.BlockSpec((tm, width), lambda i: (i, 0)),
            pl.BlockSpec(memory_space=pl.ANY),
        ],
        out_specs=pl.BlockSpec(memory_space=pl.ANY),
        scratch_shapes=[pltpu.SemaphoreType.DMA],
        input_output_aliases={2: 0},
        compiler_params=pltpu.CompilerParams(
            dimension_semantics=("arbitrary",), vmem_limit_bytes=VMEM_LIMIT),
        name="dispatch",
    )(dest_flat, h2, xbuf0)


FF_CHUNK = 512


def _expert_kernel(be_ref, nused_ref, x_ref, w1_ref, b1_ref, w2_ref, b2_ref, y_ref, w1_sc, w2_sc):
    i = pl.program_id(0)

    @pl.when(i < nused_ref[0])
    def _():
        @pl.when(jnp.logical_or(i == 0, be_ref[i] != be_ref[jnp.maximum(i - 1, 0)]))
        def _():
            w1_sc[...] = w1_ref[0].astype(BF16)
            w2_sc[...] = w2_ref[0].astype(BF16)

        packed = x_ref[...]
        lo = lax.bitcast_convert_type(packed << 16, F32)
        hi = lax.bitcast_convert_type(packed & jnp.uint32(0xFFFF0000), F32)
        xb = jnp.concatenate([lo, hi], axis=1).astype(BF16)
        y = b2_ref[0]
        for c in range(D_FF // FF_CHUNK):
            g_cols = slice(c * FF_CHUNK, (c + 1) * FF_CHUNK)
            l_cols = slice(D_FF + c * FF_CHUNK, D_FF + (c + 1) * FF_CHUNK)
            glu = jnp.dot(xb, w1_sc[:, g_cols], preferred_element_type=F32) + b1_ref[0, :, g_cols]
            lin = jnp.dot(xb, w1_sc[:, l_cols], preferred_element_type=F32) + b1_ref[0, :, l_cols]
            glu = jnp.minimum(glu, SWIGLU_LIMIT)
            lin = jnp.clip(lin, -SWIGLU_LIMIT, SWIGLU_LIMIT)
            act = glu * jax.nn.sigmoid(SWIGLU_ALPHA * glu) * (lin + 1.0)
            y = y + jnp.dot(act.astype(BF16), w2_sc[g_cols, :], preferred_element_type=F32)
        y_ref[...] = y

    @pl.when(i >= nused_ref[0])
    def _():
        y_ref[...] = jnp.zeros_like(y_ref)


def _experts(block_e, n_used, xbuf, w1, b1, w2, b2):
    n_rows, width = xbuf.shape
    bm = EXPERT_BM
    grid_spec = pltpu.PrefetchScalarGridSpec(
        num_scalar_prefetch=2,
        grid=(n_rows // bm,),
        in_specs=[
            pl.BlockSpec((bm, width), lambda i, be, nu: (i, 0)),
            pl.BlockSpec((1, D_MODEL, 2 * D_FF), lambda i, be, nu: (be[i], 0, 0)),
            pl.BlockSpec((1, 1, 2 * D_FF), lambda i, be, nu: (be[i], 0, 0)),
            pl.BlockSpec((1, D_FF, D_MODEL), lambda i, be, nu: (be[i], 0, 0)),
            pl.BlockSpec((1, 1, D_MODEL), lambda i, be, nu: (be[i], 0, 0)),
        ],
        out_specs=pl.BlockSpec((bm, D_MODEL), lambda i, be, nu: (i, 0)),
        scratch_shapes=[pltpu.VMEM((D_MODEL, 2 * D_FF), BF16), pltpu.VMEM((D_FF, D_MODEL), BF16)],
    )
    return pl.pallas_call(
        _expert_kernel,
        out_shape=jax.ShapeDtypeStruct((n_rows, D_MODEL), F32),
        grid_spec=grid_spec,
        compiler_params=pltpu.CompilerParams(
            dimension_semantics=("arbitrary",), vmem_limit_bytes=VMEM_LIMIT),
        name="experts",
    )(block_e, n_used, xbuf, w1, b1, w2, b2)


def _combine_kernel(dest_ref, x1_ref, w_ref, ybuf_ref, o_ref, g_sc, sem):
    tm = x1_ref.shape[0]

    def row_copy(r, k, d):
        return pltpu.make_async_copy(ybuf_ref.at[pl.ds(d, 1)], g_sc.at[k, pl.ds(r, 1)], sem)

    def start(r, carry):
        for k in range(TOP_K):
            row_copy(r, k, dest_ref[r * TOP_K + k]).start(priority=k % 2)
        return carry

    lax.fori_loop(0, tm, start, 0, unroll=8)
    for k in range(TOP_K):
        pltpu.make_async_copy(ybuf_ref.at[pl.ds(0, tm)], g_sc.at[k], sem).wait()
    acc = x1_ref[...]
    w = w_ref[...]
    for k in range(TOP_K):
        acc = acc + w[:, k:k + 1] * g_sc[k]
    o_ref[...] = acc


def _combine(dest_flat, x1, top_w, ybuf):
    T = x1.shape[0]
    tm = ROUTE_TM
    return pl.pallas_call(
        _combine_kernel,
        out_shape=jax.ShapeDtypeStruct((T, D_MODEL), F32),
        grid=(T // tm,),
        in_specs=[
            pl.BlockSpec((tm * TOP_K,), lambda i: (i,), memory_space=pltpu.SMEM),
            pl.BlockSpec((tm, D_MODEL), lambda i: (i, 0)),
            pl.BlockSpec((tm, TOP_K), lambda i: (i, 0)),
            pl.BlockSpec(memory_space=pl.ANY),
        ],
        out_specs=pl.BlockSpec((tm, D_MODEL), lambda i: (i, 0)),
        scratch_shapes=[pltpu.VMEM((TOP_K, tm, D_MODEL), F32), pltpu.SemaphoreType.DMA],
        compiler_params=pltpu.CompilerParams(
            dimension_semantics=("arbitrary",), vmem_limit_bytes=VMEM_LIMIT),
        name="combine",
    )(dest_flat, x1, top_w, ybuf)


def _layer(x2, mem2, B, S, g_mix, w_in, g_sgu_v, w_spatial, b_spatial, g_q_moba, g_k_moba,
           g_mem, w_mem_kv, g_q_mem, g_k_mem, w_br_sgu, w_br_moba, w_br_mem, w_out,
           g_ffn, w_router, b_router, w_gate_up, b_gate_up, w_down, b_down):
    T = B * S
    row = lambda v: v.reshape(1, -1)
    proj = _inproj(x2, row(g_mix), w_in.astype(BF16), row(g_sgu_v))
    yb = _moba(proj, g_q_moba, g_k_moba, B, S)
    kvmem = _memkv(mem2, row(g_mem), w_mem_kv.astype(BF16), row(g_k_mem), B)

    causal = jnp.tril(jnp.ones((SGU_CHUNK, SGU_CHUNK), dtype=bool))
    ws = jnp.where(causal[None], w_spatial, 0.0).astype(BF16)
    bsp = jnp.repeat(b_spatial.T, LANES, axis=1)
    x1, h2, top_e, top_w, rank, counts = _merge(
        proj, yb, x2, kvmem, ws, bsp, row(g_q_mem), w_br_sgu.astype(BF16),
        w_br_moba.astype(BF16), w_br_mem.astype(BF16), w_out.astype(BF16), row(g_ffn),
        w_router, b_router, S)

    bm = EXPERT_BM
    n_blocks = -(-(T * TOP_K + N_EXPERTS * (bm - 1)) // bm)
    cnt = counts.reshape(N_EXPERTS).astype(jnp.int32)
    padded = (cnt + bm - 1) // bm * bm
    pad_ends = jnp.cumsum(padded)
    pad_starts = pad_ends - padded
    e_ids = jnp.arange(N_EXPERTS, dtype=jnp.int32)[:, None, None]
    start_of = jnp.sum(jnp.where(top_e[None] == e_ids, pad_starts[:, None, None], 0), axis=0)
    dest = (start_of + rank).T.reshape(T * TOP_K)
    top_w = top_w.T
    block_row0 = jnp.arange(n_blocks, dtype=jnp.int32) * bm
    block_e = jnp.minimum(
        jnp.sum((pad_ends[None, :] <= block_row0[:, None]).astype(jnp.int32), axis=1), N_EXPERTS - 1)
    n_used = (pad_ends[-1:] // bm).astype(jnp.int32)

    xbuf = _dispatch(dest, h2, n_blocks * bm)
    ybuf = _experts(block_e, n_used, xbuf, w_gate_up, b_gate_up.reshape(N_EXPERTS, 1, 2 * D_FF),
                    w_down, b_down.reshape(N_EXPERTS, 1, D_MODEL))
    return _combine(dest, x1, top_w, ybuf)


def kernel(x, mem, g_mix, w_in, g_sgu_v, w_spatial, b_spatial, g_q_moba, g_k_moba, g_mem, w_mem_kv, g_q_mem, g_k_mem, w_br_sgu, w_br_moba, w_br_mem, w_out, g_ffn, w_router, b_router, w_gate_up, b_gate_up, w_down, b_down):
    B, S, D = x.shape
    x2 = x.reshape(B * S, D)
    mem2 = mem.reshape(B * MEM_LEN, D)
    params = (g_mix, w_in, g_sgu_v, w_spatial, b_spatial, g_q_moba, g_k_moba, g_mem, w_mem_kv,
              g_q_mem, g_k_mem, w_br_sgu, w_br_moba, w_br_mem, w_out, g_ffn, w_router, b_router,
              w_gate_up, b_gate_up, w_down, b_down)
    for l in range(g_mix.shape[0]):
        x2 = _layer(x2, mem2, B, S, *(p[l] for p in params))
    return x2.reshape(B, S, D)
```

```python
import functools

import numpy as np
import jax
import jax.numpy as jnp
from jax import lax
from jax.experimental import pallas as pl
from jax.experimental.pallas import tpu as pltpu

F32 = jnp.float32
BF16 = jnp.bfloat16

D_MODEL = 1024
MEM_LEN = 256
SGU_GROUPS = 4
SGU_CHUNK = 128
SGU_WIDTH = 512
MOBA_HEADS = 8
MOBA_HEAD_DIM = 64
MOBA_BLOCK = 256
MOBA_TOPK = 3
MEM_HEADS = 4
MEM_HEAD_DIM = 128
SPLIT = 512
IN_WIDTH = 6144
N_SPLITS = IN_WIDTH // SPLIT
N_EXPERTS = 32
TOP_K = 4
D_FF = 1024
SWIGLU_ALPHA = 1.702
SWIGLU_LIMIT = 7.0
EPS = 1e-6
NEG_INF = -1e30
LANES = 128

VMEM_LIMIT = 56 * 1024 * 1024

INPROJ_TM = 256
MERGE_TM = 512
ROUTE_TM = 256
EXPERT_BM = 512


def _gelu(x):
    return 0.5 * x * (1.0 + lax.erf(x * np.float32(np.sqrt(0.5))))


def _rms(x, g):
    return x * lax.rsqrt(jnp.mean(x * x, axis=-1, keepdims=True) + EPS) * g


def _inproj_kernel(x_ref, gmix_ref, w_ref, gsgu_ref, o_ref):
    h = _rms(x_ref[...], gmix_ref[...]).astype(BF16)
    for c in range(N_SPLITS):
        sl = slice(c * SPLIT, (c + 1) * SPLIT)
        acc = jnp.dot(h, w_ref[:, sl], preferred_element_type=F32)
        if c == 0:
            val = _gelu(acc)
        elif c == 1:
            val = _rms(_gelu(acc), gsgu_ref[...])
        elif c >= 6:
            val = jax.nn.sigmoid(acc)
        else:
            val = acc
        o_ref[:, sl] = val.astype(BF16)


def _inproj(x2, g_mix, w_in, g_sgu_v):
    T = x2.shape[0]
    tm = INPROJ_TM
    return pl.pallas_call(
        _inproj_kernel,
        out_shape=jax.ShapeDtypeStruct((T, IN_WIDTH), BF16),
        grid=(T // tm,),
        in_specs=[
            pl.BlockSpec((tm, D_MODEL), lambda i: (i, 0)),
            pl.BlockSpec((1, D_MODEL), lambda i: (0, 0)),
            pl.BlockSpec((D_MODEL, IN_WIDTH), lambda i: (0, 0)),
            pl.BlockSpec((1, SGU_WIDTH), lambda i: (0, 0)),
        ],
        out_specs=pl.BlockSpec((tm, IN_WIDTH), lambda i: (i, 0)),
        compiler_params=pltpu.CompilerParams(
            dimension_semantics=("arbitrary",), vmem_limit_bytes=VMEM_LIMIT),
        name="inproj",
    )(x2, g_mix, w_in, g_sgu_v)


def _pair_norm(x, g, head0):
    sq = x * x
    s0 = jnp.sum(jnp.where(head0, sq, 0.0), axis=-1, keepdims=True)
    s1 = jnp.sum(jnp.where(head0, 0.0, sq), axis=-1, keepdims=True)
    inv = jnp.where(head0, lax.rsqrt(s0 * (1.0 / MOBA_HEAD_DIM) + EPS),
                    lax.rsqrt(s1 * (1.0 / MOBA_HEAD_DIM) + EPS))
    return x * inv * g


VT_ROWS = MOBA_HEAD_DIM + 16
LOG2E = float(np.log2(np.e))


def _moba_kernel(slopes_ref, q_ref, k_ref, v_ref, gq_ref, gk_ref, bias_ref, o_ref,
                 kn_sc, vt_sc, kmean_sc, mrow_sc, qh_sc, m_sc, acc_sc, t_sc, cmax_sc, *, nb):
    qi = pl.program_id(1)
    bs = MOBA_BLOCK
    pairs = MOBA_HEADS // 2
    head0 = lax.broadcasted_iota(jnp.int32, (1, LANES), 1) < MOBA_HEAD_DIM
    nt = (((1,), (1,)), ((), ()))

    @pl.when(qi == 0)
    def _prepare_kv():
        ones_rows = jnp.where(
            lax.broadcasted_iota(jnp.int32, (VT_ROWS - MOBA_HEAD_DIM, bs), 0) == 0, 1.0, 0.0).astype(BF16)

        def body(j, carry):
            rows = pl.ds(pl.multiple_of(j * bs, bs), bs)
            means = []
            for g in range(pairs):
                cols = slice(g * LANES, (g + 1) * LANES)
                kn = _pair_norm(k_ref[rows, cols].astype(F32), gk_ref[...], head0)
                kn_sc[j, :, cols] = kn.astype(BF16)
                means.append(jnp.mean(kn, axis=0, keepdims=True))
                vt = v_ref[rows, cols].astype(F32).T.astype(BF16)
                for hh in range(2):
                    h = 2 * g + hh
                    vt_sc[j, h, :MOBA_HEAD_DIM, :] = vt[hh * MOBA_HEAD_DIM:(hh + 1) * MOBA_HEAD_DIM]
                    vt_sc[j, h, MOBA_HEAD_DIM:, :] = ones_rows
            kmean_sc[pl.ds(j, 1), :] = jnp.concatenate(means, axis=1)
            return carry
        lax.fori_loop(0, nb, body, 0)

    blk = lax.broadcasted_iota(jnp.int32, (nb, 1), 0)
    eligible = blk < qi
    dist = ((qi - blk) * bs).astype(F32)
    for g in range(pairs):
        cols = slice(g * LANES, (g + 1) * LANES)
        qn = _pair_norm(q_ref[:, cols].astype(F32), gq_ref[...], head0)
        qpair = jnp.concatenate([jnp.where(head0, qn, 0.0), jnp.where(head0, 0.0, qn)], axis=0)
        qh_sc[g] = (qpair * (LOG2E * MOBA_HEAD_DIM ** -0.5)).astype(BF16)
        q_hi = qpair.astype(BF16)
        q_lo = (qpair - q_hi.astype(F32)).astype(BF16)
        km = kmean_sc[:, cols]
        km_hi = km.astype(BF16)
        km_lo = (km - km_hi.astype(F32)).astype(BF16)
        gate = (lax.dot_general(km_hi, q_hi, nt, preferred_element_type=F32)
                + lax.dot_general(km_hi, q_lo, nt, preferred_element_type=F32)
                + lax.dot_general(km_lo, q_hi, nt, preferred_element_type=F32))
        gate = jnp.where(eligible, gate, -jnp.inf)
        rank = jnp.zeros((nb, 2 * bs), F32)
        for i in range(nb - 1):
            gi = gate[i:i + 1, :]
            beats = jnp.logical_or(gi > gate, jnp.logical_and(gi == gate, blk > i))
            rank = rank + jnp.where(beats, 1.0, 0.0)
        sel = jnp.logical_and(eligible, rank < float(MOBA_TOPK))
        for hh in range(2):
            h = 2 * g + hh
            mrow_sc[:, h * bs:(h + 1) * bs] = jnp.where(
                blk == qi, 0.0,
                jnp.where(sel[:, hh * bs:(hh + 1) * bs], (-LOG2E * slopes_ref[h]) * dist, NEG_INF))
    m_sc[...] = jnp.full(m_sc.shape, NEG_INF, F32)
    acc_sc[...] = jnp.zeros(acc_sc.shape, F32)

    def score(j, own, slot):
        scores = [lax.dot_general(kn_sc[j, :, g * LANES:(g + 1) * LANES], qh_sc[g], nt,
                                  preferred_element_type=F32) for g in range(pairs)]
        t = jnp.concatenate(scores, axis=1) + bias_ref[1 if own else 0]
        t_sc[slot] = t
        cmax_sc[slot] = jnp.max(t, axis=0, keepdims=True)

    def accumulate(j, slot):
        mrow = mrow_sc[pl.ds(j, 1), :]
        m_old = m_sc[...]
        m_new = jnp.maximum(m_old, cmax_sc[slot] + mrow)
        shift = m_new - mrow
        alpha = jnp.exp2(m_old - m_new)
        m_sc[...] = m_new
        for h in range(MOBA_HEADS):
            cols = slice(h * bs, (h + 1) * bs)
            p = jnp.exp2(t_sc[slot, :, cols] - shift[:, cols]).astype(BF16)
            pv = jnp.dot(vt_sc[j, h], p, preferred_element_type=F32)
            acc_sc[h] = alpha[:, cols] * acc_sc[h] + pv

    n_past = qi
    score(qi, True, 0)

    def two_blocks(p, carry):
        score(2 * p, False, 1)
        accumulate(jnp.where(p == 0, qi, 2 * p - 1), 0)
        score(2 * p + 1, False, 0)
        accumulate(2 * p, 1)
        return carry
    lax.fori_loop(0, n_past // 2, two_blocks, 0)

    @pl.when(n_past % 2 == 0)
    def _():
        accumulate(jnp.where(n_past == 0, qi, n_past - 1), 0)

    @pl.when(n_past % 2 == 1)
    def _():
        score(n_past - 1, False, 1)
        accumulate(jnp.where(n_past == 1, qi, n_past - 2), 0)
        accumulate(n_past - 1, 1)

    for g in range(pairs):
        halves = []
        for hh in range(2):
            acc = acc_sc[2 * g + hh]
            halves.append(acc[:MOBA_HEAD_DIM] / acc[MOBA_HEAD_DIM:MOBA_HEAD_DIM + 1])
        out_t = jnp.concatenate(halves, axis=0)
        o_ref[:, g * LANES:(g + 1) * LANES] = out_t.T.astype(BF16)


def _moba_bias_tiles():
    slopes = 2.0 ** (-8.0 * np.arange(1, MOBA_HEADS + 1, dtype=np.float32) / MOBA_HEADS)
    kk = np.arange(MOBA_BLOCK)[:, None]
    qq = np.arange(MOBA_BLOCK)[None, :]
    rel = (qq - kk).astype(np.float64)
    past = -slopes.astype(np.float64)[:, None, None] * rel[None] * LOG2E
    own = np.where((kk <= qq)[None], past, NEG_INF)
    side_by_side = lambda a: a.transpose(1, 0, 2).reshape(MOBA_BLOCK, MOBA_HEADS * MOBA_BLOCK)
    return slopes.astype(np.float32), np.stack([side_by_side(past), side_by_side(own)]).astype(np.float32)


def _moba(proj, g_q, g_k, B, S):
    nb = S // MOBA_BLOCK
    width = MOBA_HEADS * MOBA_HEAD_DIM
    slopes, bias = _moba_bias_tiles()
    gq2 = jnp.tile(g_q.reshape(1, MOBA_HEAD_DIM), (1, 2))
    gk2 = jnp.tile(g_k.reshape(1, MOBA_HEAD_DIM), (1, 2))
    full = lambda shape: pl.BlockSpec(shape, lambda b, q, s: (0,) * len(shape))
    grid_spec = pltpu.PrefetchScalarGridSpec(
        num_scalar_prefetch=1,
        grid=(B, nb),
        in_specs=[
            pl.BlockSpec((MOBA_BLOCK, width), lambda b, q, s: (b * nb + q, 2)),
            pl.BlockSpec((S, width), lambda b, q, s: (b, 3)),
            pl.BlockSpec((S, width), lambda b, q, s: (b, 4)),
            full((1, LANES)),
            full((1, LANES)),
            full((2, MOBA_BLOCK, MOBA_HEADS * MOBA_BLOCK)),
        ],
        out_specs=pl.BlockSpec((MOBA_BLOCK, width), lambda b, q, s: (b * nb + q, 0)),
        scratch_shapes=[
            pltpu.VMEM((nb, MOBA_BLOCK, width), BF16),
            pltpu.VMEM((nb, MOBA_HEADS, VT_ROWS, MOBA_BLOCK), BF16),
            pltpu.VMEM((nb, width), F32),
            pltpu.VMEM((nb, MOBA_HEADS * MOBA_BLOCK), F32),
            pltpu.VMEM((MOBA_HEADS // 2, 2 * MOBA_BLOCK, LANES), BF16),
            pltpu.VMEM((1, MOBA_HEADS * MOBA_BLOCK), F32),
            pltpu.VMEM((MOBA_HEADS, VT_ROWS, MOBA_BLOCK), F32),
            pltpu.VMEM((2, MOBA_BLOCK, MOBA_HEADS * MOBA_BLOCK), F32),
            pltpu.VMEM((2, 1, MOBA_HEADS * MOBA_BLOCK), F32),
        ],
    )
    return pl.pallas_call(
        functools.partial(_moba_kernel, nb=nb),
        out_shape=jax.ShapeDtypeStruct((B * S, width), BF16),
        grid_spec=grid_spec,
        compiler_params=pltpu.CompilerParams(
            dimension_semantics=("arbitrary", "arbitrary"), vmem_limit_bytes=VMEM_LIMIT),
        name="moba",
    )(jnp.asarray(slopes), proj, proj, proj, gq2, gk2, jnp.asarray(bias))


def _memkv_kernel(mem_ref, gmem_ref, w_ref, gk_ref, o_ref):
    mn = _rms(mem_ref[...], gmem_ref[...]).astype(BF16)
    kv = jnp.dot(mn, w_ref[...], preferred_element_type=F32)
    width = MEM_HEADS * MEM_HEAD_DIM
    for h in range(MEM_HEADS):
        sl = slice(h * MEM_HEAD_DIM, (h + 1) * MEM_HEAD_DIM)
        o_ref[:, sl] = _rms(kv[:, sl], gk_ref[...]).astype(BF16)
    o_ref[:, width:] = kv[:, width:].astype(BF16)


def _memkv(mem2, g_mem, w_mem_kv, g_k_mem, B):
    width = MEM_HEADS * MEM_HEAD_DIM
    return pl.pallas_call(
        _memkv_kernel,
        out_shape=jax.ShapeDtypeStruct((B * MEM_LEN, 2 * width), BF16),
        grid=(B,),
        in_specs=[
            pl.BlockSpec((MEM_LEN, D_MODEL), lambda b: (b, 0)),
            pl.BlockSpec((1, D_MODEL), lambda b: (0, 0)),
            pl.BlockSpec((D_MODEL, 2 * width), lambda b: (0, 0)),
            pl.BlockSpec((1, MEM_HEAD_DIM), lambda b: (0, 0)),
        ],
        out_specs=pl.BlockSpec((MEM_LEN, 2 * width), lambda b: (b, 0)),
        compiler_params=pltpu.CompilerParams(
            dimension_semantics=("arbitrary",), vmem_limit_bytes=VMEM_LIMIT),
        name="memkv",
    )(mem2, g_mem, w_mem_kv, g_k_mem)


def _merge_kernel(gu_ref, vn_ref, yb_ref, qm_ref, ga_ref, gb_ref, gc_ref, x_ref, kv_ref,
                  ws_ref, bsp_ref, gqm_ref, wa_ref, wb_ref, wc_ref, wo_ref, gffn_ref,
                  wrh_ref, wrl_ref, br_ref,
                  x1_ref, h2_ref, tope_ref, topw_ref, rank_ref, cnt_ref, run_sc):
    i = pl.program_id(0)
    tm = x_ref.shape[0]

    @pl.when(i == 0)
    def _():
        run_sc[...] = jnp.zeros_like(run_sc)

    ya_chunks = []
    for c in range(tm // SGU_CHUNK):
        rows = slice(c * SGU_CHUNK, (c + 1) * SGU_CHUNK)
        groups = []
        for g in range(SGU_GROUPS):
            cols = slice(g * LANES, (g + 1) * LANES)
            groups.append(jnp.dot(ws_ref[g], vn_ref[rows, cols], preferred_element_type=F32))
        mixed = jnp.concatenate(groups, axis=1) + bsp_ref[...]
        ya_chunks.append(gu_ref[rows, :].astype(F32) * mixed)
    y_a = jnp.concatenate(ya_chunks, axis=0).astype(BF16)

    width = MEM_HEADS * MEM_HEAD_DIM
    yc_heads = []
    for h in range(MEM_HEADS):
        cols = slice(h * MEM_HEAD_DIM, (h + 1) * MEM_HEAD_DIM)
        qn = _rms(qm_ref[:, cols].astype(F32), gqm_ref[...]).astype(BF16)
        s = lax.dot_general(qn, kv_ref[:, cols], (((1,), (1,)), ((), ())),
                            preferred_element_type=F32) * (MEM_HEAD_DIM ** -0.5)
        s = s - jnp.max(s, axis=-1, keepdims=True)
        p = jnp.exp(s)
        p = p / jnp.sum(p, axis=-1, keepdims=True)
        vcols = slice(width + h * MEM_HEAD_DIM, width + (h + 1) * MEM_HEAD_DIM)
        yc_heads.append(jnp.dot(p.astype(BF16), kv_ref[:, vcols], preferred_element_type=F32))
    y_c = jnp.concatenate(yc_heads, axis=1).astype(BF16)

    merged = ga_ref[...].astype(F32) * jnp.dot(y_a, wa_ref[...], preferred_element_type=F32)
    merged = merged + gb_ref[...].astype(F32) * jnp.dot(yb_ref[...], wb_ref[...],
                                                       preferred_element_type=F32)
    merged = merged + gc_ref[...].astype(F32) * jnp.dot(y_c, wc_ref[...],
                                                       preferred_element_type=F32)
    x1 = x_ref[...] + jnp.dot(merged.astype(BF16), wo_ref[...], preferred_element_type=F32)
    x1_ref[...] = x1
    h2 = _rms(x1, gffn_ref[...])
    h2_hi = h2.astype(BF16)
    h2_hi32 = h2_hi.astype(F32)
    h2_lo = (h2 - h2_hi32).astype(BF16)
    bits = lax.bitcast_convert_type(h2_hi32, jnp.uint32)
    half = D_MODEL // 2
    h2_ref[...] = (bits[:, :half] >> 16) | (bits[:, half:] & jnp.uint32(0xFFFF0000))

    nt = (((1,), (1,)), ((), ()))
    logits = (lax.dot_general(wrh_ref[...], h2_hi, nt, preferred_element_type=F32)
              + lax.dot_general(wrh_ref[...], h2_lo, nt, preferred_element_type=F32)
              + lax.dot_general(wrl_ref[...], h2_hi, nt, preferred_element_type=F32)
              + br_ref[...])
    row = lax.broadcasted_iota(jnp.int32, (N_EXPERTS, tm), 0)
    work = logits
    top_e, top_l = [], []
    for _ in range(TOP_K):
        mx = jnp.max(work, axis=0, keepdims=True)
        idx = jnp.min(jnp.where(work == mx, row, N_EXPERTS), axis=0, keepdims=True)
        top_e.append(idx)
        top_l.append(mx)
        work = jnp.where(row == idx, -jnp.inf, work)
    exps = [jnp.exp(v - top_l[0]) for v in top_l]
    denom = exps[0] + exps[1] + exps[2] + exps[3]

    onehots = [row == e for e in top_e]
    multi = jnp.zeros((N_EXPERTS, tm), F32)
    for oh in onehots:
        multi = multi + jnp.where(oh, 1.0, 0.0)
    r_i = lax.broadcasted_iota(jnp.int32, (tm, tm), 0)
    c_i = lax.broadcasted_iota(jnp.int32, (tm, tm), 1)
    earlier = jnp.where(r_i < c_i, 1.0, 0.0).astype(BF16)
    pos = run_sc[...] + jnp.dot(multi.astype(BF16), earlier, preferred_element_type=F32)
    run_sc[...] = run_sc[...] + jnp.sum(multi, axis=1, keepdims=True)
    cnt_ref[...] = run_sc[...]

    for k in range(TOP_K):
        tope_ref[k:k + 1, :] = top_e[k]
        topw_ref[k:k + 1, :] = exps[k] / denom
        rank_ref[k:k + 1, :] = jnp.sum(jnp.where(onehots[k], pos, 0.0), axis=0,
                                       keepdims=True).astype(jnp.int32)


def _merge(proj, yb, x2, kvmem, ws, bsp, g_q_mem, wa, wb, wc, wo, g_ffn, w_router, b_router, S):
    T = x2.shape[0]
    tm = MERGE_TM
    per_b = S // tm
    full = lambda shape: pl.BlockSpec(shape, lambda i: (0,) * len(shape))
    in_specs = [
        pl.BlockSpec((tm, SPLIT), lambda i: (i, 0)),
        pl.BlockSpec((tm, SPLIT), lambda i: (i, 1)),
        pl.BlockSpec((tm, SPLIT), lambda i: (i, 0)),
        pl.BlockSpec((tm, SPLIT), lambda i: (i, 5)),
        pl.BlockSpec((tm, D_MODEL), lambda i: (i, 3)),
        pl.BlockSpec((tm, D_MODEL), lambda i: (i, 4)),
        pl.BlockSpec((tm, D_MODEL), lambda i: (i, 5)),
        pl.BlockSpec((tm, D_MODEL), lambda i: (i, 0)),
        pl.BlockSpec((MEM_LEN, 2 * MEM_HEADS * MEM_HEAD_DIM), lambda i: (i // per_b, 0)),
        full((SGU_GROUPS, SGU_CHUNK, SGU_CHUNK)),
        full((SGU_CHUNK, SGU_WIDTH)),
        full((1, MEM_HEAD_DIM)),
        full((SGU_WIDTH, D_MODEL)),
        full((SGU_WIDTH, D_MODEL)),
        full((SGU_WIDTH, D_MODEL)),
        full((D_MODEL, D_MODEL)),
        full((1, D_MODEL)),
        full((N_EXPERTS, D_MODEL)),
        full((N_EXPERTS, D_MODEL)),
        full((N_EXPERTS, 1)),
    ]
    out_shape = (
        jax.ShapeDtypeStruct((T, D_MODEL), F32),
        jax.ShapeDtypeStruct((T, D_MODEL // 2), jnp.uint32),
        jax.ShapeDtypeStruct((TOP_K, T), jnp.int32),
        jax.ShapeDtypeStruct((TOP_K, T), F32),
        jax.ShapeDtypeStruct((TOP_K, T), jnp.int32),
        jax.ShapeDtypeStruct((N_EXPERTS, 1), F32),
    )
    out_specs = (
        pl.BlockSpec((tm, D_MODEL), lambda i: (i, 0)),
        pl.BlockSpec((tm, D_MODEL // 2), lambda i: (i, 0)),
        pl.BlockSpec((TOP_K, tm), lambda i: (0, i)),
        pl.BlockSpec((TOP_K, tm), lambda i: (0, i)),
        pl.BlockSpec((TOP_K, tm), lambda i: (0, i)),
        pl.BlockSpec((N_EXPERTS, 1), lambda i: (0, 0)),
    )
    wr_t = w_router.T
    wr_hi = wr_t.astype(BF16)
    wr_lo = (wr_t - wr_hi.astype(F32)).astype(BF16)
    return pl.pallas_call(
        _merge_kernel,
        out_shape=out_shape,
        grid=(T // tm,),
        in_specs=in_specs,
        out_specs=out_specs,
        scratch_shapes=[pltpu.VMEM((N_EXPERTS, 1), F32)],
        compiler_params=pltpu.CompilerParams(
            dimension_semantics=("arbitrary",), vmem_limit_bytes=VMEM_LIMIT),
        name="merge",
    )(proj, proj, yb, proj, proj, proj, proj, x2, kvmem, ws, bsp, g_q_mem,
      wa, wb, wc, wo, g_ffn, wr_hi, wr_lo, b_router.reshape(N_EXPERTS, 1))


def _dispatch_kernel(dest_ref, h_ref, xin_ref, xbuf_ref, sem):
    del xin_ref
    tm = h_ref.shape[0]

    def row_copy(r, d):
        return pltpu.make_async_copy(h_ref.at[pl.ds(r, 1)], xbuf_ref.at[pl.ds(d, 1)], sem)

    def start(r, carry):
        for k in range(TOP_K):
            row_copy(r, dest_ref[r * TOP_K + k]).start(priority=k % 2)
        return carry

    for r in range(tm):
        start(r, None)
    for _ in range(TOP_K):
        pltpu.make_async_copy(h_ref, xbuf_ref.at[pl.ds(0, tm)], sem).wait()


def _dispatch(dest_flat, h2, n_rows):
    T, width = h2.shape
    tm = ROUTE_TM
    xbuf0 = jnp.zeros((n_rows, width), h2.dtype)
    return pl.pallas_call(
        _dispatch_kernel,
        out_shape=jax.ShapeDtypeStruct((n_rows, width), h2.dtype),
        grid=(T // tm,),
        in_specs=[
            pl.BlockSpec((tm * TOP_K,), lambda i: (i,), memory_space=pltpu.SMEM),
            pl.BlockSpec((tm, width), lambda i: (i, 0)),
            pl.BlockSpec(memory_space=pl.ANY),
        ],
        out_specs=pl.BlockSpec(memory_space=pl.ANY),
        scratch_shapes=[pltpu.SemaphoreType.DMA],
        input_output_aliases={2: 0},
        compiler_params=pltpu.CompilerParams(
            dimension_semantics=("arbitrary",), vmem_limit_bytes=VMEM_LIMIT),
        name="dispatch",
    )(dest_flat, h2, xbuf0)


FF_CHUNK = 512


def _expert_kernel(be_ref, nused_ref, x_ref, w1_ref, b1_ref, w2_ref, b2_ref, y_ref, w1_sc, w2_sc):
    i = pl.program_id(0)

    @pl.when(i < nused_ref[0])
    def _():
        @pl.when(jnp.logical_or(i == 0, be_ref[i] != be_ref[jnp.maximum(i - 1, 0)]))
        def _():
            w1_sc[...] = w1_ref[0].astype(BF16)
            w2_sc[...] = w2_ref[0].astype(BF16)

        packed = x_ref[...]
        lo = lax.bitcast_convert_type(packed << 16, F32)
        hi = lax.bitcast_convert_type(packed & jnp.uint32(0xFFFF0000), F32)
        xb = jnp.concatenate([lo, hi], axis=1).astype(BF16)
        y = b2_ref[0]
        for c in range(D_FF // FF_CHUNK):
            g_cols = slice(c * FF_CHUNK, (c + 1) * FF_CHUNK)
            l_cols = slice(D_FF + c * FF_CHUNK, D_FF + (c + 1) * FF_CHUNK)
            glu = jnp.dot(xb, w1_sc[:, g_cols], preferred_element_type=F32) + b1_ref[0, :, g_cols]
            lin = jnp.dot(xb, w1_sc[:, l_cols], preferred_element_type=F32) + b1_ref[0, :, l_cols]
            glu = jnp.minimum(glu, SWIGLU_LIMIT)
            lin = jnp.clip(lin, -SWIGLU_LIMIT, SWIGLU_LIMIT)
            act = glu * jax.nn.sigmoid(SWIGLU_ALPHA * glu) * (lin + 1.0)
            y = y + jnp.dot(act.astype(BF16), w2_sc[g_cols, :], preferred_element_type=F32)
        y_ref[...] = y

    @pl.when(i >= nused_ref[0])
    def _():
        y_ref[...] = jnp.zeros_like(y_ref)


def _experts(block_e, n_used, xbuf, w1, b1, w2, b2):
    n_rows, width = xbuf.shape
    bm = EXPERT_BM
    grid_spec = pltpu.PrefetchScalarGridSpec(
        num_scalar_prefetch=2,
        grid=(n_rows // bm,),
        in_specs=[
            pl.BlockSpec((bm, width), lambda i, be, nu: (i, 0)),
            pl.BlockSpec((1, D_MODEL, 2 * D_FF), lambda i, be, nu: (be[i], 0, 0)),
            pl.BlockSpec((1, 1, 2 * D_FF), lambda i, be, nu: (be[i], 0, 0)),
            pl.BlockSpec((1, D_FF, D_MODEL), lambda i, be, nu: (be[i], 0, 0)),
            pl.BlockSpec((1, 1, D_MODEL), lambda i, be, nu: (be[i], 0, 0)),
        ],
        out_specs=pl.BlockSpec((bm, D_MODEL), lambda i, be, nu: (i, 0)),
        scratch_shapes=[pltpu.VMEM((D_MODEL, 2 * D_FF), BF16), pltpu.VMEM((D_FF, D_MODEL), BF16)],
    )
    return pl.pallas_call(
        _expert_kernel,
        out_shape=jax.ShapeDtypeStruct((n_rows, D_MODEL), F32),
        grid_spec=grid_spec,
        compiler_params=pltpu.CompilerParams(
            dimension_semantics=("arbitrary",), vmem_limit_bytes=VMEM_LIMIT),
        name="experts",
    )(block_e, n_used, xbuf, w1, b1, w2, b2)


def _combine_kernel(dest_ref, x1_ref, w_ref, ybuf_ref, o_ref, g_sc, sem):
    tm = x1_ref.shape[0]

    def row_copy(r, k, d):
        return pltpu.make_async_copy(ybuf_ref.at[pl.ds(d, 1)], g_sc.at[k, pl.ds(r, 1)], sem)

    def start(r, carry):
        for k in range(TOP_K):
            row_copy(r, k, dest_ref[r * TOP_K + k]).start(priority=k % 2)
        return carry

    for r in range(tm):
        start(r, None)
    for k in range(TOP_K):
        pltpu.make_async_copy(ybuf_ref.at[pl.ds(0, tm)], g_sc.at[k], sem).wait()
    acc = x1_ref[...]
    w = w_ref[...]
    for k in range(TOP_K):
        acc = acc + w[:, k:k + 1] * g_sc[k]
    o_ref[...] = acc


def _combine(dest_flat, x1, top_w, ybuf):
    T = x1.shape[0]
    tm = ROUTE_TM
    return pl.pallas_call(
        _combine_kernel,
        out_shape=jax.ShapeDtypeStruct((T, D_MODEL), F32),
        grid=(T // tm,),
        in_specs=[
            pl.BlockSpec((tm * TOP_K,), lambda i: (i,), memory_space=pltpu.SMEM),
            pl.BlockSpec((tm, D_MODEL), lambda i: (i, 0)),
            pl.BlockSpec((tm, TOP_K), lambda i: (i, 0)),
            pl.BlockSpec(memory_space=pl.ANY),
        ],
        out_specs=pl.BlockSpec((tm, D_MODEL), lambda i: (i, 0)),
        scratch_shapes=[pltpu.VMEM((TOP_K, tm, D_MODEL), F32), pltpu.SemaphoreType.DMA],
        compiler_params=pltpu.CompilerParams(
            dimension_semantics=("arbitrary",), vmem_limit_bytes=VMEM_LIMIT),
        name="combine",
    )(dest_flat, x1, top_w, ybuf)


def _layer(x2, mem2, B, S, g_mix, w_in, g_sgu_v, w_spatial, b_spatial, g_q_moba, g_k_moba,
           g_mem, w_mem_kv, g_q_mem, g_k_mem, w_br_sgu, w_br_moba, w_br_mem, w_out,
           g_ffn, w_router, b_router, w_gate_up, b_gate_up, w_down, b_down):
    T = B * S
    row = lambda v: v.reshape(1, -1)
    proj = _inproj(x2, row(g_mix), w_in.astype(BF16), row(g_sgu_v))
    yb = _moba(proj, g_q_moba, g_k_moba, B, S)
    kvmem = _memkv(mem2, row(g_mem), w_mem_kv.astype(BF16), row(g_k_mem), B)

    causal = jnp.tril(jnp.ones((SGU_CHUNK, SGU_CHUNK), dtype=bool))
    ws = jnp.where(causal[None], w_spatial, 0.0).astype(BF16)
    bsp = jnp.repeat(b_spatial.T, LANES, axis=1)
    x1, h2, top_e, top_w, rank, counts = _merge(
        proj, yb, x2, kvmem, ws, bsp, row(g_q_mem), w_br_sgu.astype(BF16),
        w_br_moba.astype(BF16), w_br_mem.astype(BF16), w_out.astype(BF16), row(g_ffn),
        w_router, b_router, S)

    bm = EXPERT_BM
    n_blocks = -(-(T * TOP_K + N_EXPERTS * (bm - 1)) // bm)
    cnt = counts.reshape(N_EXPERTS).astype(jnp.int32)
    padded = (cnt + bm - 1) // bm * bm
    pad_ends = jnp.cumsum(padded)
    pad_starts = pad_ends - padded
    e_ids = jnp.arange(N_EXPERTS, dtype=jnp.int32)[:, None, None]
    start_of = jnp.sum(jnp.where(top_e[None] == e_ids, pad_starts[:, None, None], 0), axis=0)
    dest = (start_of + rank).T.reshape(T * TOP_K)
    top_w = top_w.T
    block_row0 = jnp.arange(n_blocks, dtype=jnp.int32) * bm
    block_e = jnp.minimum(
        jnp.sum((pad_ends[None, :] <= block_row0[:, None]).astype(jnp.int32), axis=1), N_EXPERTS - 1)
    n_used = (pad_ends[-1:] // bm).astype(jnp.int32)

    xbuf = _dispatch(dest, h2, n_blocks * bm)
    ybuf = _experts(block_e, n_used, xbuf, w_gate_up, b_gate_up.reshape(N_EXPERTS, 1, 2 * D_FF),
                    w_down, b_down.reshape(N_EXPERTS, 1, D_MODEL))
    return _combine(dest, x1, top_w, ybuf)


def kernel(x, mem, g_mix, w_in, g_sgu_v, w_spatial, b_spatial, g_q_moba, g_k_moba, g_mem, w_mem_kv, g_q_mem, g_k_mem, w_br_sgu, w_br_moba, w_br_mem, w_out, g_ffn, w_router, b_router, w_gate_up, b_gate_up, w_down, b_down):
    B, S, D = x.shape
    x2 = x.reshape(B * S, D)
    mem2 = mem.reshape(B * MEM_LEN, D)
    params = (g_mix, w_in, g_sgu_v, w_spatial, b_spatial, g_q_moba, g_k_moba, g_mem, w_mem_kv,
              g_q_mem, g_k_mem, w_br_sgu, w_br_moba, w_br_mem, w_out, g_ffn, w_router, b_router,
              w_gate_up, b_gate_up, w_down, b_down)
    for l in range(g_mix.shape[0]):
        x2 = _layer(x2, mem2, B, S, *(p[l] for p in params))
    return x2.reshape(B, S, D)
```

```python
import functools

import numpy as np
import jax
import jax.numpy as jnp
from jax import lax
from jax.experimental import pallas as pl
from jax.experimental.pallas import tpu as pltpu

F32 = jnp.float32
BF16 = jnp.bfloat16

D_MODEL = 1024
MEM_LEN = 256
SGU_GROUPS = 4
SGU_CHUNK = 128
SGU_WIDTH = 512
MOBA_HEADS = 8
MOBA_HEAD_DIM = 64
MOBA_BLOCK = 256
MOBA_TOPK = 3
MEM_HEADS = 4
MEM_HEAD_DIM = 128
SPLIT = 512
IN_WIDTH = 6144
N_SPLITS = IN_WIDTH // SPLIT
N_EXPERTS = 32
TOP_K = 4
D_FF = 1024
SWIGLU_ALPHA = 1.702
SWIGLU_LIMIT = 7.0
EPS = 1e-6
NEG_INF = -1e30
LANES = 128

VMEM_LIMIT = 56 * 1024 * 1024

INPROJ_TM = 256
MERGE_TM = 512
ROUTE_TM = 256
EXPERT_BM = 512


def _gelu(x):
    return 0.5 * x * (1.0 + lax.erf(x * np.float32(np.sqrt(0.5))))


def _rms(x, g):
    return x * lax.rsqrt(jnp.mean(x * x, axis=-1, keepdims=True) + EPS) * g


def _inproj_kernel(x_ref, gmix_ref, w_ref, gsgu_ref, o_ref):
    h = _rms(x_ref[...], gmix_ref[...]).astype(BF16)
    for c in range(N_SPLITS):
        sl = slice(c * SPLIT, (c + 1) * SPLIT)
        acc = jnp.dot(h, w_ref[:, sl], preferred_element_type=F32)
        if c == 0:
            val = _gelu(acc)
        elif c == 1:
            val = _rms(_gelu(acc), gsgu_ref[...])
        elif c >= 6:
            val = jax.nn.sigmoid(acc)
        else:
            val = acc
        o_ref[:, sl] = val.astype(BF16)


def _inproj(x2, g_mix, w_in, g_sgu_v):
    T = x2.shape[0]
    tm = INPROJ_TM
    return pl.pallas_call(
        _inproj_kernel,
        out_shape=jax.ShapeDtypeStruct((T, IN_WIDTH), BF16),
        grid=(T // tm,),
        in_specs=[
            pl.BlockSpec((tm, D_MODEL), lambda i: (i, 0)),
            pl.BlockSpec((1, D_MODEL), lambda i: (0, 0)),
            pl.BlockSpec((D_MODEL, IN_WIDTH), lambda i: (0, 0)),
            pl.BlockSpec((1, SGU_WIDTH), lambda i: (0, 0)),
        ],
        out_specs=pl.BlockSpec((tm, IN_WIDTH), lambda i: (i, 0)),
        compiler_params=pltpu.CompilerParams(
            dimension_semantics=("arbitrary",), vmem_limit_bytes=VMEM_LIMIT),
        name="inproj",
    )(x2, g_mix, w_in, g_sgu_v)


def _pair_norm(x, g, head0):
    sq = x * x
    s0 = jnp.sum(jnp.where(head0, sq, 0.0), axis=-1, keepdims=True)
    s1 = jnp.sum(jnp.where(head0, 0.0, sq), axis=-1, keepdims=True)
    inv = jnp.where(head0, lax.rsqrt(s0 * (1.0 / MOBA_HEAD_DIM) + EPS),
                    lax.rsqrt(s1 * (1.0 / MOBA_HEAD_DIM) + EPS))
    return x * inv * g


VT_ROWS = MOBA_HEAD_DIM + 16
LOG2E = float(np.log2(np.e))


def _moba_kernel(slopes_ref, q_ref, k_ref, v_ref, gq_ref, gk_ref, bias_ref, o_ref,
                 kn_sc, vt_sc, kmean_sc, mrow_sc, qh_sc, m_sc, acc_sc, t_sc, cmax_sc, *, nb):
    qi = pl.program_id(1)
    bs = MOBA_BLOCK
    pairs = MOBA_HEADS // 2
    head0 = lax.broadcasted_iota(jnp.int32, (1, LANES), 1) < MOBA_HEAD_DIM
    nt = (((1,), (1,)), ((), ()))

    @pl.when(qi == 0)
    def _prepare_kv():
        ones_rows = jnp.where(
            lax.broadcasted_iota(jnp.int32, (VT_ROWS - MOBA_HEAD_DIM, bs), 0) == 0, 1.0, 0.0).astype(BF16)

        def body(j, carry):
            rows = pl.ds(pl.multiple_of(j * bs, bs), bs)
            means = []
            for g in range(pairs):
                cols = slice(g * LANES, (g + 1) * LANES)
                kn = _pair_norm(k_ref[rows, cols].astype(F32), gk_ref[...], head0)
                kn_sc[j, :, cols] = kn.astype(BF16)
                means.append(jnp.mean(kn, axis=0, keepdims=True))
                vt = v_ref[rows, cols].astype(F32).T.astype(BF16)
                for hh in range(2):
                    h = 2 * g + hh
                    vt_sc[j, h, :MOBA_HEAD_DIM, :] = vt[hh * MOBA_HEAD_DIM:(hh + 1) * MOBA_HEAD_DIM]
                    vt_sc[j, h, MOBA_HEAD_DIM:, :] = ones_rows
            kmean_sc[pl.ds(j, 1), :] = jnp.concatenate(means, axis=1)
            return carry
        lax.fori_loop(0, nb, body, 0)

    blk = lax.broadcasted_iota(jnp.int32, (nb, 1), 0)
    eligible = blk < qi
    dist = ((qi - blk) * bs).astype(F32)
    for g in range(pairs):
        cols = slice(g * LANES, (g + 1) * LANES)
        qn = _pair_norm(q_ref[:, cols].astype(F32), gq_ref[...], head0)
        qpair = jnp.concatenate([jnp.where(head0, qn, 0.0), jnp.where(head0, 0.0, qn)], axis=0)
        qh_sc[g] = (qpair * (LOG2E * MOBA_HEAD_DIM ** -0.5)).astype(BF16)
        q_hi = qpair.astype(BF16)
        q_lo = (qpair - q_hi.astype(F32)).astype(BF16)
        km = kmean_sc[:, cols]
        km_hi = km.astype(BF16)
        km_lo = (km - km_hi.astype(F32)).astype(BF16)
        gate = (lax.dot_general(km_hi, q_hi, nt, preferred_element_type=F32)
                + lax.dot_general(km_hi, q_lo, nt, preferred_element_type=F32)
                + lax.dot_general(km_lo, q_hi, nt, preferred_element_type=F32))
        gate = jnp.where(eligible, gate, -jnp.inf)
        rank = jnp.zeros((nb, 2 * bs), F32)
        for i in range(nb - 1):
            gi = gate[i:i + 1, :]
            beats = jnp.logical_or(gi > gate, jnp.logical_and(gi == gate, blk > i))
            rank = rank + jnp.where(beats, 1.0, 0.0)
        sel = jnp.logical_and(eligible, rank < float(MOBA_TOPK))
        for hh in range(2):
            h = 2 * g + hh
            mrow_sc[:, h * bs:(h + 1) * bs] = jnp.where(
                blk == qi, 0.0,
                jnp.where(sel[:, hh * bs:(hh + 1) * bs], (-LOG2E * slopes_ref[h]) * dist, NEG_INF))
    m_sc[...] = jnp.full(m_sc.shape, NEG_INF, F32)
    acc_sc[...] = jnp.zeros(acc_sc.shape, F32)

    def score(j, own, slot):
        scores = [lax.dot_general(kn_sc[j, :, g * LANES:(g + 1) * LANES], qh_sc[g], nt,
                                  preferred_element_type=F32) for g in range(pairs)]
        t = jnp.concatenate(scores, axis=1) + bias_ref[1 if own else 0]
        t_sc[slot] = t
        cmax_sc[slot] = jnp.max(t, axis=0, keepdims=True)

    def accumulate(j, slot):
        mrow = mrow_sc[pl.ds(j, 1), :]
        m_old = m_sc[...]
        m_new = jnp.maximum(m_old, cmax_sc[slot] + mrow)
        shift = m_new - mrow
        alpha = jnp.exp2(m_old - m_new)
        m_sc[...] = m_new
        for h in range(MOBA_HEADS):
            cols = slice(h * bs, (h + 1) * bs)
            p = jnp.exp2(t_sc[slot, :, cols] - shift[:, cols]).astype(BF16)
            pv = jnp.dot(vt_sc[j, h], p, preferred_element_type=F32)
            acc_sc[h] = alpha[:, cols] * acc_sc[h] + pv

    n_past = qi
    score(qi, True, 0)

    def two_blocks(p, carry):
        score(2 * p, False, 1)
        accumulate(jnp.where(p == 0, qi, 2 * p - 1), 0)
        score(2 * p + 1, False, 0)
        accumulate(2 * p, 1)
        return carry
    lax.fori_loop(0, n_past // 2, two_blocks, 0)

    @pl.when(n_past % 2 == 0)
    def _():
        accumulate(jnp.where(n_past == 0, qi, n_past - 1), 0)

    @pl.when(n_past % 2 == 1)
    def _():
        score(n_past - 1, False, 1)
        accumulate(jnp.where(n_past == 1, qi, n_past - 2), 0)
        accumulate(n_past - 1, 1)

    for g in range(pairs):
        halves = []
        for hh in range(2):
            acc = acc_sc[2 * g + hh]
            halves.append(acc[:MOBA_HEAD_DIM] / acc[MOBA_HEAD_DIM:MOBA_HEAD_DIM + 1])
        out_t = jnp.concatenate(halves, axis=0)
        o_ref[:, g * LANES:(g + 1) * LANES] = out_t.T.astype(BF16)


def _moba_bias_tiles():
    slopes = 2.0 ** (-8.0 * np.arange(1, MOBA_HEADS + 1, dtype=np.float32) / MOBA_HEADS)
    kk = np.arange(MOBA_BLOCK)[:, None]
    qq = np.arange(MOBA_BLOCK)[None, :]
    rel = (qq - kk).astype(np.float64)
    past = -slopes.astype(np.float64)[:, None, None] * rel[None] * LOG2E
    own = np.where((kk <= qq)[None], past, NEG_INF)
    side_by_side = lambda a: a.transpose(1, 0, 2).reshape(MOBA_BLOCK, MOBA_HEADS * MOBA_BLOCK)
    return slopes.astype(np.float32), np.stack([side_by_side(past), side_by_side(own)]).astype(np.float32)


def _moba(proj, g_q, g_k, B, S):
    nb = S // MOBA_BLOCK
    width = MOBA_HEADS * MOBA_HEAD_DIM
    slopes, bias = _moba_bias_tiles()
    gq2 = jnp.tile(g_q.reshape(1, MOBA_HEAD_DIM), (1, 2))
    gk2 = jnp.tile(g_k.reshape(1, MOBA_HEAD_DIM), (1, 2))
    full = lambda shape: pl.BlockSpec(shape, lambda b, q, s: (0,) * len(shape))
    grid_spec = pltpu.PrefetchScalarGridSpec(
        num_scalar_prefetch=1,
        grid=(B, nb),
        in_specs=[
            pl.BlockSpec((MOBA_BLOCK, width), lambda b, q, s: (b * nb + q, 2)),
            pl.BlockSpec((S, width), lambda b, q, s: (b, 3)),
            pl.BlockSpec((S, width), lambda b, q, s: (b, 4)),
            full((1, LANES)),
            full((1, LANES)),
            full((2, MOBA_BLOCK, MOBA_HEADS * MOBA_BLOCK)),
        ],
        out_specs=pl.BlockSpec((MOBA_BLOCK, width), lambda b, q, s: (b * nb + q, 0)),
        scratch_shapes=[
            pltpu.VMEM((nb, MOBA_BLOCK, width), BF16),
            pltpu.VMEM((nb, MOBA_HEADS, VT_ROWS, MOBA_BLOCK), BF16),
            pltpu.VMEM((nb, width), F32),
            pltpu.VMEM((nb, MOBA_HEADS * MOBA_BLOCK), F32),
            pltpu.VMEM((MOBA_HEADS // 2, 2 * MOBA_BLOCK, LANES), BF16),
            pltpu.VMEM((1, MOBA_HEADS * MOBA_BLOCK), F32),
            pltpu.VMEM((MOBA_HEADS, VT_ROWS, MOBA_BLOCK), F32),
            pltpu.VMEM((2, MOBA_BLOCK, MOBA_HEADS * MOBA_BLOCK), F32),
            pltpu.VMEM((2, 1, MOBA_HEADS * MOBA_BLOCK), F32),
        ],
    )
    return pl.pallas_call(
        functools.partial(_moba_kernel, nb=nb),
        out_shape=jax.ShapeDtypeStruct((B * S, width), BF16),
        grid_spec=grid_spec,
        compiler_params=pltpu.CompilerParams(
            dimension_semantics=("arbitrary", "arbitrary"), vmem_limit_bytes=VMEM_LIMIT),
        name="moba",
    )(jnp.asarray(slopes), proj, proj, proj, gq2, gk2, jnp.asarray(bias))


def _memkv_kernel(mem_ref, gmem_ref, w_ref, gk_ref, o_ref):
    mn = _rms(mem_ref[...], gmem_ref[...]).astype(BF16)
    kv = jnp.dot(mn, w_ref[...], preferred_element_type=F32)
    width = MEM_HEADS * MEM_HEAD_DIM
    for h in range(MEM_HEADS):
        sl = slice(h * MEM_HEAD_DIM, (h + 1) * MEM_HEAD_DIM)
        o_ref[:, sl] = _rms(kv[:, sl], gk_ref[...]).astype(BF16)
    o_ref[:, width:] = kv[:, width:].astype(BF16)


def _memkv(mem2, g_mem, w_mem_kv, g_k_mem, B):
    width = MEM_HEADS * MEM_HEAD_DIM
    return pl.pallas_call(
        _memkv_kernel,
        out_shape=jax.ShapeDtypeStruct((B * MEM_LEN, 2 * width), BF16),
        grid=(B,),
        in_specs=[
            pl.BlockSpec((MEM_LEN, D_MODEL), lambda b: (b, 0)),
            pl.BlockSpec((1, D_MODEL), lambda b: (0, 0)),
            pl.BlockSpec((D_MODEL, 2 * width), lambda b: (0, 0)),
            pl.BlockSpec((1, MEM_HEAD_DIM), lambda b: (0, 0)),
        ],
        out_specs=pl.BlockSpec((MEM_LEN, 2 * width), lambda b: (b, 0)),
        compiler_params=pltpu.CompilerParams(
            dimension_semantics=("arbitrary",), vmem_limit_bytes=VMEM_LIMIT),
        name="memkv",
    )(mem2, g_mem, w_mem_kv, g_k_mem)


def _merge_kernel(gu_ref, vn_ref, yb_ref, qm_ref, ga_ref, gb_ref, gc_ref, x_ref, kv_ref,
                  ws_ref, bsp_ref, gqm_ref, wa_ref, wb_ref, wc_ref, wo_ref, gffn_ref,
                  wrh_ref, wrl_ref, br_ref,
                  x1_ref, h2_hbm, tope_ref, topw_ref, rank_ref, cnt_ref, run_sc, hp_sc, hp_sem):
    i = pl.program_id(0)
    tm = x_ref.shape[0]

    @pl.when(i == 0)
    def _():
        run_sc[...] = jnp.zeros_like(run_sc)

    ya_chunks = []
    for c in range(tm // SGU_CHUNK):
        rows = slice(c * SGU_CHUNK, (c + 1) * SGU_CHUNK)
        groups = []
        for g in range(SGU_GROUPS):
            cols = slice(g * LANES, (g + 1) * LANES)
            groups.append(jnp.dot(ws_ref[g], vn_ref[rows, cols], preferred_element_type=F32))
        mixed = jnp.concatenate(groups, axis=1) + bsp_ref[...]
        ya_chunks.append(gu_ref[rows, :].astype(F32) * mixed)
    y_a = jnp.concatenate(ya_chunks, axis=0).astype(BF16)

    width = MEM_HEADS * MEM_HEAD_DIM
    yc_heads = []
    for h in range(MEM_HEADS):
        cols = slice(h * MEM_HEAD_DIM, (h + 1) * MEM_HEAD_DIM)
        qn = _rms(qm_ref[:, cols].astype(F32), gqm_ref[...]).astype(BF16)
        s = lax.dot_general(qn, kv_ref[:, cols], (((1,), (1,)), ((), ())),
                            preferred_element_type=F32) * (MEM_HEAD_DIM ** -0.5)
        s = s - jnp.max(s, axis=-1, keepdims=True)
        p = jnp.exp(s)
        p = p / jnp.sum(p, axis=-1, keepdims=True)
        vcols = slice(width + h * MEM_HEAD_DIM, width + (h + 1) * MEM_HEAD_DIM)
        yc_heads.append(jnp.dot(p.astype(BF16), kv_ref[:, vcols], preferred_element_type=F32))
    y_c = jnp.concatenate(yc_heads, axis=1).astype(BF16)

    merged = ga_ref[...].astype(F32) * jnp.dot(y_a, wa_ref[...], preferred_element_type=F32)
    merged = merged + gb_ref[...].astype(F32) * jnp.dot(yb_ref[...], wb_ref[...],
                                                       preferred_element_type=F32)
    merged = merged + gc_ref[...].astype(F32) * jnp.dot(y_c, wc_ref[...],
                                                       preferred_element_type=F32)
    x1 = x_ref[...] + jnp.dot(merged.astype(BF16), wo_ref[...], preferred_element_type=F32)
    x1_ref[...] = x1
    h2 = _rms(x1, gffn_ref[...])
    h2_hi = h2.astype(BF16)
    h2_hi32 = h2_hi.astype(F32)
    h2_lo = (h2 - h2_hi32).astype(BF16)
    bits = lax.bitcast_convert_type(h2_hi32, jnp.uint32)
    half = D_MODEL // 2
    slot = i % 2
    hp_sc[slot] = (bits[:, :half] >> 16) | (bits[:, half:] & jnp.uint32(0xFFFF0000))

    def h2_copies(step, s):
        rows = pl.ds(pl.multiple_of(step * tm, tm), tm)
        return [pltpu.make_async_copy(hp_sc.at[s, :, c * LANES:(c + 1) * LANES], h2_hbm.at[rows, c],
                                      hp_sem.at[s]) for c in range(half // LANES)]

    for cp in h2_copies(i, slot):
        cp.start()

    nt = (((1,), (1,)), ((), ()))
    logits = (lax.dot_general(wrh_ref[...], h2_hi, nt, preferred_element_type=F32)
              + lax.dot_general(wrh_ref[...], h2_lo, nt, preferred_element_type=F32)
              + lax.dot_general(wrl_ref[...], h2_hi, nt, preferred_element_type=F32)
              + br_ref[...])
    row = lax.broadcasted_iota(jnp.int32, (N_EXPERTS, tm), 0)
    work = logits
    top_e, top_l = [], []
    for _ in range(TOP_K):
        mx = jnp.max(work, axis=0, keepdims=True)
        idx = jnp.min(jnp.where(work == mx, row, N_EXPERTS), axis=0, keepdims=True)
        top_e.append(idx)
        top_l.append(mx)
        work = jnp.where(row == idx, -jnp.inf, work)
    exps = [jnp.exp(v - top_l[0]) for v in top_l]
    denom = exps[0] + exps[1] + exps[2] + exps[3]

    onehots = [row == e for e in top_e]
    multi = jnp.zeros((N_EXPERTS, tm), F32)
    for oh in onehots:
        multi = multi + jnp.where(oh, 1.0, 0.0)
    r_i = lax.broadcasted_iota(jnp.int32, (tm, tm), 0)
    c_i = lax.broadcasted_iota(jnp.int32, (tm, tm), 1)
    earlier = jnp.where(r_i < c_i, 1.0, 0.0).astype(BF16)
    pos = run_sc[...] + jnp.dot(multi.astype(BF16), earlier, preferred_element_type=F32)
    run_sc[...] = run_sc[...] + jnp.sum(multi, axis=1, keepdims=True)
    cnt_ref[...] = run_sc[...]

    for k in range(TOP_K):
        tope_ref[k:k + 1, :] = top_e[k]
        topw_ref[k:k + 1, :] = exps[k] / denom
        rank_ref[k:k + 1, :] = jnp.sum(jnp.where(onehots[k], pos, 0.0), axis=0,
                                       keepdims=True).astype(jnp.int32)

    @pl.when(i >= 1)
    def _():
        for cp in h2_copies(i - 1, 1 - slot):
            cp.wait()

    @pl.when(i == pl.num_programs(0) - 1)
    def _():
        for cp in h2_copies(i, slot):
            cp.wait()


def _merge(proj, yb, x2, kvmem, ws, bsp, g_q_mem, wa, wb, wc, wo, g_ffn, w_router, b_router, S):
    T = x2.shape[0]
    tm = MERGE_TM
    per_b = S // tm
    full = lambda shape: pl.BlockSpec(shape, lambda i: (0,) * len(shape))
    in_specs = [
        pl.BlockSpec((tm, SPLIT), lambda i: (i, 0)),
        pl.BlockSpec((tm, SPLIT), lambda i: (i, 1)),
        pl.BlockSpec((tm, SPLIT), lambda i: (i, 0)),
        pl.BlockSpec((tm, SPLIT), lambda i: (i, 5)),
        pl.BlockSpec((tm, D_MODEL), lambda i: (i, 3)),
        pl.BlockSpec((tm, D_MODEL), lambda i: (i, 4)),
        pl.BlockSpec((tm, D_MODEL), lambda i: (i, 5)),
        pl.BlockSpec((tm, D_MODEL), lambda i: (i, 0)),
        pl.BlockSpec((MEM_LEN, 2 * MEM_HEADS * MEM_HEAD_DIM), lambda i: (i // per_b, 0)),
        full((SGU_GROUPS, SGU_CHUNK, SGU_CHUNK)),
        full((SGU_CHUNK, SGU_WIDTH)),
        full((1, MEM_HEAD_DIM)),
        full((SGU_WIDTH, D_MODEL)),
        full((SGU_WIDTH, D_MODEL)),
        full((SGU_WIDTH, D_MODEL)),
        full((D_MODEL, D_MODEL)),
        full((1, D_MODEL)),
        full((N_EXPERTS, D_MODEL)),
        full((N_EXPERTS, D_MODEL)),
        full((N_EXPERTS, 1)),
    ]
    out_shape = (
        jax.ShapeDtypeStruct((T, D_MODEL), F32),
        jax.ShapeDtypeStruct((T, D_MODEL // 2 // LANES, LANES), jnp.uint32),
        jax.ShapeDtypeStruct((TOP_K, T), jnp.int32),
        jax.ShapeDtypeStruct((TOP_K, T), F32),
        jax.ShapeDtypeStruct((TOP_K, T), jnp.int32),
        jax.ShapeDtypeStruct((N_EXPERTS, 1), F32),
    )
    out_specs = (
        pl.BlockSpec((tm, D_MODEL), lambda i: (i, 0)),
        pl.BlockSpec(memory_space=pl.ANY),
        pl.BlockSpec((TOP_K, tm), lambda i: (0, i)),
        pl.BlockSpec((TOP_K, tm), lambda i: (0, i)),
        pl.BlockSpec((TOP_K, tm), lambda i: (0, i)),
        pl.BlockSpec((N_EXPERTS, 1), lambda i: (0, 0)),
    )
    wr_t = w_router.T
    wr_hi = wr_t.astype(BF16)
    wr_lo = (wr_t - wr_hi.astype(F32)).astype(BF16)
    return pl.pallas_call(
        _merge_kernel,
        out_shape=out_shape,
        grid=(T // tm,),
        in_specs=in_specs,
        out_specs=out_specs,
        scratch_shapes=[pltpu.VMEM((N_EXPERTS, 1), F32),
                        pltpu.VMEM((2, tm, D_MODEL // 2), jnp.uint32), pltpu.SemaphoreType.DMA((2,))],
        compiler_params=pltpu.CompilerParams(
            dimension_semantics=("arbitrary",), vmem_limit_bytes=VMEM_LIMIT),
        name="merge",
    )(proj, proj, yb, proj, proj, proj, proj, x2, kvmem, ws, bsp, g_q_mem,
      wa, wb, wc, wo, g_ffn, wr_hi, wr_lo, b_router.reshape(N_EXPERTS, 1))


def _dispatch_kernel(dest_ref, h_ref, xin_ref, xbuf_ref, sem):
    del xin_ref
    tm = h_ref.shape[0]

    def row_copy(r, d):
        return pltpu.make_async_copy(h_ref.at[r], xbuf_ref.at[d], sem)

    def start(r, carry):
        for k in range(TOP_K):
            row_copy(r, dest_ref[r * TOP_K + k]).start(priority=k % 2)
        return carry

    for r in range(tm):
        start(r, None)
    for _ in range(TOP_K):
        pltpu.make_async_copy(h_ref, xbuf_ref.at[pl.ds(0, tm)], sem).wait()


def _dispatch(dest_flat, h2, n_rows):
    T, slabs, lanes = h2.shape
    tm = ROUTE_TM
    xbuf0 = jnp.zeros((n_rows, slabs, lanes), h2.dtype)
    return pl.pallas_call(
        _dispatch_kernel,
        out_shape=jax.ShapeDtypeStruct((n_rows, slabs, lanes), h2.dtype),
        grid=(T // tm,),
        in_specs=[
            pl.BlockSpec((tm * TOP_K,), lambda i: (i,), memory_space=pltpu.SMEM),
            pl.BlockSpec((tm, slabs, lanes), lambda i: (i, 0, 0)),
            pl.BlockSpec(memory_space=pl.ANY),
        ],
        out_specs=pl.BlockSpec(memory_space=pl.ANY),
        scratch_shapes=[pltpu.SemaphoreType.DMA],
        input_output_aliases={2: 0},
        compiler_params=pltpu.CompilerParams(
            dimension_semantics=("arbitrary",), vmem_limit_bytes=VMEM_LIMIT),
        name="dispatch",
    )(dest_flat, h2, xbuf0)


FF_CHUNK = 512


def _expert_kernel(be_ref, nused_ref, x_hbm, w1_ref, b1_ref, w2_ref, b2_ref,
                   y_ref, w1_sc, w2_sc, x_sc, x_sem):
    i = pl.program_id(0)
    bm = y_ref.shape[0]
    n_used = nused_ref[0]

    def x_copies(blk, slot):
        rows = pl.ds(pl.multiple_of(blk * bm, bm), bm)
        return [pltpu.make_async_copy(x_hbm.at[rows, c], x_sc.at[slot, :, c * LANES:(c + 1) * LANES],
                                      x_sem.at[slot]) for c in range(x_hbm.shape[1])]

    @pl.when(i < n_used)
    def _():
        slot = i % 2

        @pl.when(i == 0)
        def _():
            for cp in x_copies(0, 0):
                cp.start()

        @pl.when(i + 1 < n_used)
        def _():
            for cp in x_copies(i + 1, 1 - slot):
                cp.start()

        @pl.when(jnp.logical_or(i == 0, be_ref[i] != be_ref[jnp.maximum(i - 1, 0)]))
        def _():
            w1_sc[...] = w1_ref[0].astype(BF16)
            w2_sc[...] = w2_ref[0].astype(BF16)

        for cp in x_copies(i, slot):
            cp.wait()
        packed = x_sc[slot]
        lo = lax.bitcast_convert_type(packed << 16, F32)
        hi = lax.bitcast_convert_type(packed & jnp.uint32(0xFFFF0000), F32)
        xb = jnp.concatenate([lo, hi], axis=1).astype(BF16)
        y = b2_ref[0]
        for c in range(D_FF // FF_CHUNK):
            g_cols = slice(c * FF_CHUNK, (c + 1) * FF_CHUNK)
            l_cols = slice(D_FF + c * FF_CHUNK, D_FF + (c + 1) * FF_CHUNK)
            glu = jnp.dot(xb, w1_sc[:, g_cols], preferred_element_type=F32) + b1_ref[0, :, g_cols]
            lin = jnp.dot(xb, w1_sc[:, l_cols], preferred_element_type=F32) + b1_ref[0, :, l_cols]
            glu = jnp.minimum(glu, SWIGLU_LIMIT)
            lin = jnp.clip(lin, -SWIGLU_LIMIT, SWIGLU_LIMIT)
            act = glu * jax.nn.sigmoid(SWIGLU_ALPHA * glu) * (lin + 1.0)
            y = y + jnp.dot(act.astype(BF16), w2_sc[g_cols, :], preferred_element_type=F32)
        y_ref[...] = y

    @pl.when(i >= nused_ref[0])
    def _():
        y_ref[...] = jnp.zeros_like(y_ref)


def _experts(block_e, n_used, xbuf, w1, b1, w2, b2):
    n_rows, slabs, lanes = xbuf.shape
    bm = EXPERT_BM
    grid_spec = pltpu.PrefetchScalarGridSpec(
        num_scalar_prefetch=2,
        grid=(n_rows // bm,),
        in_specs=[
            pl.BlockSpec(memory_space=pl.ANY),
            pl.BlockSpec((1, D_MODEL, 2 * D_FF), lambda i, be, nu: (be[i], 0, 0)),
            pl.BlockSpec((1, 1, 2 * D_FF), lambda i, be, nu: (be[i], 0, 0)),
            pl.BlockSpec((1, D_FF, D_MODEL), lambda i, be, nu: (be[i], 0, 0)),
            pl.BlockSpec((1, 1, D_MODEL), lambda i, be, nu: (be[i], 0, 0)),
        ],
        out_specs=pl.BlockSpec((bm, D_MODEL), lambda i, be, nu: (i, 0)),
        scratch_shapes=[pltpu.VMEM((D_MODEL, 2 * D_FF), BF16), pltpu.VMEM((D_FF, D_MODEL), BF16),
                        pltpu.VMEM((2, bm, slabs * lanes), jnp.uint32), pltpu.SemaphoreType.DMA((2,))],
    )
    return pl.pallas_call(
        _expert_kernel,
        out_shape=jax.ShapeDtypeStruct((n_rows, D_MODEL), F32),
        grid_spec=grid_spec,
        compiler_params=pltpu.CompilerParams(
            dimension_semantics=("arbitrary",), vmem_limit_bytes=VMEM_LIMIT),
        name="experts",
    )(block_e, n_used, xbuf, w1, b1, w2, b2)


def _combine_kernel(dest_ref, x1_ref, w_ref, ybuf_ref, o_ref, g_sc, sem):
    tm = x1_ref.shape[0]

    def row_copy(r, k, d):
        return pltpu.make_async_copy(ybuf_ref.at[pl.ds(d, 1)], g_sc.at[k, pl.ds(r, 1)], sem)

    def start(r, carry):
        for k in range(TOP_K):
            row_copy(r, k, dest_ref[r * TOP_K + k]).start(priority=k % 2)
        return carry

    for r in range(tm):
        start(r, None)
    for k in range(TOP_K):
        pltpu.make_async_copy(ybuf_ref.at[pl.ds(0, tm)], g_sc.at[k], sem).wait()
    acc = x1_ref[...]
    w = w_ref[...]
    for k in range(TOP_K):
        acc = acc + w[:, k:k + 1] * g_sc[k]
    o_ref[...] = acc


def _combine(dest_flat, x1, top_w, ybuf):
    T = x1.shape[0]
    tm = ROUTE_TM
    return pl.pallas_call(
        _combine_kernel,
        out_shape=jax.ShapeDtypeStruct((T, D_MODEL), F32),
        grid=(T // tm,),
        in_specs=[
            pl.BlockSpec((tm * TOP_K,), lambda i: (i,), memory_space=pltpu.SMEM),
            pl.BlockSpec((tm, D_MODEL), lambda i: (i, 0)),
            pl.BlockSpec((tm, TOP_K), lambda i: (i, 0)),
            pl.BlockSpec(memory_space=pl.ANY),
        ],
        out_specs=pl.BlockSpec((tm, D_MODEL), lambda i: (i, 0)),
        scratch_shapes=[pltpu.VMEM((TOP_K, tm, D_MODEL), F32), pltpu.SemaphoreType.DMA],
        compiler_params=pltpu.CompilerParams(
            dimension_semantics=("arbitrary",), vmem_limit_bytes=VMEM_LIMIT),
        name="combine",
    )(dest_flat, x1, top_w, ybuf)


def _layer(x2, mem2, B, S, g_mix, w_in, g_sgu_v, w_spatial, b_spatial, g_q_moba, g_k_moba,
           g_mem, w_mem_kv, g_q_mem, g_k_mem, w_br_sgu, w_br_moba, w_br_mem, w_out,
           g_ffn, w_router, b_router, w_gate_up, b_gate_up, w_down, b_down):
    T = B * S
    row = lambda v: v.reshape(1, -1)
    proj = _inproj(x2, row(g_mix), w_in.astype(BF16), row(g_sgu_v))
    yb = _moba(proj, g_q_moba, g_k_moba, B, S)
    kvmem = _memkv(mem2, row(g_mem), w_mem_kv.astype(BF16), row(g_k_mem), B)

    causal = jnp.tril(jnp.ones((SGU_CHUNK, SGU_CHUNK), dtype=bool))
    ws = jnp.where(causal[None], w_spatial, 0.0).astype(BF16)
    bsp = jnp.repeat(b_spatial.T, LANES, axis=1)
    x1, h2, top_e, top_w, rank, counts = _merge(
        proj, yb, x2, kvmem, ws, bsp, row(g_q_mem), w_br_sgu.astype(BF16),
        w_br_moba.astype(BF16), w_br_mem.astype(BF16), w_out.astype(BF16), row(g_ffn),
        w_router, b_router, S)

    bm = EXPERT_BM
    n_blocks = -(-(T * TOP_K + N_EXPERTS * (bm - 1)) // bm)
    cnt = counts.reshape(N_EXPERTS).astype(jnp.int32)
    padded = (cnt + bm - 1) // bm * bm
    pad_ends = jnp.cumsum(padded)
    pad_starts = pad_ends - padded
    e_ids = jnp.arange(N_EXPERTS, dtype=jnp.int32)[:, None, None]
    start_of = jnp.sum(jnp.where(top_e[None] == e_ids, pad_starts[:, None, None], 0), axis=0)
    dest = (start_of + rank).T.reshape(T * TOP_K)
    top_w = top_w.T
    block_row0 = jnp.arange(n_blocks, dtype=jnp.int32) * bm
    block_e = jnp.minimum(
        jnp.sum((pad_ends[None, :] <= block_row0[:, None]).astype(jnp.int32), axis=1), N_EXPERTS - 1)
    n_used = (pad_ends[-1:] // bm).astype(jnp.int32)

    xbuf = _dispatch(dest, h2, n_blocks * bm)
    ybuf = _experts(block_e, n_used, xbuf, w_gate_up, b_gate_up.reshape(N_EXPERTS, 1, 2 * D_FF),
                    w_down, b_down.reshape(N_EXPERTS, 1, D_MODEL))
    return _combine(dest, x1, top_w, ybuf)


def kernel(x, mem, g_mix, w_in, g_sgu_v, w_spatial, b_spatial, g_q_moba, g_k_moba, g_mem, w_mem_kv, g_q_mem, g_k_mem, w_br_sgu, w_br_moba, w_br_mem, w_out, g_ffn, w_router, b_router, w_gate_up, b_gate_up, w_down, b_down):
    B, S, D = x.shape
    x2 = x.reshape(B * S, D)
    mem2 = mem.reshape(B * MEM_LEN, D)
    params = (g_mix, w_in, g_sgu_v, w_spatial, b_spatial, g_q_moba, g_k_moba, g_mem, w_mem_kv,
              g_q_mem, g_k_mem, w_br_sgu, w_br_moba, w_br_mem, w_out, g_ffn, w_router, b_router,
              w_gate_up, b_gate_up, w_down, b_down)
    for l in range(g_mix.shape[0]):
        x2 = _layer(x2, mem2, B, S, *(p[l] for p in params))
    return x2.reshape(B, S, D)
```

```python
import functools

import numpy as np
import jax
import jax.numpy as jnp
from jax import lax
from jax.experimental import pallas as pl
from jax.experimental.pallas import tpu as pltpu

F32 = jnp.float32
BF16 = jnp.bfloat16

D_MODEL = 1024
MEM_LEN = 256
SGU_GROUPS = 4
SGU_CHUNK = 128
SGU_WIDTH = 512
MOBA_HEADS = 8
MOBA_HEAD_DIM = 64
MOBA_BLOCK = 256
MOBA_TOPK = 3
MEM_HEADS = 4
MEM_HEAD_DIM = 128
SPLIT = 512
IN_WIDTH = 6144
N_SPLITS = IN_WIDTH // SPLIT
N_EXPERTS = 32
TOP_K = 4
D_FF = 1024
SWIGLU_ALPHA = 1.702
SWIGLU_LIMIT = 7.0
EPS = 1e-6
NEG_INF = -1e30
LANES = 128

VMEM_LIMIT = 56 * 1024 * 1024

INPROJ_TM = 256
MERGE_TM = 512
ROUTE_TM = 256
EXPERT_BM = 512


def _gelu(x):
    return 0.5 * x * (1.0 + lax.erf(x * np.float32(np.sqrt(0.5))))


def _rms(x, g):
    return x * lax.rsqrt(jnp.mean(x * x, axis=-1, keepdims=True) + EPS) * g


def _inproj_kernel(x_ref, gmix_ref, w_ref, gsgu_ref, o_ref):
    h = _rms(x_ref[...], gmix_ref[...]).astype(BF16)
    for c in range(N_SPLITS):
        sl = slice(c * SPLIT, (c + 1) * SPLIT)
        acc = jnp.dot(h, w_ref[:, sl], preferred_element_type=F32)
        if c == 0:
            val = _gelu(acc)
        elif c == 1:
            val = _rms(_gelu(acc), gsgu_ref[...])
        elif c >= 6:
            val = jax.nn.sigmoid(acc)
        else:
            val = acc
        o_ref[:, sl] = val.astype(BF16)


def _inproj(x2, g_mix, w_in, g_sgu_v):
    T = x2.shape[0]
    tm = INPROJ_TM
    return pl.pallas_call(
        _inproj_kernel,
        out_shape=jax.ShapeDtypeStruct((T, IN_WIDTH), BF16),
        grid=(T // tm,),
        in_specs=[
            pl.BlockSpec((tm, D_MODEL), lambda i: (i, 0)),
            pl.BlockSpec((1, D_MODEL), lambda i: (0, 0)),
            pl.BlockSpec((D_MODEL, IN_WIDTH), lambda i: (0, 0)),
            pl.BlockSpec((1, SGU_WIDTH), lambda i: (0, 0)),
        ],
        out_specs=pl.BlockSpec((tm, IN_WIDTH), lambda i: (i, 0)),
        compiler_params=pltpu.CompilerParams(
            dimension_semantics=("arbitrary",), vmem_limit_bytes=VMEM_LIMIT),
        name="inproj",
    )(x2, g_mix, w_in, g_sgu_v)


def _pair_norm(x, g, head0):
    sq = x * x
    s0 = jnp.sum(jnp.where(head0, sq, 0.0), axis=-1, keepdims=True)
    s1 = jnp.sum(jnp.where(head0, 0.0, sq), axis=-1, keepdims=True)
    inv = jnp.where(head0, lax.rsqrt(s0 * (1.0 / MOBA_HEAD_DIM) + EPS),
                    lax.rsqrt(s1 * (1.0 / MOBA_HEAD_DIM) + EPS))
    return x * inv * g


VT_ROWS = MOBA_HEAD_DIM + 16
LOG2E = float(np.log2(np.e))


def _moba_kernel(slopes_ref, q_ref, k_ref, v_ref, gq_ref, gk_ref, bias_ref, o_ref,
                 kn_sc, vt_sc, kmean_sc, mrow_sc, qh_sc, m_sc, acc_sc, t_sc, cmax_sc, *, nb):
    qi = pl.program_id(1)
    bs = MOBA_BLOCK
    pairs = MOBA_HEADS // 2
    head0 = lax.broadcasted_iota(jnp.int32, (1, LANES), 1) < MOBA_HEAD_DIM
    nt = (((1,), (1,)), ((), ()))

    @pl.when(qi == 0)
    def _prepare_kv():
        ones_rows = jnp.where(
            lax.broadcasted_iota(jnp.int32, (VT_ROWS - MOBA_HEAD_DIM, bs), 0) == 0, 1.0, 0.0).astype(BF16)

        def body(j, carry):
            rows = pl.ds(pl.multiple_of(j * bs, bs), bs)
            means = []
            for g in range(pairs):
                cols = slice(g * LANES, (g + 1) * LANES)
                kn = _pair_norm(k_ref[rows, cols].astype(F32), gk_ref[...], head0)
                kn_sc[j, :, cols] = kn.astype(BF16)
                means.append(jnp.mean(kn, axis=0, keepdims=True))
                vt = v_ref[rows, cols].astype(F32).T.astype(BF16)
                for hh in range(2):
                    h = 2 * g + hh
                    vt_sc[j, h, :MOBA_HEAD_DIM, :] = vt[hh * MOBA_HEAD_DIM:(hh + 1) * MOBA_HEAD_DIM]
                    vt_sc[j, h, MOBA_HEAD_DIM:, :] = ones_rows
            kmean_sc[pl.ds(j, 1), :] = jnp.concatenate(means, axis=1)
            return carry
        lax.fori_loop(0, nb, body, 0)

    blk = lax.broadcasted_iota(jnp.int32, (nb, 1), 0)
    eligible = blk < qi
    dist = ((qi - blk) * bs).astype(F32)
    for g in range(pairs):
        cols = slice(g * LANES, (g + 1) * LANES)
        qn = _pair_norm(q_ref[:, cols].astype(F32), gq_ref[...], head0)
        qpair = jnp.concatenate([jnp.where(head0, qn, 0.0), jnp.where(head0, 0.0, qn)], axis=0)
        qh_sc[g] = (qpair * (LOG2E * MOBA_HEAD_DIM ** -0.5)).astype(BF16)
        q_hi = qpair.astype(BF16)
        q_lo = (qpair - q_hi.astype(F32)).astype(BF16)
        km = kmean_sc[:, cols]
        km_hi = km.astype(BF16)
        km_lo = (km - km_hi.astype(F32)).astype(BF16)
        gate = (lax.dot_general(km_hi, q_hi, nt, preferred_element_type=F32)
                + lax.dot_general(km_hi, q_lo, nt, preferred_element_type=F32)
                + lax.dot_general(km_lo, q_hi, nt, preferred_element_type=F32))
        gate = jnp.where(eligible, gate, -jnp.inf)
        rank = jnp.zeros((nb, 2 * bs), F32)
        for i in range(nb - 1):
            gi = gate[i:i + 1, :]
            beats = jnp.logical_or(gi > gate, jnp.logical_and(gi == gate, blk > i))
            rank = rank + jnp.where(beats, 1.0, 0.0)
        sel = jnp.logical_and(eligible, rank < float(MOBA_TOPK))
        for hh in range(2):
            h = 2 * g + hh
            mrow_sc[:, h * bs:(h + 1) * bs] = jnp.where(
                blk == qi, 0.0,
                jnp.where(sel[:, hh * bs:(hh + 1) * bs], (-LOG2E * slopes_ref[h]) * dist, NEG_INF))
    m_sc[...] = jnp.full(m_sc.shape, NEG_INF, F32)
    acc_sc[...] = jnp.zeros(acc_sc.shape, F32)

    def score(j, own, slot):
        scores = [lax.dot_general(kn_sc[j, :, g * LANES:(g + 1) * LANES], qh_sc[g], nt,
                                  preferred_element_type=F32) for g in range(pairs)]
        t = jnp.concatenate(scores, axis=1) + bias_ref[1 if own else 0]
        t_sc[slot] = t
        cmax_sc[slot] = jnp.max(t, axis=0, keepdims=True)

    def accumulate(j, slot):
        mrow = mrow_sc[pl.ds(j, 1), :]
        m_old = m_sc[...]
        m_new = jnp.maximum(m_old, cmax_sc[slot] + mrow)
        shift = m_new - mrow
        alpha = jnp.exp2(m_old - m_new)
        m_sc[...] = m_new
        for h in range(MOBA_HEADS):
            cols = slice(h * bs, (h + 1) * bs)
            p = jnp.exp2(t_sc[slot, :, cols] - shift[:, cols]).astype(BF16)
            pv = jnp.dot(vt_sc[j, h], p, preferred_element_type=F32)
            acc_sc[h] = alpha[:, cols] * acc_sc[h] + pv

    n_past = qi
    score(qi, True, 0)

    def two_blocks(p, carry):
        score(2 * p, False, 1)
        accumulate(jnp.where(p == 0, qi, 2 * p - 1), 0)
        score(2 * p + 1, False, 0)
        accumulate(2 * p, 1)
        return carry
    lax.fori_loop(0, n_past // 2, two_blocks, 0)

    @pl.when(n_past % 2 == 0)
    def _():
        accumulate(jnp.where(n_past == 0, qi, n_past - 1), 0)

    @pl.when(n_past % 2 == 1)
    def _():
        score(n_past - 1, False, 1)
        accumulate(jnp.where(n_past == 1, qi, n_past - 2), 0)
        accumulate(n_past - 1, 1)

    for g in range(pairs):
        halves = []
        for hh in range(2):
            acc = acc_sc[2 * g + hh]
            halves.append(acc[:MOBA_HEAD_DIM] / acc[MOBA_HEAD_DIM:MOBA_HEAD_DIM + 1])
        out_t = jnp.concatenate(halves, axis=0)
        o_ref[:, g * LANES:(g + 1) * LANES] = out_t.T.astype(BF16)


def _moba_bias_tiles():
    slopes = 2.0 ** (-8.0 * np.arange(1, MOBA_HEADS + 1, dtype=np.float32) / MOBA_HEADS)
    kk = np.arange(MOBA_BLOCK)[:, None]
    qq = np.arange(MOBA_BLOCK)[None, :]
    rel = (qq - kk).astype(np.float64)
    past = -slopes.astype(np.float64)[:, None, None] * rel[None] * LOG2E
    own = np.where((kk <= qq)[None], past, NEG_INF)
    side_by_side = lambda a: a.transpose(1, 0, 2).reshape(MOBA_BLOCK, MOBA_HEADS * MOBA_BLOCK)
    return slopes.astype(np.float32), np.stack([side_by_side(past), side_by_side(own)]).astype(np.float32)


def _moba(proj, g_q, g_k, B, S):
    nb = S // MOBA_BLOCK
    width = MOBA_HEADS * MOBA_HEAD_DIM
    slopes, bias = _moba_bias_tiles()
    gq2 = jnp.tile(g_q.reshape(1, MOBA_HEAD_DIM), (1, 2))
    gk2 = jnp.tile(g_k.reshape(1, MOBA_HEAD_DIM), (1, 2))
    full = lambda shape: pl.BlockSpec(shape, lambda b, q, s: (0,) * len(shape))
    grid_spec = pltpu.PrefetchScalarGridSpec(
        num_scalar_prefetch=1,
        grid=(B, nb),
        in_specs=[
            pl.BlockSpec((MOBA_BLOCK, width), lambda b, q, s: (b * nb + q, 2)),
            pl.BlockSpec((S, width), lambda b, q, s: (b, 3)),
            pl.BlockSpec((S, width), lambda b, q, s: (b, 4)),
            full((1, LANES)),
            full((1, LANES)),
            full((2, MOBA_BLOCK, MOBA_HEADS * MOBA_BLOCK)),
        ],
        out_specs=pl.BlockSpec((MOBA_BLOCK, width), lambda b, q, s: (b * nb + q, 0)),
        scratch_shapes=[
            pltpu.VMEM((nb, MOBA_BLOCK, width), BF16),
            pltpu.VMEM((nb, MOBA_HEADS, VT_ROWS, MOBA_BLOCK), BF16),
            pltpu.VMEM((nb, width), F32),
            pltpu.VMEM((nb, MOBA_HEADS * MOBA_BLOCK), F32),
            pltpu.VMEM((MOBA_HEADS // 2, 2 * MOBA_BLOCK, LANES), BF16),
            pltpu.VMEM((1, MOBA_HEADS * MOBA_BLOCK), F32),
            pltpu.VMEM((MOBA_HEADS, VT_ROWS, MOBA_BLOCK), F32),
            pltpu.VMEM((2, MOBA_BLOCK, MOBA_HEADS * MOBA_BLOCK), F32),
            pltpu.VMEM((2, 1, MOBA_HEADS * MOBA_BLOCK), F32),
        ],
    )
    return pl.pallas_call(
        functools.partial(_moba_kernel, nb=nb),
        out_shape=jax.ShapeDtypeStruct((B * S, width), BF16),
        grid_spec=grid_spec,
        compiler_params=pltpu.CompilerParams(
            dimension_semantics=("arbitrary", "arbitrary"), vmem_limit_bytes=VMEM_LIMIT),
        name="moba",
    )(jnp.asarray(slopes), proj, proj, proj, gq2, gk2, jnp.asarray(bias))


def _memkv_kernel(mem_ref, gmem_ref, w_ref, gk_ref, o_ref):
    mn = _rms(mem_ref[...], gmem_ref[...]).astype(BF16)
    kv = jnp.dot(mn, w_ref[...], preferred_element_type=F32)
    width = MEM_HEADS * MEM_HEAD_DIM
    for h in range(MEM_HEADS):
        sl = slice(h * MEM_HEAD_DIM, (h + 1) * MEM_HEAD_DIM)
        o_ref[:, sl] = _rms(kv[:, sl], gk_ref[...]).astype(BF16)
    o_ref[:, width:] = kv[:, width:].astype(BF16)


def _memkv(mem2, g_mem, w_mem_kv, g_k_mem, B):
    width = MEM_HEADS * MEM_HEAD_DIM
    return pl.pallas_call(
        _memkv_kernel,
        out_shape=jax.ShapeDtypeStruct((B * MEM_LEN, 2 * width), BF16),
        grid=(B,),
        in_specs=[
            pl.BlockSpec((MEM_LEN, D_MODEL), lambda b: (b, 0)),
            pl.BlockSpec((1, D_MODEL), lambda b: (0, 0)),
            pl.BlockSpec((D_MODEL, 2 * width), lambda b: (0, 0)),
            pl.BlockSpec((1, MEM_HEAD_DIM), lambda b: (0, 0)),
        ],
        out_specs=pl.BlockSpec((MEM_LEN, 2 * width), lambda b: (b, 0)),
        compiler_params=pltpu.CompilerParams(
            dimension_semantics=("arbitrary",), vmem_limit_bytes=VMEM_LIMIT),
        name="memkv",
    )(mem2, g_mem, w_mem_kv, g_k_mem)


def _merge_kernel(gu_ref, vn_ref, yb_ref, qm_ref, ga_ref, gb_ref, gc_ref, x_ref, kv_ref,
                  ws_ref, bsp_ref, gqm_ref, wa_ref, wb_ref, wc_ref, wo_ref, gffn_ref,
                  wrh_ref, wrl_ref, br_ref,
                  x1_ref, h2_hbm, tope_ref, topw_ref, rank_ref, cnt_ref, run_sc, hp_sc, hp_sem):
    i = pl.program_id(0)
    tm = x_ref.shape[0]

    @pl.when(i == 0)
    def _():
        run_sc[...] = jnp.zeros_like(run_sc)

    ya_chunks = []
    for c in range(tm // SGU_CHUNK):
        rows = slice(c * SGU_CHUNK, (c + 1) * SGU_CHUNK)
        groups = []
        for g in range(SGU_GROUPS):
            cols = slice(g * LANES, (g + 1) * LANES)
            groups.append(jnp.dot(ws_ref[g], vn_ref[rows, cols], preferred_element_type=F32))
        mixed = jnp.concatenate(groups, axis=1) + bsp_ref[...]
        ya_chunks.append(gu_ref[rows, :].astype(F32) * mixed)
    y_a = jnp.concatenate(ya_chunks, axis=0).astype(BF16)

    width = MEM_HEADS * MEM_HEAD_DIM
    yc_heads = []
    for h in range(MEM_HEADS):
        cols = slice(h * MEM_HEAD_DIM, (h + 1) * MEM_HEAD_DIM)
        qn = _rms(qm_ref[:, cols].astype(F32), gqm_ref[...]).astype(BF16)
        s = lax.dot_general(qn, kv_ref[:, cols], (((1,), (1,)), ((), ())),
                            preferred_element_type=F32) * (MEM_HEAD_DIM ** -0.5)
        s = s - jnp.max(s, axis=-1, keepdims=True)
        p = jnp.exp(s)
        p = p / jnp.sum(p, axis=-1, keepdims=True)
        vcols = slice(width + h * MEM_HEAD_DIM, width + (h + 1) * MEM_HEAD_DIM)
        yc_heads.append(jnp.dot(p.astype(BF16), kv_ref[:, vcols], preferred_element_type=F32))
    y_c = jnp.concatenate(yc_heads, axis=1).astype(BF16)

    merged = ga_ref[...].astype(F32) * jnp.dot(y_a, wa_ref[...], preferred_element_type=F32)
    merged = merged + gb_ref[...].astype(F32) * jnp.dot(yb_ref[...], wb_ref[...],
                                                       preferred_element_type=F32)
    merged = merged + gc_ref[...].astype(F32) * jnp.dot(y_c, wc_ref[...],
                                                       preferred_element_type=F32)
    x1 = x_ref[...] + jnp.dot(merged.astype(BF16), wo_ref[...], preferred_element_type=F32)
    x1_ref[...] = x1
    h2 = _rms(x1, gffn_ref[...])
    h2_hi = h2.astype(BF16)
    h2_hi32 = h2_hi.astype(F32)
    h2_lo = (h2 - h2_hi32).astype(BF16)
    bits = lax.bitcast_convert_type(h2_hi32, jnp.uint32)
    half = D_MODEL // 2
    slot = i % 2
    hp_sc[slot] = (bits[:, :half] >> 16) | (bits[:, half:] & jnp.uint32(0xFFFF0000))

    def h2_copies(step, s):
        rows = pl.ds(pl.multiple_of(step * tm, tm), tm)
        return [pltpu.make_async_copy(hp_sc.at[s, :, c * LANES:(c + 1) * LANES], h2_hbm.at[rows, c],
                                      hp_sem.at[s]) for c in range(half // LANES)]

    for cp in h2_copies(i, slot):
        cp.start()

    nt = (((1,), (1,)), ((), ()))
    logits = (lax.dot_general(wrh_ref[...], h2_hi, nt, preferred_element_type=F32)
              + lax.dot_general(wrh_ref[...], h2_lo, nt, preferred_element_type=F32)
              + lax.dot_general(wrl_ref[...], h2_hi, nt, preferred_element_type=F32)
              + br_ref[...])
    row = lax.broadcasted_iota(jnp.int32, (N_EXPERTS, tm), 0)
    work = logits
    top_e, top_l = [], []
    for _ in range(TOP_K):
        mx = jnp.max(work, axis=0, keepdims=True)
        idx = jnp.min(jnp.where(work == mx, row, N_EXPERTS), axis=0, keepdims=True)
        top_e.append(idx)
        top_l.append(mx)
        work = jnp.where(row == idx, -jnp.inf, work)
    exps = [jnp.exp(v - top_l[0]) for v in top_l]
    denom = exps[0] + exps[1] + exps[2] + exps[3]

    onehots = [row == e for e in top_e]
    multi = jnp.zeros((N_EXPERTS, tm), F32)
    for oh in onehots:
        multi = multi + jnp.where(oh, 1.0, 0.0)
    r_i = lax.broadcasted_iota(jnp.int32, (tm, tm), 0)
    c_i = lax.broadcasted_iota(jnp.int32, (tm, tm), 1)
    earlier = jnp.where(r_i < c_i, 1.0, 0.0).astype(BF16)
    pos = run_sc[...] + jnp.dot(multi.astype(BF16), earlier, preferred_element_type=F32)
    run_sc[...] = run_sc[...] + jnp.sum(multi, axis=1, keepdims=True)
    cnt_ref[...] = run_sc[...]

    for k in range(TOP_K):
        tope_ref[k:k + 1, :] = top_e[k]
        topw_ref[k:k + 1, :] = exps[k] / denom
        rank_ref[k:k + 1, :] = jnp.sum(jnp.where(onehots[k], pos, 0.0), axis=0,
                                       keepdims=True).astype(jnp.int32)

    @pl.when(i >= 1)
    def _():
        for cp in h2_copies(i - 1, 1 - slot):
            cp.wait()

    @pl.when(i == pl.num_programs(0) - 1)
    def _():
        for cp in h2_copies(i, slot):
            cp.wait()


def _merge(proj, yb, x2, kvmem, ws, bsp, g_q_mem, wa, wb, wc, wo, g_ffn, w_router, b_router, S):
    T = x2.shape[0]
    tm = MERGE_TM
    per_b = S // tm
    full = lambda shape: pl.BlockSpec(shape, lambda i: (0,) * len(shape))
    in_specs = [
        pl.BlockSpec((tm, SPLIT), lambda i: (i, 0)),
        pl.BlockSpec((tm, SPLIT), lambda i: (i, 1)),
        pl.BlockSpec((tm, SPLIT), lambda i: (i, 0)),
        pl.BlockSpec((tm, SPLIT), lambda i: (i, 5)),
        pl.BlockSpec((tm, D_MODEL), lambda i: (i, 3)),
        pl.BlockSpec((tm, D_MODEL), lambda i: (i, 4)),
        pl.BlockSpec((tm, D_MODEL), lambda i: (i, 5)),
        pl.BlockSpec((tm, D_MODEL), lambda i: (i, 0)),
        pl.BlockSpec((MEM_LEN, 2 * MEM_HEADS * MEM_HEAD_DIM), lambda i: (i // per_b, 0)),
        full((SGU_GROUPS, SGU_CHUNK, SGU_CHUNK)),
        full((SGU_CHUNK, SGU_WIDTH)),
        full((1, MEM_HEAD_DIM)),
        full((SGU_WIDTH, D_MODEL)),
        full((SGU_WIDTH, D_MODEL)),
        full((SGU_WIDTH, D_MODEL)),
        full((D_MODEL, D_MODEL)),
        full((1, D_MODEL)),
        full((N_EXPERTS, D_MODEL)),
        full((N_EXPERTS, D_MODEL)),
        full((N_EXPERTS, 1)),
    ]
    out_shape = (
        jax.ShapeDtypeStruct((T, D_MODEL), F32),
        jax.ShapeDtypeStruct((T, D_MODEL // 2 // LANES, LANES), jnp.uint32),
        jax.ShapeDtypeStruct((TOP_K, T), jnp.int32),
        jax.ShapeDtypeStruct((TOP_K, T), F32),
        jax.ShapeDtypeStruct((TOP_K, T), jnp.int32),
        jax.ShapeDtypeStruct((N_EXPERTS, 1), F32),
    )
    out_specs = (
        pl.BlockSpec((tm, D_MODEL), lambda i: (i, 0)),
        pl.BlockSpec(memory_space=pl.ANY),
        pl.BlockSpec((TOP_K, tm), lambda i: (0, i)),
        pl.BlockSpec((TOP_K, tm), lambda i: (0, i)),
        pl.BlockSpec((TOP_K, tm), lambda i: (0, i)),
        pl.BlockSpec((N_EXPERTS, 1), lambda i: (0, 0)),
    )
    wr_t = w_router.T
    wr_hi = wr_t.astype(BF16)
    wr_lo = (wr_t - wr_hi.astype(F32)).astype(BF16)
    return pl.pallas_call(
        _merge_kernel,
        out_shape=out_shape,
        grid=(T // tm,),
        in_specs=in_specs,
        out_specs=out_specs,
        scratch_shapes=[pltpu.VMEM((N_EXPERTS, 1), F32),
                        pltpu.VMEM((2, tm, D_MODEL // 2), jnp.uint32), pltpu.SemaphoreType.DMA((2,))],
        compiler_params=pltpu.CompilerParams(
            dimension_semantics=("arbitrary",), vmem_limit_bytes=VMEM_LIMIT),
        name="merge",
    )(proj, proj, yb, proj, proj, proj, proj, x2, kvmem, ws, bsp, g_q_mem,
      wa, wb, wc, wo, g_ffn, wr_hi, wr_lo, b_router.reshape(N_EXPERTS, 1))


def _dispatch_kernel(dest_ref, h_ref, xin_ref, xbuf_ref, sem):
    del xin_ref
    tm = h_ref.shape[0]

    def row_copy(r, d):
        return pltpu.make_async_copy(h_ref.at[r], xbuf_ref.at[d], sem)

    def start(r, carry):
        for k in range(TOP_K):
            row_copy(r, dest_ref[r * TOP_K + k]).start(priority=k % 2)
        return carry

    for r in range(tm):
        start(r, None)
    for _ in range(TOP_K):
        pltpu.make_async_copy(h_ref, xbuf_ref.at[pl.ds(0, tm)], sem).wait()


def _dispatch(dest_flat, h2, n_rows):
    T, slabs, lanes = h2.shape
    tm = ROUTE_TM
    xbuf0 = jnp.zeros((n_rows, slabs, lanes), h2.dtype)
    return pl.pallas_call(
        _dispatch_kernel,
        out_shape=jax.ShapeDtypeStruct((n_rows, slabs, lanes), h2.dtype),
        grid=(T // tm,),
        in_specs=[
            pl.BlockSpec((tm * TOP_K,), lambda i: (i,), memory_space=pltpu.SMEM),
            pl.BlockSpec((tm, slabs, lanes), lambda i: (i, 0, 0)),
            pl.BlockSpec(memory_space=pl.ANY),
        ],
        out_specs=pl.BlockSpec(memory_space=pl.ANY),
        scratch_shapes=[pltpu.SemaphoreType.DMA],
        input_output_aliases={2: 0},
        compiler_params=pltpu.CompilerParams(
            dimension_semantics=("arbitrary",), vmem_limit_bytes=VMEM_LIMIT),
        name="dispatch",
    )(dest_flat, h2, xbuf0)


FF_CHUNK = 512


def _expert_kernel(be_ref, nused_ref, x_hbm, w1_ref, b1_ref, w2_ref, b2_ref,
                   y_hbm, w1_sc, w2_sc, x_sc, x_sem, y_sc, y_sem):
    i = pl.program_id(0)
    bm = x_sc.shape[1]
    n_used = nused_ref[0]

    def x_copies(blk, slot):
        rows = pl.ds(pl.multiple_of(blk * bm, bm), bm)
        return [pltpu.make_async_copy(x_hbm.at[rows, c], x_sc.at[slot, :, c * LANES:(c + 1) * LANES],
                                      x_sem.at[slot]) for c in range(x_hbm.shape[1])]

    def y_copies(blk, slot):
        rows = pl.ds(pl.multiple_of(blk * bm, bm), bm)
        return [pltpu.make_async_copy(y_sc.at[slot, :, c * LANES:(c + 1) * LANES], y_hbm.at[rows, c],
                                      y_sem.at[slot]) for c in range(y_hbm.shape[1])]

    @pl.when(i < n_used)
    def _():
        slot = i % 2

        @pl.when(i == 0)
        def _():
            for cp in x_copies(0, 0):
                cp.start()

        @pl.when(i + 1 < n_used)
        def _():
            for cp in x_copies(i + 1, 1 - slot):
                cp.start()

        @pl.when(jnp.logical_or(i == 0, be_ref[i] != be_ref[jnp.maximum(i - 1, 0)]))
        def _():
            w1_sc[...] = w1_ref[0].astype(BF16)
            w2_sc[...] = w2_ref[0].astype(BF16)

        for cp in x_copies(i, slot):
            cp.wait()
        packed = x_sc[slot]
        lo = lax.bitcast_convert_type(packed << 16, F32)
        hi = lax.bitcast_convert_type(packed & jnp.uint32(0xFFFF0000), F32)
        xb = jnp.concatenate([lo, hi], axis=1).astype(BF16)
        y = b2_ref[0]
        for c in range(D_FF // FF_CHUNK):
            g_cols = slice(c * FF_CHUNK, (c + 1) * FF_CHUNK)
            l_cols = slice(D_FF + c * FF_CHUNK, D_FF + (c + 1) * FF_CHUNK)
            glu = jnp.dot(xb, w1_sc[:, g_cols], preferred_element_type=F32) + b1_ref[0, :, g_cols]
            lin = jnp.dot(xb, w1_sc[:, l_cols], preferred_element_type=F32) + b1_ref[0, :, l_cols]
            glu = jnp.minimum(glu, SWIGLU_LIMIT)
            lin = jnp.clip(lin, -SWIGLU_LIMIT, SWIGLU_LIMIT)
            act = glu * jax.nn.sigmoid(SWIGLU_ALPHA * glu) * (lin + 1.0)
            y = y + jnp.dot(act.astype(BF16), w2_sc[g_cols, :], preferred_element_type=F32)
        y_sc[slot] = y
        for cp in y_copies(i, slot):
            cp.start()

        @pl.when(i >= 1)
        def _():
            for cp in y_copies(i - 1, 1 - slot):
                cp.wait()

        @pl.when(i == n_used - 1)
        def _():
            for cp in y_copies(i, slot):
                cp.wait()

    @pl.when(i >= n_used)
    def _():
        y_sc[0] = jnp.zeros(y_sc.shape[1:], F32)
        for cp in y_copies(i, 0):
            cp.start()
        for cp in y_copies(i, 0):
            cp.wait()


def _experts(block_e, n_used, xbuf, w1, b1, w2, b2):
    n_rows, slabs, lanes = xbuf.shape
    bm = EXPERT_BM
    grid_spec = pltpu.PrefetchScalarGridSpec(
        num_scalar_prefetch=2,
        grid=(n_rows // bm,),
        in_specs=[
            pl.BlockSpec(memory_space=pl.ANY),
            pl.BlockSpec((1, D_MODEL, 2 * D_FF), lambda i, be, nu: (be[i], 0, 0)),
            pl.BlockSpec((1, 1, 2 * D_FF), lambda i, be, nu: (be[i], 0, 0)),
            pl.BlockSpec((1, D_FF, D_MODEL), lambda i, be, nu: (be[i], 0, 0)),
            pl.BlockSpec((1, 1, D_MODEL), lambda i, be, nu: (be[i], 0, 0)),
        ],
        out_specs=pl.BlockSpec(memory_space=pl.ANY),
        scratch_shapes=[pltpu.VMEM((D_MODEL, 2 * D_FF), BF16), pltpu.VMEM((D_FF, D_MODEL), BF16),
                        pltpu.VMEM((2, bm, slabs * lanes), jnp.uint32), pltpu.SemaphoreType.DMA((2,)),
                        pltpu.VMEM((2, bm, D_MODEL), F32), pltpu.SemaphoreType.DMA((2,))],
    )
    return pl.pallas_call(
        _expert_kernel,
        out_shape=jax.ShapeDtypeStruct((n_rows, D_MODEL // LANES, LANES), F32),
        grid_spec=grid_spec,
        compiler_params=pltpu.CompilerParams(
            dimension_semantics=("arbitrary",), vmem_limit_bytes=VMEM_LIMIT),
        name="experts",
    )(block_e, n_used, xbuf, w1, b1, w2, b2)


def _combine_kernel(dest_ref, w_ref, x1_hbm, ybuf_hbm, out_hbm, g_sc, x1_sc, res_sc, g_sem, x_sem, o_sem):
    i = pl.program_id(0)
    n_steps = pl.num_programs(0)
    tm = g_sc.shape[1]
    chunks = D_MODEL // LANES
    slot = i % 2

    def x1_copies(step, s):
        rows = pl.ds(pl.multiple_of(step * tm, tm), tm)
        return [pltpu.make_async_copy(x1_hbm.at[rows, c * LANES:(c + 1) * LANES], x1_sc.at[s, :, c, :],
                                      x_sem.at[s]) for c in range(chunks)]

    def out_copies(step, s):
        rows = pl.ds(pl.multiple_of(step * tm, tm), tm)
        return [pltpu.make_async_copy(res_sc.at[s, :, c, :], out_hbm.at[rows, c * LANES:(c + 1) * LANES],
                                      o_sem.at[s]) for c in range(chunks)]

    @pl.when(i == 0)
    def _():
        for cp in x1_copies(0, 0):
            cp.start()

    @pl.when(i + 1 < n_steps)
    def _():
        for cp in x1_copies(i + 1, 1 - slot):
            cp.start()

    for r in range(tm):
        for k in range(TOP_K):
            pltpu.make_async_copy(ybuf_hbm.at[dest_ref[r * TOP_K + k]], g_sc.at[k, r],
                                  g_sem).start(priority=k % 2)
    for k in range(TOP_K):
        pltpu.make_async_copy(ybuf_hbm.at[pl.ds(0, tm)], g_sc.at[k], g_sem).wait()
    for cp in x1_copies(i, slot):
        cp.wait()

    for r in range(tm):
        acc = x1_sc[slot, r]
        for k in range(TOP_K):
            acc = acc + w_ref[r * TOP_K + k] * g_sc[k, r]
        res_sc[slot, r] = acc
    for cp in out_copies(i, slot):
        cp.start()

    @pl.when(i >= 1)
    def _():
        for cp in out_copies(i - 1, 1 - slot):
            cp.wait()

    @pl.when(i == n_steps - 1)
    def _():
        for cp in out_copies(i, slot):
            cp.wait()


def _combine(dest_flat, x1, w_flat, ybuf):
    T = x1.shape[0]
    tm = ROUTE_TM
    tile = (D_MODEL // LANES, LANES)
    return pl.pallas_call(
        _combine_kernel,
        out_shape=jax.ShapeDtypeStruct((T, D_MODEL), F32),
        grid=(T // tm,),
        in_specs=[
            pl.BlockSpec((tm * TOP_K,), lambda i: (i,), memory_space=pltpu.SMEM),
            pl.BlockSpec((tm * TOP_K,), lambda i: (i,), memory_space=pltpu.SMEM),
            pl.BlockSpec(memory_space=pl.ANY),
            pl.BlockSpec(memory_space=pl.ANY),
        ],
        out_specs=pl.BlockSpec(memory_space=pl.ANY),
        scratch_shapes=[pltpu.VMEM((TOP_K, tm) + tile, F32), pltpu.VMEM((2, tm) + tile, F32),
                        pltpu.VMEM((2, tm) + tile, F32), pltpu.SemaphoreType.DMA,
                        pltpu.SemaphoreType.DMA((2,)), pltpu.SemaphoreType.DMA((2,))],
        compiler_params=pltpu.CompilerParams(
            dimension_semantics=("arbitrary",), vmem_limit_bytes=VMEM_LIMIT),
        name="combine",
    )(dest_flat, w_flat, x1, ybuf)


def _layer(x2, mem2, B, S, g_mix, w_in, g_sgu_v, w_spatial, b_spatial, g_q_moba, g_k_moba,
           g_mem, w_mem_kv, g_q_mem, g_k_mem, w_br_sgu, w_br_moba, w_br_mem, w_out,
           g_ffn, w_router, b_router, w_gate_up, b_gate_up, w_down, b_down):
    T = B * S
    row = lambda v: v.reshape(1, -1)
    proj = _inproj(x2, row(g_mix), w_in.astype(BF16), row(g_sgu_v))
    yb = _moba(proj, g_q_moba, g_k_moba, B, S)
    kvmem = _memkv(mem2, row(g_mem), w_mem_kv.astype(BF16), row(g_k_mem), B)

    causal = jnp.tril(jnp.ones((SGU_CHUNK, SGU_CHUNK), dtype=bool))
    ws = jnp.where(causal[None], w_spatial, 0.0).astype(BF16)
    bsp = jnp.repeat(b_spatial.T, LANES, axis=1)
    x1, h2, top_e, top_w, rank, counts = _merge(
        proj, yb, x2, kvmem, ws, bsp, row(g_q_mem), w_br_sgu.astype(BF16),
        w_br_moba.astype(BF16), w_br_mem.astype(BF16), w_out.astype(BF16), row(g_ffn),
        w_router, b_router, S)

    bm = EXPERT_BM
    n_blocks = -(-(T * TOP_K + N_EXPERTS * (bm - 1)) // bm)
    cnt = counts.reshape(N_EXPERTS).astype(jnp.int32)
    padded = (cnt + bm - 1) // bm * bm
    pad_ends = jnp.cumsum(padded)
    pad_starts = pad_ends - padded
    e_ids = jnp.arange(N_EXPERTS, dtype=jnp.int32)[:, None, None]
    start_of = jnp.sum(jnp.where(top_e[None] == e_ids, pad_starts[:, None, None], 0), axis=0)
    dest = (start_of + rank).T.reshape(T * TOP_K)
    top_w = top_w.T.reshape(T * TOP_K)
    block_row0 = jnp.arange(n_blocks, dtype=jnp.int32) * bm
    block_e = jnp.minimum(
        jnp.sum((pad_ends[None, :] <= block_row0[:, None]).astype(jnp.int32), axis=1), N_EXPERTS - 1)
    n_used = (pad_ends[-1:] // bm).astype(jnp.int32)

    xbuf = _dispatch(dest, h2, n_blocks * bm)
    ybuf = _experts(block_e, n_used, xbuf, w_gate_up, b_gate_up.reshape(N_EXPERTS, 1, 2 * D_FF),
                    w_down, b_down.reshape(N_EXPERTS, 1, D_MODEL))
    return _combine(dest, x1, top_w, ybuf)


def kernel(x, mem, g_mix, w_in, g_sgu_v, w_spatial, b_spatial, g_q_moba, g_k_moba, g_mem, w_mem_kv, g_q_mem, g_k_mem, w_br_sgu, w_br_moba, w_br_mem, w_out, g_ffn, w_router, b_router, w_gate_up, b_gate_up, w_down, b_down):
    B, S, D = x.shape
    x2 = x.reshape(B * S, D)
    mem2 = mem.reshape(B * MEM_LEN, D)
    params = (g_mix, w_in, g_sgu_v, w_spatial, b_spatial, g_q_moba, g_k_moba, g_mem, w_mem_kv,
              g_q_mem, g_k_mem, w_br_sgu, w_br_moba, w_br_mem, w_out, g_ffn, w_router, b_router,
              w_gate_up, b_gate_up, w_down, b_down)
    for l in range(g_mix.shape[0]):
        x2 = _layer(x2, mem2, B, S, *(p[l] for p in params))
    return x2.reshape(B, S, D)
```

```python
import functools

import numpy as np
import jax
import jax.numpy as jnp
from jax import lax
from jax.experimental import pallas as pl
from jax.experimental.pallas import tpu as pltpu

F32 = jnp.float32
BF16 = jnp.bfloat16

D_MODEL = 1024
MEM_LEN = 256
SGU_GROUPS = 4
SGU_CHUNK = 128
SGU_WIDTH = 512
MOBA_HEADS = 8
MOBA_HEAD_DIM = 64
MOBA_BLOCK = 256
MOBA_TOPK = 3
MEM_HEADS = 4
MEM_HEAD_DIM = 128
SPLIT = 512
IN_WIDTH = 6144
N_SPLITS = IN_WIDTH // SPLIT
N_EXPERTS = 32
TOP_K = 4
D_FF = 1024
SWIGLU_ALPHA = 1.702
SWIGLU_LIMIT = 7.0
EPS = 1e-6
NEG_INF = -1e30
LANES = 128

VMEM_LIMIT = 56 * 1024 * 1024

INPROJ_TM = 256
MERGE_TM = 512
ROUTE_TM = 256
EXPERT_BM = 512


def _gelu(x):
    return 0.5 * x * (1.0 + lax.erf(x * np.float32(np.sqrt(0.5))))


def _rms(x, g):
    return x * lax.rsqrt(jnp.mean(x * x, axis=-1, keepdims=True) + EPS) * g


def _inproj_kernel(x_ref, gmix_ref, w_ref, gsgu_ref, o_ref):
    h = _rms(x_ref[...], gmix_ref[...]).astype(BF16)
    for c in range(N_SPLITS):
        sl = slice(c * SPLIT, (c + 1) * SPLIT)
        acc = jnp.dot(h, w_ref[:, sl], preferred_element_type=F32)
        if c == 0:
            val = _gelu(acc)
        elif c == 1:
            val = _rms(_gelu(acc), gsgu_ref[...])
        elif c >= 6:
            val = jax.nn.sigmoid(acc)
        else:
            val = acc
        o_ref[:, sl] = val.astype(BF16)


def _inproj(x2, g_mix, w_in, g_sgu_v):
    T = x2.shape[0]
    tm = INPROJ_TM
    return pl.pallas_call(
        _inproj_kernel,
        out_shape=jax.ShapeDtypeStruct((T, IN_WIDTH), BF16),
        grid=(T // tm,),
        in_specs=[
            pl.BlockSpec((tm, D_MODEL), lambda i: (i, 0)),
            pl.BlockSpec((1, D_MODEL), lambda i: (0, 0)),
            pl.BlockSpec((D_MODEL, IN_WIDTH), lambda i: (0, 0)),
            pl.BlockSpec((1, SGU_WIDTH), lambda i: (0, 0)),
        ],
        out_specs=pl.BlockSpec((tm, IN_WIDTH), lambda i: (i, 0)),
        compiler_params=pltpu.CompilerParams(
            dimension_semantics=("arbitrary",), vmem_limit_bytes=VMEM_LIMIT),
        name="inproj",
    )(x2, g_mix, w_in, g_sgu_v)


def _pair_norm(x, g, head0):
    sq = x * x
    s0 = jnp.sum(jnp.where(head0, sq, 0.0), axis=-1, keepdims=True)
    s1 = jnp.sum(jnp.where(head0, 0.0, sq), axis=-1, keepdims=True)
    inv = jnp.where(head0, lax.rsqrt(s0 * (1.0 / MOBA_HEAD_DIM) + EPS),
                    lax.rsqrt(s1 * (1.0 / MOBA_HEAD_DIM) + EPS))
    return x * inv * g


VT_ROWS = MOBA_HEAD_DIM + 16
LOG2E = float(np.log2(np.e))


def _moba_kernel(slopes_ref, q_ref, k_ref, v_ref, gq_ref, gk_ref, bias_ref, o_ref,
                 kn_sc, vt_sc, kmean_sc, mrow_sc, qh_sc, m_sc, acc_sc, t_sc, cmax_sc, *, nb):
    qi = pl.program_id(1)
    bs = MOBA_BLOCK
    pairs = MOBA_HEADS // 2
    head0 = lax.broadcasted_iota(jnp.int32, (1, LANES), 1) < MOBA_HEAD_DIM
    nt = (((1,), (1,)), ((), ()))

    @pl.when(qi == 0)
    def _prepare_kv():
        ones_rows = jnp.where(
            lax.broadcasted_iota(jnp.int32, (VT_ROWS - MOBA_HEAD_DIM, bs), 0) == 0, 1.0, 0.0).astype(BF16)

        def body(j, carry):
            rows = pl.ds(pl.multiple_of(j * bs, bs), bs)
            means = []
            for g in range(pairs):
                cols = slice(g * LANES, (g + 1) * LANES)
                kn = _pair_norm(k_ref[rows, cols].astype(F32), gk_ref[...], head0)
                kn_sc[j, :, cols] = kn.astype(BF16)
                means.append(jnp.mean(kn, axis=0, keepdims=True))
                vt = v_ref[rows, cols].astype(F32).T.astype(BF16)
                for hh in range(2):
                    h = 2 * g + hh
                    vt_sc[j, h, :MOBA_HEAD_DIM, :] = vt[hh * MOBA_HEAD_DIM:(hh + 1) * MOBA_HEAD_DIM]
                    vt_sc[j, h, MOBA_HEAD_DIM:, :] = ones_rows
            kmean_sc[pl.ds(j, 1), :] = jnp.concatenate(means, axis=1)
            return carry
        lax.fori_loop(0, nb, body, 0)

    blk = lax.broadcasted_iota(jnp.int32, (nb, 1), 0)
    eligible = blk < qi
    dist = ((qi - blk) * bs).astype(F32)
    for g in range(pairs):
        cols = slice(g * LANES, (g + 1) * LANES)
        qn = _pair_norm(q_ref[:, cols].astype(F32), gq_ref[...], head0)
        qpair = jnp.concatenate([jnp.where(head0, qn, 0.0), jnp.where(head0, 0.0, qn)], axis=0)
        qh_sc[g] = (qpair * (LOG2E * MOBA_HEAD_DIM ** -0.5)).astype(BF16)
        q_hi = qpair.astype(BF16)
        q_lo = (qpair - q_hi.astype(F32)).astype(BF16)
        km = kmean_sc[:, cols]
        km_hi = km.astype(BF16)
        km_lo = (km - km_hi.astype(F32)).astype(BF16)
        gate = (lax.dot_general(km_hi, q_hi, nt, preferred_element_type=F32)
                + lax.dot_general(km_hi, q_lo, nt, preferred_element_type=F32)
                + lax.dot_general(km_lo, q_hi, nt, preferred_element_type=F32))
        gate = jnp.where(eligible, gate, -jnp.inf)
        rank = jnp.zeros((nb, 2 * bs), F32)
        for i in range(nb - 1):
            gi = gate[i:i + 1, :]
            beats = jnp.logical_or(gi > gate, jnp.logical_and(gi == gate, blk > i))
            rank = rank + jnp.where(beats, 1.0, 0.0)
        sel = jnp.logical_and(eligible, rank < float(MOBA_TOPK))
        for hh in range(2):
            h = 2 * g + hh
            mrow_sc[:, h * bs:(h + 1) * bs] = jnp.where(
                blk == qi, 0.0,
                jnp.where(sel[:, hh * bs:(hh + 1) * bs], (-LOG2E * slopes_ref[h]) * dist, NEG_INF))
    m_sc[...] = jnp.full(m_sc.shape, NEG_INF, F32)
    acc_sc[...] = jnp.zeros(acc_sc.shape, F32)

    def score(j, own, slot):
        scores = [lax.dot_general(kn_sc[j, :, g * LANES:(g + 1) * LANES], qh_sc[g], nt,
                                  preferred_element_type=F32) for g in range(pairs)]
        t = jnp.concatenate(scores, axis=1) + bias_ref[1 if own else 0]
        t_sc[slot] = t
        cmax_sc[slot] = jnp.max(t, axis=0, keepdims=True)

    def accumulate(j, slot):
        mrow = mrow_sc[pl.ds(j, 1), :]
        m_old = m_sc[...]
        m_new = jnp.maximum(m_old, cmax_sc[slot] + mrow)
        shift = m_new - mrow
        alpha = jnp.exp2(m_old - m_new)
        m_sc[...] = m_new
        for h in range(MOBA_HEADS):
            cols = slice(h * bs, (h + 1) * bs)
            p = jnp.exp2(t_sc[slot, :, cols] - shift[:, cols]).astype(BF16)
            pv = jnp.dot(vt_sc[j, h], p, preferred_element_type=F32)
            acc_sc[h] = alpha[:, cols] * acc_sc[h] + pv

    n_past = qi
    score(qi, True, 0)

    def two_blocks(p, carry):
        score(2 * p, False, 1)
        accumulate(jnp.where(p == 0, qi, 2 * p - 1), 0)
        score(2 * p + 1, False, 0)
        accumulate(2 * p, 1)
        return carry
    lax.fori_loop(0, n_past // 2, two_blocks, 0)

    @pl.when(n_past % 2 == 0)
    def _():
        accumulate(jnp.where(n_past == 0, qi, n_past - 1), 0)

    @pl.when(n_past % 2 == 1)
    def _():
        score(n_past - 1, False, 1)
        accumulate(jnp.where(n_past == 1, qi, n_past - 2), 0)
        accumulate(n_past - 1, 1)

    for g in range(pairs):
        halves = []
        for hh in range(2):
            acc = acc_sc[2 * g + hh]
            halves.append(acc[:MOBA_HEAD_DIM] / acc[MOBA_HEAD_DIM:MOBA_HEAD_DIM + 1])
        out_t = jnp.concatenate(halves, axis=0)
        o_ref[:, g * LANES:(g + 1) * LANES] = out_t.T.astype(BF16)


def _moba_bias_tiles():
    slopes = 2.0 ** (-8.0 * np.arange(1, MOBA_HEADS + 1, dtype=np.float32) / MOBA_HEADS)
    kk = np.arange(MOBA_BLOCK)[:, None]
    qq = np.arange(MOBA_BLOCK)[None, :]
    rel = (qq - kk).astype(np.float64)
    past = -slopes.astype(np.float64)[:, None, None] * rel[None] * LOG2E
    own = np.where((kk <= qq)[None], past, NEG_INF)
    side_by_side = lambda a: a.transpose(1, 0, 2).reshape(MOBA_BLOCK, MOBA_HEADS * MOBA_BLOCK)
    return slopes.astype(np.float32), np.stack([side_by_side(past), side_by_side(own)]).astype(np.float32)


def _moba(proj, g_q, g_k, B, S):
    nb = S // MOBA_BLOCK
    width = MOBA_HEADS * MOBA_HEAD_DIM
    slopes, bias = _moba_bias_tiles()
    gq2 = jnp.tile(g_q.reshape(1, MOBA_HEAD_DIM), (1, 2))
    gk2 = jnp.tile(g_k.reshape(1, MOBA_HEAD_DIM), (1, 2))
    full = lambda shape: pl.BlockSpec(shape, lambda b, q, s: (0,) * len(shape))
    grid_spec = pltpu.PrefetchScalarGridSpec(
        num_scalar_prefetch=1,
        grid=(B, nb),
        in_specs=[
            pl.BlockSpec((MOBA_BLOCK, width), lambda b, q, s: (b * nb + q, 2)),
            pl.BlockSpec((S, width), lambda b, q, s: (b, 3)),
            pl.BlockSpec((S, width), lambda b, q, s: (b, 4)),
            full((1, LANES)),
            full((1, LANES)),
            full((2, MOBA_BLOCK, MOBA_HEADS * MOBA_BLOCK)),
        ],
        out_specs=pl.BlockSpec((MOBA_BLOCK, width), lambda b, q, s: (b * nb + q, 0)),
        scratch_shapes=[
            pltpu.VMEM((nb, MOBA_BLOCK, width), BF16),
            pltpu.VMEM((nb, MOBA_HEADS, VT_ROWS, MOBA_BLOCK), BF16),
            pltpu.VMEM((nb, width), F32),
            pltpu.VMEM((nb, MOBA_HEADS * MOBA_BLOCK), F32),
            pltpu.VMEM((MOBA_HEADS // 2, 2 * MOBA_BLOCK, LANES), BF16),
            pltpu.VMEM((1, MOBA_HEADS * MOBA_BLOCK), F32),
            pltpu.VMEM((MOBA_HEADS, VT_ROWS, MOBA_BLOCK), F32),
            pltpu.VMEM((2, MOBA_BLOCK, MOBA_HEADS * MOBA_BLOCK), F32),
            pltpu.VMEM((2, 1, MOBA_HEADS * MOBA_BLOCK), F32),
        ],
    )
    return pl.pallas_call(
        functools.partial(_moba_kernel, nb=nb),
        out_shape=jax.ShapeDtypeStruct((B * S, width), BF16),
        grid_spec=grid_spec,
        compiler_params=pltpu.CompilerParams(
            dimension_semantics=("arbitrary", "arbitrary"), vmem_limit_bytes=VMEM_LIMIT),
        name="moba",
    )(jnp.asarray(slopes), proj, proj, proj, gq2, gk2, jnp.asarray(bias))


def _memkv_kernel(mem_ref, gmem_ref, w_ref, gk_ref, o_ref):
    mn = _rms(mem_ref[...], gmem_ref[...]).astype(BF16)
    kv = jnp.dot(mn, w_ref[...], preferred_element_type=F32)
    width = MEM_HEADS * MEM_HEAD_DIM
    for h in range(MEM_HEADS):
        sl = slice(h * MEM_HEAD_DIM, (h + 1) * MEM_HEAD_DIM)
        o_ref[:, sl] = _rms(kv[:, sl], gk_ref[...]).astype(BF16)
    o_ref[:, width:] = kv[:, width:].astype(BF16)


def _memkv(mem2, g_mem, w_mem_kv, g_k_mem, B):
    width = MEM_HEADS * MEM_HEAD_DIM
    return pl.pallas_call(
        _memkv_kernel,
        out_shape=jax.ShapeDtypeStruct((B * MEM_LEN, 2 * width), BF16),
        grid=(B,),
        in_specs=[
            pl.BlockSpec((MEM_LEN, D_MODEL), lambda b: (b, 0)),
            pl.BlockSpec((1, D_MODEL), lambda b: (0, 0)),
            pl.BlockSpec((D_MODEL, 2 * width), lambda b: (0, 0)),
            pl.BlockSpec((1, MEM_HEAD_DIM), lambda b: (0, 0)),
        ],
        out_specs=pl.BlockSpec((MEM_LEN, 2 * width), lambda b: (b, 0)),
        compiler_params=pltpu.CompilerParams(
            dimension_semantics=("arbitrary",), vmem_limit_bytes=VMEM_LIMIT),
        name="memkv",
    )(mem2, g_mem, w_mem_kv, g_k_mem)


def _merge_kernel(gu_ref, vn_ref, yb_ref, qm_ref, ga_ref, gb_ref, gc_ref, x_ref, kv_ref,
                  ws_ref, bsp_ref, gqm_ref, wa_ref, wb_ref, wc_ref, wo_ref, gffn_ref,
                  wrh_ref, wrl_ref, br_ref,
                  x1_ref, h2_hbm, tope_ref, topw_ref, rank_ref, cnt_ref, run_sc, hp_sc, hp_sem):
    i = pl.program_id(0)
    tm = x_ref.shape[0]

    @pl.when(i == 0)
    def _():
        run_sc[...] = jnp.zeros_like(run_sc)

    ya_chunks = []
    for c in range(tm // SGU_CHUNK):
        rows = slice(c * SGU_CHUNK, (c + 1) * SGU_CHUNK)
        groups = []
        for g in range(SGU_GROUPS):
            cols = slice(g * LANES, (g + 1) * LANES)
            groups.append(jnp.dot(ws_ref[g], vn_ref[rows, cols], preferred_element_type=F32))
        mixed = jnp.concatenate(groups, axis=1) + bsp_ref[...]
        ya_chunks.append(gu_ref[rows, :].astype(F32) * mixed)
    y_a = jnp.concatenate(ya_chunks, axis=0).astype(BF16)

    width = MEM_HEADS * MEM_HEAD_DIM
    yc_heads = []
    for h in range(MEM_HEADS):
        cols = slice(h * MEM_HEAD_DIM, (h + 1) * MEM_HEAD_DIM)
        qn = _rms(qm_ref[:, cols].astype(F32), gqm_ref[...]).astype(BF16)
        s = lax.dot_general(qn, kv_ref[:, cols], (((1,), (1,)), ((), ())),
                            preferred_element_type=F32) * (MEM_HEAD_DIM ** -0.5)
        s = s - jnp.max(s, axis=-1, keepdims=True)
        p = jnp.exp(s)
        p = p / jnp.sum(p, axis=-1, keepdims=True)
        vcols = slice(width + h * MEM_HEAD_DIM, width + (h + 1) * MEM_HEAD_DIM)
        yc_heads.append(jnp.dot(p.astype(BF16), kv_ref[:, vcols], preferred_element_type=F32))
    y_c = jnp.concatenate(yc_heads, axis=1).astype(BF16)

    merged = ga_ref[...].astype(F32) * jnp.dot(y_a, wa_ref[...], preferred_element_type=F32)
    merged = merged + gb_ref[...].astype(F32) * jnp.dot(yb_ref[...], wb_ref[...],
                                                       preferred_element_type=F32)
    merged = merged + gc_ref[...].astype(F32) * jnp.dot(y_c, wc_ref[...],
                                                       preferred_element_type=F32)
    x1 = x_ref[...] + jnp.dot(merged.astype(BF16), wo_ref[...], preferred_element_type=F32)
    x1_ref[...] = x1
    h2 = _rms(x1, gffn_ref[...])
    h2_hi = h2.astype(BF16)
    h2_hi32 = h2_hi.astype(F32)
    h2_lo = (h2 - h2_hi32).astype(BF16)
    bits = lax.bitcast_convert_type(h2_hi32, jnp.uint32)
    half = D_MODEL // 2
    slot = i % 2
    hp_sc[slot] = (bits[:, :half] >> 16) | (bits[:, half:] & jnp.uint32(0xFFFF0000))

    def h2_copies(step, s):
        rows = pl.ds(pl.multiple_of(step * tm, tm), tm)
        return [pltpu.make_async_copy(hp_sc.at[s, :, c * LANES:(c + 1) * LANES], h2_hbm.at[rows, c],
                                      hp_sem.at[s]) for c in range(half // LANES)]

    for cp in h2_copies(i, slot):
        cp.start()

    nt = (((1,), (1,)), ((), ()))
    logits = (lax.dot_general(wrh_ref[...], h2_hi, nt, preferred_element_type=F32)
              + lax.dot_general(wrh_ref[...], h2_lo, nt, preferred_element_type=F32)
              + lax.dot_general(wrl_ref[...], h2_hi, nt, preferred_element_type=F32)
              + br_ref[...])
    row = lax.broadcasted_iota(jnp.int32, (N_EXPERTS, tm), 0)
    work = logits
    top_e, top_l = [], []
    for _ in range(TOP_K):
        mx = jnp.max(work, axis=0, keepdims=True)
        idx = jnp.min(jnp.where(work == mx, row, N_EXPERTS), axis=0, keepdims=True)
        top_e.append(idx)
        top_l.append(mx)
        work = jnp.where(row == idx, -jnp.inf, work)
    exps = [jnp.exp(v - top_l[0]) for v in top_l]
    denom = exps[0] + exps[1] + exps[2] + exps[3]

    onehots = [row == e for e in top_e]
    multi = jnp.zeros((N_EXPERTS, tm), F32)
    for oh in onehots:
        multi = multi + jnp.where(oh, 1.0, 0.0)
    r_i = lax.broadcasted_iota(jnp.int32, (tm, tm), 0)
    c_i = lax.broadcasted_iota(jnp.int32, (tm, tm), 1)
    earlier = jnp.where(r_i < c_i, 1.0, 0.0).astype(BF16)
    pos = run_sc[...] + jnp.dot(multi.astype(BF16), earlier, preferred_element_type=F32)
    run_sc[...] = run_sc[...] + jnp.sum(multi, axis=1, keepdims=True)
    cnt_ref[...] = run_sc[...]

    for k in range(TOP_K):
        tope_ref[k:k + 1, :] = top_e[k]
        topw_ref[k:k + 1, :] = exps[k] / denom
        rank_ref[k:k + 1, :] = jnp.sum(jnp.where(onehots[k], pos, 0.0), axis=0,
                                       keepdims=True).astype(jnp.int32)

    @pl.when(i >= 1)
    def _():
        for cp in h2_copies(i - 1, 1 - slot):
            cp.wait()

    @pl.when(i == pl.num_programs(0) - 1)
    def _():
        for cp in h2_copies(i, slot):
            cp.wait()


def _merge(proj, yb, x2, kvmem, ws, bsp, g_q_mem, wa, wb, wc, wo, g_ffn, w_router, b_router, S):
    T = x2.shape[0]
    tm = MERGE_TM
    per_b = S // tm
    full = lambda shape: pl.BlockSpec(shape, lambda i: (0,) * len(shape))
    in_specs = [
        pl.BlockSpec((tm, SPLIT), lambda i: (i, 0)),
        pl.BlockSpec((tm, SPLIT), lambda i: (i, 1)),
        pl.BlockSpec((tm, SPLIT), lambda i: (i, 0)),
        pl.BlockSpec((tm, SPLIT), lambda i: (i, 5)),
        pl.BlockSpec((tm, D_MODEL), lambda i: (i, 3)),
        pl.BlockSpec((tm, D_MODEL), lambda i: (i, 4)),
        pl.BlockSpec((tm, D_MODEL), lambda i: (i, 5)),
        pl.BlockSpec((tm, D_MODEL), lambda i: (i, 0)),
        pl.BlockSpec((MEM_LEN, 2 * MEM_HEADS * MEM_HEAD_DIM), lambda i: (i // per_b, 0)),
        full((SGU_GROUPS, SGU_CHUNK, SGU_CHUNK)),
        full((SGU_CHUNK, SGU_WIDTH)),
        full((1, MEM_HEAD_DIM)),
        full((SGU_WIDTH, D_MODEL)),
        full((SGU_WIDTH, D_MODEL)),
        full((SGU_WIDTH, D_MODEL)),
        full((D_MODEL, D_MODEL)),
        full((1, D_MODEL)),
        full((N_EXPERTS, D_MODEL)),
        full((N_EXPERTS, D_MODEL)),
        full((N_EXPERTS, 1)),
    ]
    out_shape = (
        jax.ShapeDtypeStruct((T, D_MODEL), F32),
        jax.ShapeDtypeStruct((T, D_MODEL // 2 // LANES, LANES), jnp.uint32),
        jax.ShapeDtypeStruct((TOP_K, T), jnp.int32),
        jax.ShapeDtypeStruct((TOP_K, T), F32),
        jax.ShapeDtypeStruct((TOP_K, T), jnp.int32),
        jax.ShapeDtypeStruct((N_EXPERTS, 1), F32),
    )
    out_specs = (
        pl.BlockSpec((tm, D_MODEL), lambda i: (i, 0)),
        pl.BlockSpec(memory_space=pl.ANY),
        pl.BlockSpec((TOP_K, tm), lambda i: (0, i)),
        pl.BlockSpec((TOP_K, tm), lambda i: (0, i)),
        pl.BlockSpec((TOP_K, tm), lambda i: (0, i)),
        pl.BlockSpec((N_EXPERTS, 1), lambda i: (0, 0)),
    )
    wr_t = w_router.T
    wr_hi = wr_t.astype(BF16)
    wr_lo = (wr_t - wr_hi.astype(F32)).astype(BF16)
    return pl.pallas_call(
        _merge_kernel,
        out_shape=out_shape,
        grid=(T // tm,),
        in_specs=in_specs,
        out_specs=out_specs,
        scratch_shapes=[pltpu.VMEM((N_EXPERTS, 1), F32),
                        pltpu.VMEM((2, tm, D_MODEL // 2), jnp.uint32), pltpu.SemaphoreType.DMA((2,))],
        compiler_params=pltpu.CompilerParams(
            dimension_semantics=("arbitrary",), vmem_limit_bytes=VMEM_LIMIT),
        name="merge",
    )(proj, proj, yb, proj, proj, proj, proj, x2, kvmem, ws, bsp, g_q_mem,
      wa, wb, wc, wo, g_ffn, wr_hi, wr_lo, b_router.reshape(N_EXPERTS, 1))


def _dispatch_kernel(padlo_ref, padlen_ref, dest_ref, h_ref, xbuf_ref, sem, z_sc, z_sem):
    i = pl.program_id(0)
    tm = h_ref.shape[0]

    @pl.when(i == 0)
    def _zero_padding_rows():
        z_sc[...] = jnp.zeros(z_sc.shape, z_sc.dtype)
        bits = range(EXPERT_BM.bit_length() - 2, -1, -1)

        def pieces(e):
            off, length = padlo_ref[e], padlen_ref[e]
            for b in bits:
                take = ((length >> b) & 1) == 1
                yield take, pltpu.make_async_copy(z_sc.at[pl.ds(0, 1 << b)],
                                                  xbuf_ref.at[pl.ds(off, 1 << b)], z_sem)
                off = off + jnp.where(take, 1 << b, 0)

        for e in range(N_EXPERTS):
            for take, cp in pieces(e):
                pl.when(take)(cp.start)
        for e in range(N_EXPERTS):
            for take, cp in pieces(e):
                pl.when(take)(cp.wait)

        zrows = z_sc.shape[0]
        first = (padlo_ref[N_EXPERTS - 1] + padlen_ref[N_EXPERTS - 1]) // zrows
        def tail_copy(j):
            return pltpu.make_async_copy(z_sc, xbuf_ref.at[pl.ds(pl.multiple_of(j * zrows, zrows), zrows)], z_sem)
        lax.fori_loop(first, xbuf_ref.shape[0] // zrows, lambda j, c: (tail_copy(j).start(), c)[1], 0)
        lax.fori_loop(first, xbuf_ref.shape[0] // zrows, lambda j, c: (tail_copy(j).wait(), c)[1], 0)

    def row_copy(r, d):
        return pltpu.make_async_copy(h_ref.at[r], xbuf_ref.at[d], sem)

    def start(r, carry):
        for k in range(TOP_K):
            row_copy(r, dest_ref[r * TOP_K + k]).start(priority=k % 2)
        return carry

    for r in range(tm):
        start(r, None)
    for _ in range(TOP_K):
        pltpu.make_async_copy(h_ref, xbuf_ref.at[pl.ds(0, tm)], sem).wait()


def _dispatch(pad_lo, pad_len, dest_flat, h2, n_rows):
    T, slabs, lanes = h2.shape
    tm = ROUTE_TM
    grid_spec = pltpu.PrefetchScalarGridSpec(
        num_scalar_prefetch=2,
        grid=(T // tm,),
        in_specs=[
            pl.BlockSpec((tm * TOP_K,), lambda i, lo, ln: (i,), memory_space=pltpu.SMEM),
            pl.BlockSpec((tm, slabs, lanes), lambda i, lo, ln: (i, 0, 0)),
        ],
        out_specs=pl.BlockSpec(memory_space=pl.ANY),
        scratch_shapes=[pltpu.SemaphoreType.DMA, pltpu.VMEM((EXPERT_BM // 2, slabs, lanes), h2.dtype),
                        pltpu.SemaphoreType.DMA],
    )
    return pl.pallas_call(
        _dispatch_kernel,
        out_shape=jax.ShapeDtypeStruct((n_rows, slabs, lanes), h2.dtype),
        grid_spec=grid_spec,
        compiler_params=pltpu.CompilerParams(
            dimension_semantics=("arbitrary",), vmem_limit_bytes=VMEM_LIMIT),
        name="dispatch",
    )(pad_lo, pad_len, dest_flat, h2)


FF_CHUNK = 512


def _expert_kernel(be_ref, nused_ref, x_hbm, w1_ref, b1_ref, w2_ref, b2_ref,
                   y_hbm, w1_sc, w2_sc, x_sc, x_sem, y_sc, y_sem):
    i = pl.program_id(0)
    bm = x_sc.shape[1]
    n_used = nused_ref[0]

    def x_copies(blk, slot):
        rows = pl.ds(pl.multiple_of(blk * bm, bm), bm)
        return [pltpu.make_async_copy(x_hbm.at[rows, c], x_sc.at[slot, :, c * LANES:(c + 1) * LANES],
                                      x_sem.at[slot]) for c in range(x_hbm.shape[1])]

    def y_copies(blk, slot):
        rows = pl.ds(pl.multiple_of(blk * bm, bm), bm)
        return [pltpu.make_async_copy(y_sc.at[slot, :, c * LANES:(c + 1) * LANES], y_hbm.at[rows, c],
                                      y_sem.at[slot]) for c in range(y_hbm.shape[1])]

    @pl.when(i < n_used)
    def _():
        slot = i % 2

        @pl.when(i == 0)
        def _():
            for cp in x_copies(0, 0):
                cp.start()

        @pl.when(i + 1 < n_used)
        def _():
            for cp in x_copies(i + 1, 1 - slot):
                cp.start()

        @pl.when(jnp.logical_or(i == 0, be_ref[i] != be_ref[jnp.maximum(i - 1, 0)]))
        def _():
            w1_sc[...] = w1_ref[0].astype(BF16)
            w2_sc[...] = w2_ref[0].astype(BF16)

        for cp in x_copies(i, slot):
            cp.wait()
        packed = x_sc[slot]
        lo = lax.bitcast_convert_type(packed << 16, F32)
        hi = lax.bitcast_convert_type(packed & jnp.uint32(0xFFFF0000), F32)
        xb = jnp.concatenate([lo, hi], axis=1).astype(BF16)
        y = b2_ref[0]
        for c in range(D_FF // FF_CHUNK):
            g_cols = slice(c * FF_CHUNK, (c + 1) * FF_CHUNK)
            l_cols = slice(D_FF + c * FF_CHUNK, D_FF + (c + 1) * FF_CHUNK)
            glu = jnp.dot(xb, w1_sc[:, g_cols], preferred_element_type=F32) + b1_ref[0, :, g_cols]
            lin = jnp.dot(xb, w1_sc[:, l_cols], preferred_element_type=F32) + b1_ref[0, :, l_cols]
            glu = jnp.minimum(glu, SWIGLU_LIMIT)
            lin = jnp.clip(lin, -SWIGLU_LIMIT, SWIGLU_LIMIT)
            act = glu * jax.nn.sigmoid(SWIGLU_ALPHA * glu) * (lin + 1.0)
            y = y + jnp.dot(act.astype(BF16), w2_sc[g_cols, :], preferred_element_type=F32)
        y_sc[slot] = y
        for cp in y_copies(i, slot):
            cp.start()

        @pl.when(i >= 1)
        def _():
            for cp in y_copies(i - 1, 1 - slot):
                cp.wait()

        @pl.when(i == n_used - 1)
        def _():
            for cp in y_copies(i, slot):
                cp.wait()

    @pl.when(i >= n_used)
    def _():
        y_sc[0] = jnp.zeros(y_sc.shape[1:], F32)
        for cp in y_copies(i, 0):
            cp.start()
        for cp in y_copies(i, 0):
            cp.wait()


def _experts(block_e, n_used, xbuf, w1, b1, w2, b2):
    n_rows, slabs, lanes = xbuf.shape
    bm = EXPERT_BM
    grid_spec = pltpu.PrefetchScalarGridSpec(
        num_scalar_prefetch=2,
        grid=(n_rows // bm,),
        in_specs=[
            pl.BlockSpec(memory_space=pl.ANY),
            pl.BlockSpec((1, D_MODEL, 2 * D_FF), lambda i, be, nu: (be[i], 0, 0)),
            pl.BlockSpec((1, 1, 2 * D_FF), lambda i, be, nu: (be[i], 0, 0)),
            pl.BlockSpec((1, D_FF, D_MODEL), lambda i, be, nu: (be[i], 0, 0)),
            pl.BlockSpec((1, 1, D_MODEL), lambda i, be, nu: (be[i], 0, 0)),
        ],
        out_specs=pl.BlockSpec(memory_space=pl.ANY),
        scratch_shapes=[pltpu.VMEM((D_MODEL, 2 * D_FF), BF16), pltpu.VMEM((D_FF, D_MODEL), BF16),
                        pltpu.VMEM((2, bm, slabs * lanes), jnp.uint32), pltpu.SemaphoreType.DMA((2,)),
                        pltpu.VMEM((2, bm, D_MODEL), F32), pltpu.SemaphoreType.DMA((2,))],
    )
    return pl.pallas_call(
        _expert_kernel,
        out_shape=jax.ShapeDtypeStruct((n_rows, D_MODEL // LANES, LANES), F32),
        grid_spec=grid_spec,
        compiler_params=pltpu.CompilerParams(
            dimension_semantics=("arbitrary",), vmem_limit_bytes=VMEM_LIMIT),
        name="experts",
    )(block_e, n_used, xbuf, w1, b1, w2, b2)


def _combine_kernel(dest_ref, nxt_ref, w_ref, x1_hbm, ybuf_hbm, out_hbm, g_sc, x1_sc, res_sc,
                    g_sem, x_sem, o_sem):
    i = pl.program_id(0)
    n_steps = pl.num_programs(0)
    tm = g_sc.shape[2]
    chunks = D_MODEL // LANES
    slot = i % 2

    def gather(idx_ref, s):
        for r in range(tm):
            for k in range(TOP_K):
                pltpu.make_async_copy(ybuf_hbm.at[idx_ref[r * TOP_K + k]], g_sc.at[s, k, r],
                                      g_sem.at[s]).start(priority=k % 2)

    def x1_copies(step, s):
        rows = pl.ds(pl.multiple_of(step * tm, tm), tm)
        return [pltpu.make_async_copy(x1_hbm.at[rows, c * LANES:(c + 1) * LANES], x1_sc.at[s, :, c, :],
                                      x_sem.at[s]) for c in range(chunks)]

    def out_copies(step, s):
        rows = pl.ds(pl.multiple_of(step * tm, tm), tm)
        return [pltpu.make_async_copy(res_sc.at[s, :, c, :], out_hbm.at[rows, c * LANES:(c + 1) * LANES],
                                      o_sem.at[s]) for c in range(chunks)]

    @pl.when(i == 0)
    def _():
        for cp in x1_copies(0, 0):
            cp.start()

    @pl.when(i + 1 < n_steps)
    def _():
        for cp in x1_copies(i + 1, 1 - slot):
            cp.start()

    @pl.when(i == 0)
    def _():
        gather(dest_ref, 0)

    for parity in range(2):
        @pl.when(jnp.logical_and(i + 1 < n_steps, slot == parity))
        def _():
            gather(nxt_ref, 1 - parity)

    for k in range(TOP_K):
        pltpu.make_async_copy(ybuf_hbm.at[pl.ds(0, tm)], g_sc.at[slot, k], g_sem.at[slot]).wait()
    for cp in x1_copies(i, slot):
        cp.wait()

    for parity in range(2):
        @pl.when(slot == parity)
        def _():
            for r in range(tm):
                acc = x1_sc[parity, r]
                for k in range(TOP_K):
                    acc = acc + w_ref[r * TOP_K + k] * g_sc[parity, k, r]
                res_sc[parity, r] = acc
    for cp in out_copies(i, slot):
        cp.start()

    @pl.when(i >= 1)
    def _():
        for cp in out_copies(i - 1, 1 - slot):
            cp.wait()

    @pl.when(i == n_steps - 1)
    def _():
        for cp in out_copies(i, slot):
            cp.wait()


def _combine(dest_flat, x1, w_flat, ybuf):
    T = x1.shape[0]
    tm = ROUTE_TM
    tile = (D_MODEL // LANES, LANES)
    return pl.pallas_call(
        _combine_kernel,
        out_shape=jax.ShapeDtypeStruct((T, D_MODEL), F32),
        grid=(T // tm,),
        in_specs=[
            pl.BlockSpec((tm * TOP_K,), lambda i: (i,), memory_space=pltpu.SMEM),
            pl.BlockSpec((tm * TOP_K,), lambda i: (jnp.minimum(i + 1, T // tm - 1),),
                         memory_space=pltpu.SMEM),
            pl.BlockSpec((tm * TOP_K,), lambda i: (i,), memory_space=pltpu.SMEM),
            pl.BlockSpec(memory_space=pl.ANY),
            pl.BlockSpec(memory_space=pl.ANY),
        ],
        out_specs=pl.BlockSpec(memory_space=pl.ANY),
        scratch_shapes=[pltpu.VMEM((2, TOP_K, tm) + tile, F32), pltpu.VMEM((2, tm) + tile, F32),
                        pltpu.VMEM((2, tm) + tile, F32), pltpu.SemaphoreType.DMA((2,)),
                        pltpu.SemaphoreType.DMA((2,)), pltpu.SemaphoreType.DMA((2,))],
        compiler_params=pltpu.CompilerParams(
            dimension_semantics=("arbitrary",), vmem_limit_bytes=VMEM_LIMIT),
        name="combine",
    )(dest_flat, dest_flat, w_flat, x1, ybuf)


def _layer(x2, mem2, B, S, g_mix, w_in, g_sgu_v, w_spatial, b_spatial, g_q_moba, g_k_moba,
           g_mem, w_mem_kv, g_q_mem, g_k_mem, w_br_sgu, w_br_moba, w_br_mem, w_out,
           g_ffn, w_router, b_router, w_gate_up, b_gate_up, w_down, b_down):
    T = B * S
    row = lambda v: v.reshape(1, -1)
    proj = _inproj(x2, row(g_mix), w_in.astype(BF16), row(g_sgu_v))
    yb = _moba(proj, g_q_moba, g_k_moba, B, S)
    kvmem = _memkv(mem2, row(g_mem), w_mem_kv.astype(BF16), row(g_k_mem), B)

    causal = jnp.tril(jnp.ones((SGU_CHUNK, SGU_CHUNK), dtype=bool))
    ws = jnp.where(causal[None], w_spatial, 0.0).astype(BF16)
    bsp = jnp.repeat(b_spatial.T, LANES, axis=1)
    x1, h2, top_e, top_w, rank, counts = _merge(
        proj, yb, x2, kvmem, ws, bsp, row(g_q_mem), w_br_sgu.astype(BF16),
        w_br_moba.astype(BF16), w_br_mem.astype(BF16), w_out.astype(BF16), row(g_ffn),
        w_router, b_router, S)

    bm = EXPERT_BM
    n_blocks = -(-(T * TOP_K + N_EXPERTS * (bm - 1)) // bm)
    cnt = counts.reshape(N_EXPERTS).astype(jnp.int32)
    padded = (cnt + bm - 1) // bm * bm
    pad_ends = jnp.cumsum(padded)
    pad_starts = pad_ends - padded
    e_ids = jnp.arange(N_EXPERTS, dtype=jnp.int32)[:, None, None]
    start_of = jnp.sum(jnp.where(top_e[None] == e_ids, pad_starts[:, None, None], 0), axis=0)
    dest = (start_of + rank).T.reshape(T * TOP_K)
    top_w = top_w.T.reshape(T * TOP_K)
    block_row0 = jnp.arange(n_blocks, dtype=jnp.int32) * bm
    block_e = jnp.minimum(
        jnp.sum((pad_ends[None, :] <= block_row0[:, None]).astype(jnp.int32), axis=1), N_EXPERTS - 1)
    n_used = (pad_ends[-1:] // bm).astype(jnp.int32)

    xbuf = _dispatch(pad_starts + cnt, padded - cnt, dest, h2, n_blocks * bm)
    ybuf = _experts(block_e, n_used, xbuf, w_gate_up, b_gate_up.reshape(N_EXPERTS, 1, 2 * D_FF),
                    w_down, b_down.reshape(N_EXPERTS, 1, D_MODEL))
    return _combine(dest, x1, top_w, ybuf)


def kernel(x, mem, g_mix, w_in, g_sgu_v, w_spatial, b_spatial, g_q_moba, g_k_moba, g_mem, w_mem_kv, g_q_mem, g_k_mem, w_br_sgu, w_br_moba, w_br_mem, w_out, g_ffn, w_router, b_router, w_gate_up, b_gate_up, w_down, b_down):
    B, S, D = x.shape
    x2 = x.reshape(B * S, D)
    mem2 = mem.reshape(B * MEM_LEN, D)
    params = (g_mix, w_in, g_sgu_v, w_spatial, b_spatial, g_q_moba, g_k_moba, g_mem, w_mem_kv,
              g_q_mem, g_k_mem, w_br_sgu, w_br_moba, w_br_mem, w_out, g_ffn, w_router, b_router,
              w_gate_up, b_gate_up, w_down, b_down)
    for l in range(g_mix.shape[0]):
        x2 = _layer(x2, mem2, B, S, *(p[l] for p in params))
    return x2.reshape(B, S, D)
```

```python
import functools

import numpy as np
import jax
import jax.numpy as jnp
from jax import lax
from jax.experimental import pallas as pl
from jax.experimental.pallas import tpu as pltpu

F32 = jnp.float32
BF16 = jnp.bfloat16

D_MODEL = 1024
MEM_LEN = 256
SGU_GROUPS = 4
SGU_CHUNK = 128
SGU_WIDTH = 512
MOBA_HEADS = 8
MOBA_HEAD_DIM = 64
MOBA_BLOCK = 256
MOBA_TOPK = 3
MEM_HEADS = 4
MEM_HEAD_DIM = 128
SPLIT = 512
IN_WIDTH = 6144
N_SPLITS = IN_WIDTH // SPLIT
N_EXPERTS = 32
TOP_K = 4
D_FF = 1024
SWIGLU_ALPHA = 1.702
SWIGLU_LIMIT = 7.0
EPS = 1e-6
NEG_INF = -1e30
LANES = 128

VMEM_LIMIT = 56 * 1024 * 1024

INPROJ_TM = 256
MERGE_TM = 512
ROUTE_TM = 256
EXPERT_BM = 512


def _gelu(x):
    return 0.5 * x * (1.0 + lax.erf(x * np.float32(np.sqrt(0.5))))


def _rms(x, g):
    return x * lax.rsqrt(jnp.mean(x * x, axis=-1, keepdims=True) + EPS) * g


def _inproj_kernel(x_ref, gmix_ref, w_ref, gsgu_ref, o_ref):
    h = _rms(x_ref[...], gmix_ref[...]).astype(BF16)
    for c in range(N_SPLITS):
        sl = slice(c * SPLIT, (c + 1) * SPLIT)
        acc = jnp.dot(h, w_ref[:, sl], preferred_element_type=F32)
        if c == 0:
            val = _gelu(acc)
        elif c == 1:
            val = _rms(_gelu(acc), gsgu_ref[...])
        elif c >= 6:
            val = jax.nn.sigmoid(acc)
        else:
            val = acc
        o_ref[:, sl] = val.astype(BF16)


def _inproj(x2, g_mix, w_in, g_sgu_v):
    T = x2.shape[0]
    tm = INPROJ_TM
    return pl.pallas_call(
        _inproj_kernel,
        out_shape=jax.ShapeDtypeStruct((T, IN_WIDTH), BF16),
        grid=(T // tm,),
        in_specs=[
            pl.BlockSpec((tm, D_MODEL), lambda i: (i, 0)),
            pl.BlockSpec((1, D_MODEL), lambda i: (0, 0)),
            pl.BlockSpec((D_MODEL, IN_WIDTH), lambda i: (0, 0)),
            pl.BlockSpec((1, SGU_WIDTH), lambda i: (0, 0)),
        ],
        out_specs=pl.BlockSpec((tm, IN_WIDTH), lambda i: (i, 0)),
        compiler_params=pltpu.CompilerParams(
            dimension_semantics=("arbitrary",), vmem_limit_bytes=VMEM_LIMIT),
        name="inproj",
    )(x2, g_mix, w_in, g_sgu_v)


def _pair_norm(x, g, head0):
    sq = x * x
    s0 = jnp.sum(jnp.where(head0, sq, 0.0), axis=-1, keepdims=True)
    s1 = jnp.sum(jnp.where(head0, 0.0, sq), axis=-1, keepdims=True)
    inv = jnp.where(head0, lax.rsqrt(s0 * (1.0 / MOBA_HEAD_DIM) + EPS),
                    lax.rsqrt(s1 * (1.0 / MOBA_HEAD_DIM) + EPS))
    return x * inv * g


VT_ROWS = MOBA_HEAD_DIM + 16
LOG2E = float(np.log2(np.e))


def _moba_kernel(slopes_ref, q_ref, k_ref, v_ref, gq_ref, gk_ref, bias_ref, o_ref,
                 kn_sc, vt_sc, kmean_sc, mrow_sc, qh_sc, m_sc, acc_sc, t_sc, cmax_sc, *, nb):
    qi = pl.program_id(1)
    bs = MOBA_BLOCK
    pairs = MOBA_HEADS // 2
    head0 = lax.broadcasted_iota(jnp.int32, (1, LANES), 1) < MOBA_HEAD_DIM
    nt = (((1,), (1,)), ((), ()))

    @pl.when(qi == 0)
    def _prepare_kv():
        ones_rows = jnp.where(
            lax.broadcasted_iota(jnp.int32, (VT_ROWS - MOBA_HEAD_DIM, bs), 0) == 0, 1.0, 0.0).astype(BF16)

        def body(j, carry):
            rows = pl.ds(pl.multiple_of(j * bs, bs), bs)
            means = []
            for g in range(pairs):
                cols = slice(g * LANES, (g + 1) * LANES)
                kn = _pair_norm(k_ref[rows, cols].astype(F32), gk_ref[...], head0)
                kn_sc[j, :, cols] = kn.astype(BF16)
                means.append(jnp.mean(kn, axis=0, keepdims=True))
                vt = v_ref[rows, cols].astype(F32).T.astype(BF16)
                for hh in range(2):
                    h = 2 * g + hh
                    vt_sc[j, h, :MOBA_HEAD_DIM, :] = vt[hh * MOBA_HEAD_DIM:(hh + 1) * MOBA_HEAD_DIM]
                    vt_sc[j, h, MOBA_HEAD_DIM:, :] = ones_rows
            kmean_sc[pl.ds(j, 1), :] = jnp.concatenate(means, axis=1)
            return carry
        lax.fori_loop(0, nb, body, 0)

    blk = lax.broadcasted_iota(jnp.int32, (nb, 1), 0)
    eligible = blk < qi
    dist = ((qi - blk) * bs).astype(F32)
    for g in range(pairs):
        cols = slice(g * LANES, (g + 1) * LANES)
        qn = _pair_norm(q_ref[:, cols].astype(F32), gq_ref[...], head0)
        qpair = jnp.concatenate([jnp.where(head0, qn, 0.0), jnp.where(head0, 0.0, qn)], axis=0)
        qh_sc[g] = (qpair * (LOG2E * MOBA_HEAD_DIM ** -0.5)).astype(BF16)
        q_hi = qpair.astype(BF16)
        q_lo = (qpair - q_hi.astype(F32)).astype(BF16)
        km = kmean_sc[:, cols]
        km_hi = km.astype(BF16)
        km_lo = (km - km_hi.astype(F32)).astype(BF16)
        gate = (lax.dot_general(km_hi, q_hi, nt, preferred_element_type=F32)
                + lax.dot_general(km_hi, q_lo, nt, preferred_element_type=F32)
                + lax.dot_general(km_lo, q_hi, nt, preferred_element_type=F32))
        gate = jnp.where(eligible, gate, -jnp.inf)
        rank = jnp.zeros((nb, 2 * bs), F32)
        for i in range(nb - 1):
            gi = gate[i:i + 1, :]
            beats = jnp.logical_or(gi > gate, jnp.logical_and(gi == gate, blk > i))
            rank = rank + jnp.where(beats, 1.0, 0.0)
        sel = jnp.logical_and(eligible, rank < float(MOBA_TOPK))
        for hh in range(2):
            h = 2 * g + hh
            mrow_sc[:, h * bs:(h + 1) * bs] = jnp.where(
                blk == qi, 0.0,
                jnp.where(sel[:, hh * bs:(hh + 1) * bs], (-LOG2E * slopes_ref[h]) * dist, NEG_INF))
    m_sc[...] = jnp.full(m_sc.shape, NEG_INF, F32)
    acc_sc[...] = jnp.zeros(acc_sc.shape, F32)

    def score(j, own, slot):
        scores = [lax.dot_general(kn_sc[j, :, g * LANES:(g + 1) * LANES], qh_sc[g], nt,
                                  preferred_element_type=F32) for g in range(pairs)]
        t = jnp.concatenate(scores, axis=1) + bias_ref[1 if own else 0]
        t_sc[slot] = t
        cmax_sc[slot] = jnp.max(t, axis=0, keepdims=True)

    def accumulate(j, slot):
        mrow = mrow_sc[pl.ds(j, 1), :]
        m_old = m_sc[...]
        m_new = jnp.maximum(m_old, cmax_sc[slot] + mrow)
        shift = m_new - mrow
        alpha = jnp.exp2(m_old - m_new)
        m_sc[...] = m_new
        for h in range(MOBA_HEADS):
            cols = slice(h * bs, (h + 1) * bs)
            p = jnp.exp2(t_sc[slot, :, cols] - shift[:, cols]).astype(BF16)
            pv = jnp.dot(vt_sc[j, h], p, preferred_element_type=F32)
            acc_sc[h] = alpha[:, cols] * acc_sc[h] + pv

    n_past = qi
    score(qi, True, 0)

    def two_blocks(p, carry):
        score(2 * p, False, 1)
        accumulate(jnp.where(p == 0, qi, 2 * p - 1), 0)
        score(2 * p + 1, False, 0)
        accumulate(2 * p, 1)
        return carry
    lax.fori_loop(0, n_past // 2, two_blocks, 0)

    @pl.when(n_past % 2 == 0)
    def _():
        accumulate(jnp.where(n_past == 0, qi, n_past - 1), 0)

    @pl.when(n_past % 2 == 1)
    def _():
        score(n_past - 1, False, 1)
        accumulate(jnp.where(n_past == 1, qi, n_past - 2), 0)
        accumulate(n_past - 1, 1)

    for g in range(pairs):
        halves = []
        for hh in range(2):
            acc = acc_sc[2 * g + hh]
            halves.append(acc[:MOBA_HEAD_DIM] / acc[MOBA_HEAD_DIM:MOBA_HEAD_DIM + 1])
        out_t = jnp.concatenate(halves, axis=0)
        o_ref[:, g * LANES:(g + 1) * LANES] = out_t.T.astype(BF16)


def _moba_bias_tiles():
    slopes = 2.0 ** (-8.0 * np.arange(1, MOBA_HEADS + 1, dtype=np.float32) / MOBA_HEADS)
    kk = np.arange(MOBA_BLOCK)[:, None]
    qq = np.arange(MOBA_BLOCK)[None, :]
    rel = (qq - kk).astype(np.float64)
    past = -slopes.astype(np.float64)[:, None, None] * rel[None] * LOG2E
    own = np.where((kk <= qq)[None], past, NEG_INF)
    side_by_side = lambda a: a.transpose(1, 0, 2).reshape(MOBA_BLOCK, MOBA_HEADS * MOBA_BLOCK)
    return slopes.astype(np.float32), np.stack([side_by_side(past), side_by_side(own)]).astype(np.float32)


def _moba(proj, g_q, g_k, B, S):
    nb = S // MOBA_BLOCK
    width = MOBA_HEADS * MOBA_HEAD_DIM
    slopes, bias = _moba_bias_tiles()
    gq2 = jnp.tile(g_q.reshape(1, MOBA_HEAD_DIM), (1, 2))
    gk2 = jnp.tile(g_k.reshape(1, MOBA_HEAD_DIM), (1, 2))
    full = lambda shape: pl.BlockSpec(shape, lambda b, q, s: (0,) * len(shape))
    grid_spec = pltpu.PrefetchScalarGridSpec(
        num_scalar_prefetch=1,
        grid=(B, nb),
        in_specs=[
            pl.BlockSpec((MOBA_BLOCK, width), lambda b, q, s: (b * nb + q, 2)),
            pl.BlockSpec((S, width), lambda b, q, s: (b, 3)),
            pl.BlockSpec((S, width), lambda b, q, s: (b, 4)),
            full((1, LANES)),
            full((1, LANES)),
            full((2, MOBA_BLOCK, MOBA_HEADS * MOBA_BLOCK)),
        ],
        out_specs=pl.BlockSpec((MOBA_BLOCK, width), lambda b, q, s: (b * nb + q, 0)),
        scratch_shapes=[
            pltpu.VMEM((nb, MOBA_BLOCK, width), BF16),
            pltpu.VMEM((nb, MOBA_HEADS, VT_ROWS, MOBA_BLOCK), BF16),
            pltpu.VMEM((nb, width), F32),
            pltpu.VMEM((nb, MOBA_HEADS * MOBA_BLOCK), F32),
            pltpu.VMEM((MOBA_HEADS // 2, 2 * MOBA_BLOCK, LANES), BF16),
            pltpu.VMEM((1, MOBA_HEADS * MOBA_BLOCK), F32),
            pltpu.VMEM((MOBA_HEADS, VT_ROWS, MOBA_BLOCK), F32),
            pltpu.VMEM((2, MOBA_BLOCK, MOBA_HEADS * MOBA_BLOCK), F32),
            pltpu.VMEM((2, 1, MOBA_HEADS * MOBA_BLOCK), F32),
        ],
    )
    return pl.pallas_call(
        functools.partial(_moba_kernel, nb=nb),
        out_shape=jax.ShapeDtypeStruct((B * S, width), BF16),
        grid_spec=grid_spec,
        compiler_params=pltpu.CompilerParams(
            dimension_semantics=("arbitrary", "arbitrary"), vmem_limit_bytes=VMEM_LIMIT),
        name="moba",
    )(jnp.asarray(slopes), proj, proj, proj, gq2, gk2, jnp.asarray(bias))


def _memkv_kernel(mem_ref, gmem_ref, w_ref, gk_ref, o_ref):
    mn = _rms(mem_ref[...], gmem_ref[...]).astype(BF16)
    kv = jnp.dot(mn, w_ref[...], preferred_element_type=F32)
    width = MEM_HEADS * MEM_HEAD_DIM
    for h in range(MEM_HEADS):
        sl = slice(h * MEM_HEAD_DIM, (h + 1) * MEM_HEAD_DIM)
        o_ref[:, sl] = _rms(kv[:, sl], gk_ref[...]).astype(BF16)
    o_ref[:, width:] = kv[:, width:].astype(BF16)


def _memkv(mem2, g_mem, w_mem_kv, g_k_mem, B):
    width = MEM_HEADS * MEM_HEAD_DIM
    return pl.pallas_call(
        _memkv_kernel,
        out_shape=jax.ShapeDtypeStruct((B * MEM_LEN, 2 * width), BF16),
        grid=(B,),
        in_specs=[
            pl.BlockSpec((MEM_LEN, D_MODEL), lambda b: (b, 0)),
            pl.BlockSpec((1, D_MODEL), lambda b: (0, 0)),
            pl.BlockSpec((D_MODEL, 2 * width), lambda b: (0, 0)),
            pl.BlockSpec((1, MEM_HEAD_DIM), lambda b: (0, 0)),
        ],
        out_specs=pl.BlockSpec((MEM_LEN, 2 * width), lambda b: (b, 0)),
        compiler_params=pltpu.CompilerParams(
            dimension_semantics=("arbitrary",), vmem_limit_bytes=VMEM_LIMIT),
        name="memkv",
    )(mem2, g_mem, w_mem_kv, g_k_mem)


def _merge_kernel(gu_ref, vn_ref, yb_ref, qm_ref, ga_ref, gb_ref, gc_ref, x_ref, kv_ref,
                  ws_ref, bsp_ref, gqm_ref, wa_ref, wb_ref, wc_ref, wo_ref, gffn_ref,
                  wrh_ref, wrl_ref, br_ref,
                  x1_ref, h2_hbm, tope_ref, topw_ref, rank_ref, cnt_ref, run_sc, hp_sc, hp_sem):
    i = pl.program_id(0)
    tm = x_ref.shape[0]

    @pl.when(i == 0)
    def _():
        run_sc[...] = jnp.zeros_like(run_sc)

    ya_chunks = []
    for c in range(tm // SGU_CHUNK):
        rows = slice(c * SGU_CHUNK, (c + 1) * SGU_CHUNK)
        groups = []
        for g in range(SGU_GROUPS):
            cols = slice(g * LANES, (g + 1) * LANES)
            groups.append(jnp.dot(ws_ref[g], vn_ref[rows, cols], preferred_element_type=F32))
        mixed = jnp.concatenate(groups, axis=1) + bsp_ref[...]
        ya_chunks.append(gu_ref[rows, :].astype(F32) * mixed)
    y_a = jnp.concatenate(ya_chunks, axis=0).astype(BF16)

    width = MEM_HEADS * MEM_HEAD_DIM
    yc_heads = []
    for h in range(MEM_HEADS):
        cols = slice(h * MEM_HEAD_DIM, (h + 1) * MEM_HEAD_DIM)
        qn = _rms(qm_ref[:, cols].astype(F32), gqm_ref[...]).astype(BF16)
        s = lax.dot_general(qn, kv_ref[:, cols], (((1,), (1,)), ((), ())),
                            preferred_element_type=F32) * (MEM_HEAD_DIM ** -0.5)
        s = s - jnp.max(s, axis=-1, keepdims=True)
        p = jnp.exp(s)
        p = p / jnp.sum(p, axis=-1, keepdims=True)
        vcols = slice(width + h * MEM_HEAD_DIM, width + (h + 1) * MEM_HEAD_DIM)
        yc_heads.append(jnp.dot(p.astype(BF16), kv_ref[:, vcols], preferred_element_type=F32))
    y_c = jnp.concatenate(yc_heads, axis=1).astype(BF16)

    merged = ga_ref[...].astype(F32) * jnp.dot(y_a, wa_ref[...], preferred_element_type=F32)
    merged = merged + gb_ref[...].astype(F32) * jnp.dot(yb_ref[...], wb_ref[...],
                                                       preferred_element_type=F32)
    merged = merged + gc_ref[...].astype(F32) * jnp.dot(y_c, wc_ref[...],
                                                       preferred_element_type=F32)
    x1 = x_ref[...] + jnp.dot(merged.astype(BF16), wo_ref[...], preferred_element_type=F32)
    x1_ref[...] = x1
    h2 = _rms(x1, gffn_ref[...])
    h2_hi = h2.astype(BF16)
    h2_hi32 = h2_hi.astype(F32)
    h2_lo = (h2 - h2_hi32).astype(BF16)
    bits = lax.bitcast_convert_type(h2_hi32, jnp.uint32)
    half = D_MODEL // 2
    slot = i % 2
    hp_sc[slot] = (bits[:, :half] >> 16) | (bits[:, half:] & jnp.uint32(0xFFFF0000))

    def h2_copies(step, s):
        rows = pl.ds(pl.multiple_of(step * tm, tm), tm)
        return [pltpu.make_async_copy(hp_sc.at[s, :, c * LANES:(c + 1) * LANES], h2_hbm.at[rows, c],
                                      hp_sem.at[s]) for c in range(half // LANES)]

    for cp in h2_copies(i, slot):
        cp.start()

    nt = (((1,), (1,)), ((), ()))
    logits = (lax.dot_general(wrh_ref[...], h2_hi, nt, preferred_element_type=F32)
              + lax.dot_general(wrh_ref[...], h2_lo, nt, preferred_element_type=F32)
              + lax.dot_general(wrl_ref[...], h2_hi, nt, preferred_element_type=F32)
              + br_ref[...])
    row = lax.broadcasted_iota(jnp.int32, (N_EXPERTS, tm), 0)
    work = logits
    top_e, top_l = [], []
    for _ in range(TOP_K):
        mx = jnp.max(work, axis=0, keepdims=True)
        idx = jnp.min(jnp.where(work == mx, row, N_EXPERTS), axis=0, keepdims=True)
        top_e.append(idx)
        top_l.append(mx)
        work = jnp.where(row == idx, -jnp.inf, work)
    exps = [jnp.exp(v - top_l[0]) for v in top_l]
    denom = exps[0] + exps[1] + exps[2] + exps[3]

    onehots = [row == e for e in top_e]
    multi = jnp.zeros((N_EXPERTS, tm), F32)
    for oh in onehots:
        multi = multi + jnp.where(oh, 1.0, 0.0)
    r_i = lax.broadcasted_iota(jnp.int32, (tm, tm), 0)
    c_i = lax.broadcasted_iota(jnp.int32, (tm, tm), 1)
    earlier = jnp.where(r_i < c_i, 1.0, 0.0).astype(BF16)
    pos = run_sc[...] + jnp.dot(multi.astype(BF16), earlier, preferred_element_type=F32)
    run_sc[...] = run_sc[...] + jnp.sum(multi, axis=1, keepdims=True)
    cnt_ref[...] = run_sc[...]

    for k in range(TOP_K):
        tope_ref[k:k + 1, :] = top_e[k]
        topw_ref[k:k + 1, :] = exps[k] / denom
        rank_ref[k:k + 1, :] = jnp.sum(jnp.where(onehots[k], pos, 0.0), axis=0,
                                       keepdims=True).astype(jnp.int32)

    @pl.when(i >= 1)
    def _():
        for cp in h2_copies(i - 1, 1 - slot):
            cp.wait()

    @pl.when(i == pl.num_programs(0) - 1)
    def _():
        for cp in h2_copies(i, slot):
            cp.wait()


def _merge(proj, yb, x2, kvmem, ws, bsp, g_q_mem, wa, wb, wc, wo, g_ffn, w_router, b_router, S):
    T = x2.shape[0]
    tm = MERGE_TM
    per_b = S // tm
    full = lambda shape: pl.BlockSpec(shape, lambda i: (0,) * len(shape))
    in_specs = [
        pl.BlockSpec((tm, SPLIT), lambda i: (i, 0)),
        pl.BlockSpec((tm, SPLIT), lambda i: (i, 1)),
        pl.BlockSpec((tm, SPLIT), lambda i: (i, 0)),
        pl.BlockSpec((tm, SPLIT), lambda i: (i, 5)),
        pl.BlockSpec((tm, D_MODEL), lambda i: (i, 3)),
        pl.BlockSpec((tm, D_MODEL), lambda i: (i, 4)),
        pl.BlockSpec((tm, D_MODEL), lambda i: (i, 5)),
        pl.BlockSpec((tm, D_MODEL), lambda i: (i, 0)),
        pl.BlockSpec((MEM_LEN, 2 * MEM_HEADS * MEM_HEAD_DIM), lambda i: (i // per_b, 0)),
        full((SGU_GROUPS, SGU_CHUNK, SGU_CHUNK)),
        full((SGU_CHUNK, SGU_WIDTH)),
        full((1, MEM_HEAD_DIM)),
        full((SGU_WIDTH, D_MODEL)),
        full((SGU_WIDTH, D_MODEL)),
        full((SGU_WIDTH, D_MODEL)),
        full((D_MODEL, D_MODEL)),
        full((1, D_MODEL)),
        full((N_EXPERTS, D_MODEL)),
        full((N_EXPERTS, D_MODEL)),
        full((N_EXPERTS, 1)),
    ]
    out_shape = (
        jax.ShapeDtypeStruct((T, D_MODEL), F32),
        jax.ShapeDtypeStruct((T, D_MODEL // 2 // LANES, LANES), jnp.uint32),
        jax.ShapeDtypeStruct((TOP_K, T), jnp.int32),
        jax.ShapeDtypeStruct((TOP_K, T), F32),
        jax.ShapeDtypeStruct((TOP_K, T), jnp.int32),
        jax.ShapeDtypeStruct((N_EXPERTS, 1), F32),
    )
    out_specs = (
        pl.BlockSpec((tm, D_MODEL), lambda i: (i, 0)),
        pl.BlockSpec(memory_space=pl.ANY),
        pl.BlockSpec((TOP_K, tm), lambda i: (0, i)),
        pl.BlockSpec((TOP_K, tm), lambda i: (0, i)),
        pl.BlockSpec((TOP_K, tm), lambda i: (0, i)),
        pl.BlockSpec((N_EXPERTS, 1), lambda i: (0, 0)),
    )
    wr_t = w_router.T
    wr_hi = wr_t.astype(BF16)
    wr_lo = (wr_t - wr_hi.astype(F32)).astype(BF16)
    return pl.pallas_call(
        _merge_kernel,
        out_shape=out_shape,
        grid=(T // tm,),
        in_specs=in_specs,
        out_specs=out_specs,
        scratch_shapes=[pltpu.VMEM((N_EXPERTS, 1), F32),
                        pltpu.VMEM((2, tm, D_MODEL // 2), jnp.uint32), pltpu.SemaphoreType.DMA((2,))],
        compiler_params=pltpu.CompilerParams(
            dimension_semantics=("arbitrary",), vmem_limit_bytes=VMEM_LIMIT),
        name="merge",
    )(proj, proj, yb, proj, proj, proj, proj, x2, kvmem, ws, bsp, g_q_mem,
      wa, wb, wc, wo, g_ffn, wr_hi, wr_lo, b_router.reshape(N_EXPERTS, 1))


def _dispatch_kernel(padlo_ref, padlen_ref, dest_ref, h_hbm, xbuf_ref, sem, z_sc, z_sem, h_sc, in_sem):
    i = pl.program_id(0)
    tm = h_sc.shape[1]

    @pl.when(i == 0)
    def _zero_padding_rows():
        z_sc[...] = jnp.zeros(z_sc.shape, z_sc.dtype)
        bits = range(EXPERT_BM.bit_length() - 2, -1, -1)

        def pieces(e):
            off, length = padlo_ref[e], padlen_ref[e]
            for b in bits:
                take = ((length >> b) & 1) == 1
                yield take, pltpu.make_async_copy(z_sc.at[pl.ds(0, 1 << b)],
                                                  xbuf_ref.at[pl.ds(off, 1 << b)], z_sem)
                off = off + jnp.where(take, 1 << b, 0)

        for e in range(N_EXPERTS):
            for take, cp in pieces(e):
                pl.when(take)(cp.start)
        for e in range(N_EXPERTS):
            for take, cp in pieces(e):
                pl.when(take)(cp.wait)

        zrows = z_sc.shape[0]
        first = (padlo_ref[N_EXPERTS - 1] + padlen_ref[N_EXPERTS - 1]) // zrows
        def tail_copy(j):
            return pltpu.make_async_copy(z_sc, xbuf_ref.at[pl.ds(pl.multiple_of(j * zrows, zrows), zrows)], z_sem)
        lax.fori_loop(first, xbuf_ref.shape[0] // zrows, lambda j, c: (tail_copy(j).start(), c)[1], 0)
        lax.fori_loop(first, xbuf_ref.shape[0] // zrows, lambda j, c: (tail_copy(j).wait(), c)[1], 0)

    n_steps = pl.num_programs(0)
    nbuf = h_sc.shape[0]

    def fetch(step, s):
        rows = pl.ds(pl.multiple_of(step * tm, tm), tm)
        return pltpu.make_async_copy(h_hbm.at[rows], h_sc.at[s], in_sem.at[s])

    def drain(s):
        for _ in range(TOP_K):
            pltpu.make_async_copy(h_sc.at[s], xbuf_ref.at[pl.ds(0, tm)], sem.at[s]).wait()

    @pl.when(i == 0)
    def _():
        fetch(0, 0).start()

    @pl.when(i + 1 < n_steps)
    def _():
        fetch(i + 1, (i + 1) % nbuf).start()

    fetch(i, i % nbuf).wait()
    for s in range(nbuf):
        @pl.when(i % nbuf == s)
        def _():
            for r in range(tm):
                for k in range(TOP_K):
                    pltpu.make_async_copy(h_sc.at[s, r], xbuf_ref.at[dest_ref[r * TOP_K + k]],
                                          sem.at[s]).start(priority=k % 2)

            @pl.when(i >= 1)
            def _():
                drain((s - 1) % nbuf)

            @pl.when(i == n_steps - 1)
            def _():
                drain(s)


def _dispatch(pad_lo, pad_len, dest_flat, h2, n_rows):
    T, slabs, lanes = h2.shape
    tm = ROUTE_TM
    grid_spec = pltpu.PrefetchScalarGridSpec(
        num_scalar_prefetch=2,
        grid=(T // tm,),
        in_specs=[
            pl.BlockSpec((tm * TOP_K,), lambda i, lo, ln: (i,), memory_space=pltpu.SMEM),
            pl.BlockSpec(memory_space=pl.ANY),
        ],
        out_specs=pl.BlockSpec(memory_space=pl.ANY),
        scratch_shapes=[pltpu.SemaphoreType.DMA((3,)), pltpu.VMEM((EXPERT_BM // 2, slabs, lanes), h2.dtype),
                        pltpu.SemaphoreType.DMA, pltpu.VMEM((3, tm, slabs, lanes), h2.dtype),
                        pltpu.SemaphoreType.DMA((3,))],
    )
    return pl.pallas_call(
        _dispatch_kernel,
        out_shape=jax.ShapeDtypeStruct((n_rows, slabs, lanes), h2.dtype),
        grid_spec=grid_spec,
        compiler_params=pltpu.CompilerParams(
            dimension_semantics=("arbitrary",), vmem_limit_bytes=VMEM_LIMIT),
        name="dispatch",
    )(pad_lo, pad_len, dest_flat, h2)


FF_CHUNK = 256


def _expert_kernel(be_ref, nused_ref, x_hbm, w1_ref, b1_ref, w2_ref, b2_ref,
                   y_hbm, w1_sc, w2_sc, x_sc, x_sem, y_sc, y_sem, xb_sc):
    i = pl.program_id(0)
    bm = x_sc.shape[1]
    n_used = nused_ref[0]

    def x_copies(blk, slot):
        rows = pl.ds(pl.multiple_of(blk * bm, bm), bm)
        return [pltpu.make_async_copy(x_hbm.at[rows, c], x_sc.at[slot, :, c * LANES:(c + 1) * LANES],
                                      x_sem.at[slot]) for c in range(x_hbm.shape[1])]

    def y_copies(blk, slot):
        rows = pl.ds(pl.multiple_of(blk * bm, bm), bm)
        return [pltpu.make_async_copy(y_sc.at[slot, :, c * LANES:(c + 1) * LANES], y_hbm.at[rows, c],
                                      y_sem.at[slot]) for c in range(y_hbm.shape[1])]

    def unpack(raw_slot, parity):
        packed = x_sc[raw_slot]
        lo = lax.bitcast_convert_type(packed << 16, F32)
        hi = lax.bitcast_convert_type(packed & jnp.uint32(0xFFFF0000), F32)
        xb_sc[parity] = jnp.concatenate([lo, hi], axis=1).astype(BF16)

    @pl.when(i == 0)
    def _():
        x_sc[1] = jnp.zeros(x_sc.shape[1:], x_sc.dtype)
        x_sc[2] = jnp.zeros(x_sc.shape[1:], x_sc.dtype)
        for cp in x_copies(0, 0):
            cp.start()
        for cp in x_copies(0, 0):
            cp.wait()
        unpack(0, 0)

        @pl.when(1 < n_used)
        def _():
            for cp in x_copies(1, 1):
                cp.start()

    @pl.when(i < n_used)
    def _():
        @pl.when(i + 2 < n_used)
        def _():
            for cp in x_copies(i + 2, (i + 2) % 3):
                cp.start()

        @pl.when(i + 1 < n_used)
        def _():
            for cp in x_copies(i + 1, (i + 1) % 3):
                cp.wait()

        @pl.when(jnp.logical_or(i == 0, be_ref[i] != be_ref[jnp.maximum(i - 1, 0)]))
        def _():
            w1_sc[...] = w1_ref[0].astype(BF16)
            w2_sc[...] = w2_ref[0].astype(BF16)

        for parity in range(2):
            @pl.when(i % 2 == parity)
            def _():
                unpack((i + 1) % 3, 1 - parity)
                y = b2_ref[0]
                pending = None
                for c in range(D_FF // FF_CHUNK):
                    g_cols = slice(c * FF_CHUNK, (c + 1) * FF_CHUNK)
                    l_cols = slice(D_FF + c * FF_CHUNK, D_FF + (c + 1) * FF_CHUNK)
                    glu = jnp.dot(xb_sc[parity], w1_sc[:, g_cols],
                                  preferred_element_type=F32) + b1_ref[0, :, g_cols]
                    lin = jnp.dot(xb_sc[parity], w1_sc[:, l_cols],
                                  preferred_element_type=F32) + b1_ref[0, :, l_cols]
                    if pending is not None:
                        y = y + jnp.dot(pending[0], w2_sc[pending[1], :], preferred_element_type=F32)
                    glu = jnp.minimum(glu, SWIGLU_LIMIT)
                    lin = jnp.clip(lin, -SWIGLU_LIMIT, SWIGLU_LIMIT)
                    act = glu * jax.nn.sigmoid(SWIGLU_ALPHA * glu) * (lin + 1.0)
                    pending = (act.astype(BF16), g_cols)
                y = y + jnp.dot(pending[0], w2_sc[pending[1], :], preferred_element_type=F32)
                y_sc[parity] = y

        slot = i % 2
        for cp in y_copies(i, slot):
            cp.start()

        @pl.when(i >= 1)
        def _():
            for cp in y_copies(i - 1, 1 - slot):
                cp.wait()

        @pl.when(i == n_used - 1)
        def _():
            for cp in y_copies(i, slot):
                cp.wait()

    @pl.when(i >= n_used)
    def _():
        y_sc[0] = jnp.zeros(y_sc.shape[1:], F32)
        for cp in y_copies(i, 0):
            cp.start()
        for cp in y_copies(i, 0):
            cp.wait()


def _experts(block_e, n_used, xbuf, w1, b1, w2, b2):
    n_rows, slabs, lanes = xbuf.shape
    bm = EXPERT_BM
    grid_spec = pltpu.PrefetchScalarGridSpec(
        num_scalar_prefetch=2,
        grid=(n_rows // bm,),
        in_specs=[
            pl.BlockSpec(memory_space=pl.ANY),
            pl.BlockSpec((1, D_MODEL, 2 * D_FF), lambda i, be, nu: (be[i], 0, 0)),
            pl.BlockSpec((1, 1, 2 * D_FF), lambda i, be, nu: (be[i], 0, 0)),
            pl.BlockSpec((1, D_FF, D_MODEL), lambda i, be, nu: (be[i], 0, 0)),
            pl.BlockSpec((1, 1, D_MODEL), lambda i, be, nu: (be[i], 0, 0)),
        ],
        out_specs=pl.BlockSpec(memory_space=pl.ANY),
        scratch_shapes=[pltpu.VMEM((D_MODEL, 2 * D_FF), BF16), pltpu.VMEM((D_FF, D_MODEL), BF16),
                        pltpu.VMEM((3, bm, slabs * lanes), jnp.uint32), pltpu.SemaphoreType.DMA((3,)),
                        pltpu.VMEM((2, bm, D_MODEL), F32), pltpu.SemaphoreType.DMA((2,)),
                        pltpu.VMEM((2, bm, D_MODEL), BF16)],
    )
    return pl.pallas_call(
        _expert_kernel,
        out_shape=jax.ShapeDtypeStruct((n_rows, D_MODEL // LANES, LANES), F32),
        grid_spec=grid_spec,
        compiler_params=pltpu.CompilerParams(
            dimension_semantics=("arbitrary",), vmem_limit_bytes=VMEM_LIMIT),
        name="experts",
    )(block_e, n_used, xbuf, w1, b1, w2, b2)


def _combine_kernel(dest_ref, nxt_ref, w_ref, x1_hbm, ybuf_hbm, out_hbm, g_sc, x1_sc, res_sc,
                    g_sem, x_sem, o_sem):
    i = pl.program_id(0)
    n_steps = pl.num_programs(0)
    tm = g_sc.shape[2]
    chunks = D_MODEL // LANES
    slot = i % 2

    def gather(idx_ref, s):
        for r in range(tm):
            for k in range(TOP_K):
                pltpu.make_async_copy(ybuf_hbm.at[idx_ref[r * TOP_K + k]], g_sc.at[s, k, r],
                                      g_sem.at[s]).start(priority=k % 2)

    def x1_copies(step, s):
        rows = pl.ds(pl.multiple_of(step * tm, tm), tm)
        return [pltpu.make_async_copy(x1_hbm.at[rows, c * LANES:(c + 1) * LANES], x1_sc.at[s, :, c, :],
                                      x_sem.at[s]) for c in range(chunks)]

    def out_copies(step, s):
        rows = pl.ds(pl.multiple_of(step * tm, tm), tm)
        return [pltpu.make_async_copy(res_sc.at[s, :, c, :], out_hbm.at[rows, c * LANES:(c + 1) * LANES],
                                      o_sem.at[s]) for c in range(chunks)]

    @pl.when(i == 0)
    def _():
        for cp in x1_copies(0, 0):
            cp.start()

    @pl.when(i + 1 < n_steps)
    def _():
        for cp in x1_copies(i + 1, 1 - slot):
            cp.start()

    @pl.when(i == 0)
    def _():
        gather(dest_ref, 0)

    for parity in range(2):
        @pl.when(jnp.logical_and(i + 1 < n_steps, slot == parity))
        def _():
            gather(nxt_ref, 1 - parity)

    for k in range(TOP_K):
        pltpu.make_async_copy(ybuf_hbm.at[pl.ds(0, tm)], g_sc.at[slot, k], g_sem.at[slot]).wait()
    for cp in x1_copies(i, slot):
        cp.wait()

    for parity in range(2):
        @pl.when(slot == parity)
        def _():
            for r in range(tm):
                acc = x1_sc[parity, r]
                for k in range(TOP_K):
                    acc = acc + w_ref[r * TOP_K + k] * g_sc[parity, k, r]
                res_sc[parity, r] = acc
    for cp in out_copies(i, slot):
        cp.start()

    @pl.when(i >= 1)
    def _():
        for cp in out_copies(i - 1, 1 - slot):
            cp.wait()

    @pl.when(i == n_steps - 1)
    def _():
        for cp in out_copies(i, slot):
            cp.wait()


def _combine(dest_flat, x1, w_flat, ybuf):
    T = x1.shape[0]
    tm = ROUTE_TM
    tile = (D_MODEL // LANES, LANES)
    return pl.pallas_call(
        _combine_kernel,
        out_shape=jax.ShapeDtypeStruct((T, D_MODEL), F32),
        grid=(T // tm,),
        in_specs=[
            pl.BlockSpec((tm * TOP_K,), lambda i: (i,), memory_space=pltpu.SMEM),
            pl.BlockSpec((tm * TOP_K,), lambda i: (jnp.minimum(i + 1, T // tm - 1),),
                         memory_space=pltpu.SMEM),
            pl.BlockSpec((tm * TOP_K,), lambda i: (i,), memory_space=pltpu.SMEM),
            pl.BlockSpec(memory_space=pl.ANY),
            pl.BlockSpec(memory_space=pl.ANY),
        ],
        out_specs=pl.BlockSpec(memory_space=pl.ANY),
        scratch_shapes=[pltpu.VMEM((2, TOP_K, tm) + tile, F32), pltpu.VMEM((2, tm) + tile, F32),
                        pltpu.VMEM((2, tm) + tile, F32), pltpu.SemaphoreType.DMA((2,)),
                        pltpu.SemaphoreType.DMA((2,)), pltpu.SemaphoreType.DMA((2,))],
        compiler_params=pltpu.CompilerParams(
            dimension_semantics=("arbitrary",), vmem_limit_bytes=VMEM_LIMIT),
        name="combine",
    )(dest_flat, dest_flat, w_flat, x1, ybuf)


def _layer(x2, mem2, B, S, g_mix, w_in, g_sgu_v, w_spatial, b_spatial, g_q_moba, g_k_moba,
           g_mem, w_mem_kv, g_q_mem, g_k_mem, w_br_sgu, w_br_moba, w_br_mem, w_out,
           g_ffn, w_router, b_router, w_gate_up, b_gate_up, w_down, b_down):
    T = B * S
    row = lambda v: v.reshape(1, -1)
    proj = _inproj(x2, row(g_mix), w_in.astype(BF16), row(g_sgu_v))
    yb = _moba(proj, g_q_moba, g_k_moba, B, S)
    kvmem = _memkv(mem2, row(g_mem), w_mem_kv.astype(BF16), row(g_k_mem), B)

    causal = jnp.tril(jnp.ones((SGU_CHUNK, SGU_CHUNK), dtype=bool))
    ws = jnp.where(causal[None], w_spatial, 0.0).astype(BF16)
    bsp = jnp.repeat(b_spatial.T, LANES, axis=1)
    x1, h2, top_e, top_w, rank, counts = _merge(
        proj, yb, x2, kvmem, ws, bsp, row(g_q_mem), w_br_sgu.astype(BF16),
        w_br_moba.astype(BF16), w_br_mem.astype(BF16), w_out.astype(BF16), row(g_ffn),
        w_router, b_router, S)

    bm = EXPERT_BM
    n_blocks = -(-(T * TOP_K + N_EXPERTS * (bm - 1)) // bm)
    cnt = counts.reshape(N_EXPERTS).astype(jnp.int32)
    padded = (cnt + bm - 1) // bm * bm
    pad_ends = jnp.cumsum(padded)
    pad_starts = pad_ends - padded
    e_ids = jnp.arange(N_EXPERTS, dtype=jnp.int32)[:, None, None]
    start_of = jnp.sum(jnp.where(top_e[None] == e_ids, pad_starts[:, None, None], 0), axis=0)
    dest = (start_of + rank).T.reshape(T * TOP_K)
    top_w = top_w.T.reshape(T * TOP_K)
    block_row0 = jnp.arange(n_blocks, dtype=jnp.int32) * bm
    block_e = jnp.minimum(
        jnp.sum((pad_ends[None, :] <= block_row0[:, None]).astype(jnp.int32), axis=1), N_EXPERTS - 1)
    n_used = (pad_ends[-1:] // bm).astype(jnp.int32)

    xbuf = _dispatch(pad_starts + cnt, padded - cnt, dest, h2, n_blocks * bm)
    ybuf = _experts(block_e, n_used, xbuf, w_gate_up, b_gate_up.reshape(N_EXPERTS, 1, 2 * D_FF),
                    w_down, b_down.reshape(N_EXPERTS, 1, D_MODEL))
    return _combine(dest, x1, top_w, ybuf)


def kernel(x, mem, g_mix, w_in, g_sgu_v, w_spatial, b_spatial, g_q_moba, g_k_moba, g_mem, w_mem_kv, g_q_mem, g_k_mem, w_br_sgu, w_br_moba, w_br_mem, w_out, g_ffn, w_router, b_router, w_gate_up, b_gate_up, w_down, b_down):
    B, S, D = x.shape
    x2 = x.reshape(B * S, D)
    mem2 = mem.reshape(B * MEM_LEN, D)
    params = (g_mix, w_in, g_sgu_v, w_spatial, b_spatial, g_q_moba, g_k_moba, g_mem, w_mem_kv,
              g_q_mem, g_k_mem, w_br_sgu, w_br_moba, w_br_mem, w_out, g_ffn, w_router, b_router,
              w_gate_up, b_gate_up, w_down, b_down)
    for l in range(g_mix.shape[0]):
        x2 = _layer(x2, mem2, B, S, *(p[l] for p in params))
    return x2.reshape(B, S, D)
```

```python
import functools

import numpy as np
import jax
import jax.numpy as jnp
from jax import lax
from jax.experimental import pallas as pl
from jax.experimental.pallas import tpu as pltpu

F32 = jnp.float32
BF16 = jnp.bfloat16

D_MODEL = 1024
MEM_LEN = 256
SGU_GROUPS = 4
SGU_CHUNK = 128
SGU_WIDTH = 512
MOBA_HEADS = 8
MOBA_HEAD_DIM = 64
MOBA_BLOCK = 256
MOBA_TOPK = 3
MEM_HEADS = 4
MEM_HEAD_DIM = 128
SPLIT = 512
IN_WIDTH = 6144
N_SPLITS = IN_WIDTH // SPLIT
N_EXPERTS = 32
TOP_K = 4
D_FF = 1024
SWIGLU_ALPHA = 1.702
SWIGLU_LIMIT = 7.0
EPS = 1e-6
NEG_INF = -1e30
LANES = 128

VMEM_LIMIT = 56 * 1024 * 1024

INPROJ_TM = 256
MERGE_TM = 512
ROUTE_TM = 256
EXPERT_BM = 512


def _gelu(x):
    return 0.5 * x * (1.0 + lax.erf(x * np.float32(np.sqrt(0.5))))


def _rms(x, g):
    return x * lax.rsqrt(jnp.mean(x * x, axis=-1, keepdims=True) + EPS) * g


def _inproj_kernel(x_ref, gmix_ref, w_ref, gsgu_ref, o_ref):
    h = _rms(x_ref[...], gmix_ref[...]).astype(BF16)
    for c in range(N_SPLITS):
        sl = slice(c * SPLIT, (c + 1) * SPLIT)
        acc = jnp.dot(h, w_ref[:, sl], preferred_element_type=F32)
        if c == 0:
            val = _gelu(acc)
        elif c == 1:
            val = _rms(_gelu(acc), gsgu_ref[...])
        elif c >= 6:
            val = jax.nn.sigmoid(acc)
        else:
            val = acc
        o_ref[:, sl] = val.astype(BF16)


def _inproj(x2, g_mix, w_in, g_sgu_v):
    T = x2.shape[0]
    tm = INPROJ_TM
    return pl.pallas_call(
        _inproj_kernel,
        out_shape=jax.ShapeDtypeStruct((T, IN_WIDTH), BF16),
        grid=(T // tm,),
        in_specs=[
            pl.BlockSpec((tm, D_MODEL), lambda i: (i, 0)),
            pl.BlockSpec((1, D_MODEL), lambda i: (0, 0)),
            pl.BlockSpec((D_MODEL, IN_WIDTH), lambda i: (0, 0)),
            pl.BlockSpec((1, SGU_WIDTH), lambda i: (0, 0)),
        ],
        out_specs=pl.BlockSpec((tm, IN_WIDTH), lambda i: (i, 0)),
        compiler_params=pltpu.CompilerParams(
            dimension_semantics=("arbitrary",), vmem_limit_bytes=VMEM_LIMIT),
        name="inproj",
    )(x2, g_mix, w_in, g_sgu_v)


def _pair_norm(x, g, head0):
    sq = x * x
    s0 = jnp.sum(jnp.where(head0, sq, 0.0), axis=-1, keepdims=True)
    s1 = jnp.sum(jnp.where(head0, 0.0, sq), axis=-1, keepdims=True)
    inv = jnp.where(head0, lax.rsqrt(s0 * (1.0 / MOBA_HEAD_DIM) + EPS),
                    lax.rsqrt(s1 * (1.0 / MOBA_HEAD_DIM) + EPS))
    return x * inv * g


VT_ROWS = MOBA_HEAD_DIM + 16
LOG2E = float(np.log2(np.e))


def _moba_kernel(slopes_ref, q_ref, k_ref, v_ref, gq_ref, gk_ref, bias_ref, o_ref,
                 kn_sc, vt_sc, kmean_sc, mrow_sc, qh_sc, m_sc, acc_sc, t_sc, cmax_sc, *, nb):
    qi = pl.program_id(1)
    bs = MOBA_BLOCK
    pairs = MOBA_HEADS // 2
    head0 = lax.broadcasted_iota(jnp.int32, (1, LANES), 1) < MOBA_HEAD_DIM
    nt = (((1,), (1,)), ((), ()))

    @pl.when(qi == 0)
    def _prepare_kv():
        ones_rows = jnp.where(
            lax.broadcasted_iota(jnp.int32, (VT_ROWS - MOBA_HEAD_DIM, bs), 0) == 0, 1.0, 0.0).astype(BF16)

        def body(j, carry):
            rows = pl.ds(pl.multiple_of(j * bs, bs), bs)
            means = []
            for g in range(pairs):
                cols = slice(g * LANES, (g + 1) * LANES)
                kn = _pair_norm(k_ref[rows, cols].astype(F32), gk_ref[...], head0)
                kn_sc[j, :, cols] = kn.astype(BF16)
                means.append(jnp.mean(kn, axis=0, keepdims=True))
                vt = v_ref[rows, cols].astype(F32).T.astype(BF16)
                for hh in range(2):
                    h = 2 * g + hh
                    vt_sc[j, h, :MOBA_HEAD_DIM, :] = vt[hh * MOBA_HEAD_DIM:(hh + 1) * MOBA_HEAD_DIM]
                    vt_sc[j, h, MOBA_HEAD_DIM:, :] = ones_rows
            kmean_sc[pl.ds(j, 1), :] = jnp.concatenate(means, axis=1)
            return carry
        lax.fori_loop(0, nb, body, 0)

    blk = lax.broadcasted_iota(jnp.int32, (nb, 1), 0)
    eligible = blk < qi
    dist = ((qi - blk) * bs).astype(F32)
    for g in range(pairs):
        cols = slice(g * LANES, (g + 1) * LANES)
        qn = _pair_norm(q_ref[:, cols].astype(F32), gq_ref[...], head0)
        qpair = jnp.concatenate([jnp.where(head0, qn, 0.0), jnp.where(head0, 0.0, qn)], axis=0)
        qh_sc[g] = (qpair * (LOG2E * MOBA_HEAD_DIM ** -0.5)).astype(BF16)
        q_hi = qpair.astype(BF16)
        q_lo = (qpair - q_hi.astype(F32)).astype(BF16)
        km = kmean_sc[:, cols]
        km_hi = km.astype(BF16)
        km_lo = (km - km_hi.astype(F32)).astype(BF16)
        gate = (lax.dot_general(km_hi, q_hi, nt, preferred_element_type=F32)
                + lax.dot_general(km_hi, q_lo, nt, preferred_element_type=F32)
                + lax.dot_general(km_lo, q_hi, nt, preferred_element_type=F32))
        gate = jnp.where(eligible, gate, -jnp.inf)
        rank = jnp.zeros((nb, 2 * bs), F32)
        for i in range(nb - 1):
            gi = gate[i:i + 1, :]
            beats = jnp.logical_or(gi > gate, jnp.logical_and(gi == gate, blk > i))
            rank = rank + jnp.where(beats, 1.0, 0.0)
        sel = jnp.logical_and(eligible, rank < float(MOBA_TOPK))
        for hh in range(2):
            h = 2 * g + hh
            mrow_sc[:, h * bs:(h + 1) * bs] = jnp.where(
                blk == qi, 0.0,
                jnp.where(sel[:, hh * bs:(hh + 1) * bs], (-LOG2E * slopes_ref[h]) * dist, NEG_INF))
    m_sc[...] = jnp.full(m_sc.shape, NEG_INF, F32)
    acc_sc[...] = jnp.zeros(acc_sc.shape, F32)

    def score(j, own, slot):
        scores = [lax.dot_general(kn_sc[j, :, g * LANES:(g + 1) * LANES], qh_sc[g], nt,
                                  preferred_element_type=F32) for g in range(pairs)]
        t = jnp.concatenate(scores, axis=1) + bias_ref[1 if own else 0]
        t_sc[slot] = t
        cmax_sc[slot] = jnp.max(t, axis=0, keepdims=True)

    def accumulate(j, slot):
        mrow = mrow_sc[pl.ds(j, 1), :]
        m_old = m_sc[...]
        m_new = jnp.maximum(m_old, cmax_sc[slot] + mrow)
        shift = m_new - mrow
        alpha = jnp.exp2(m_old - m_new)
        m_sc[...] = m_new
        for h in range(MOBA_HEADS):
            cols = slice(h * bs, (h + 1) * bs)
            p = jnp.exp2(t_sc[slot, :, cols] - shift[:, cols]).astype(BF16)
            pv = jnp.dot(vt_sc[j, h], p, preferred_element_type=F32)
            acc_sc[h] = alpha[:, cols] * acc_sc[h] + pv

    n_past = qi
    score(qi, True, 0)

    def two_blocks(p, carry):
        score(2 * p, False, 1)
        accumulate(jnp.where(p == 0, qi, 2 * p - 1), 0)
        score(2 * p + 1, False, 0)
        accumulate(2 * p, 1)
        return carry
    lax.fori_loop(0, n_past // 2, two_blocks, 0)

    @pl.when(n_past % 2 == 0)
    def _():
        accumulate(jnp.where(n_past == 0, qi, n_past - 1), 0)

    @pl.when(n_past % 2 == 1)
    def _():
        score(n_past - 1, False, 1)
        accumulate(jnp.where(n_past == 1, qi, n_past - 2), 0)
        accumulate(n_past - 1, 1)

    for g in range(pairs):
        halves = []
        for hh in range(2):
            acc = acc_sc[2 * g + hh]
            halves.append(acc[:MOBA_HEAD_DIM] / acc[MOBA_HEAD_DIM:MOBA_HEAD_DIM + 1])
        out_t = jnp.concatenate(halves, axis=0)
        o_ref[:, g * LANES:(g + 1) * LANES] = out_t.T.astype(BF16)


def _moba_bias_tiles():
    slopes = 2.0 ** (-8.0 * np.arange(1, MOBA_HEADS + 1, dtype=np.float32) / MOBA_HEADS)
    kk = np.arange(MOBA_BLOCK)[:, None]
    qq = np.arange(MOBA_BLOCK)[None, :]
    rel = (qq - kk).astype(np.float64)
    past = -slopes.astype(np.float64)[:, None, None] * rel[None] * LOG2E
    own = np.where((kk <= qq)[None], past, NEG_INF)
    side_by_side = lambda a: a.transpose(1, 0, 2).reshape(MOBA_BLOCK, MOBA_HEADS * MOBA_BLOCK)
    return slopes.astype(np.float32), np.stack([side_by_side(past), side_by_side(own)]).astype(np.float32)


def _moba(proj, g_q, g_k, B, S):
    nb = S // MOBA_BLOCK
    width = MOBA_HEADS * MOBA_HEAD_DIM
    slopes, bias = _moba_bias_tiles()
    gq2 = jnp.tile(g_q.reshape(1, MOBA_HEAD_DIM), (1, 2))
    gk2 = jnp.tile(g_k.reshape(1, MOBA_HEAD_DIM), (1, 2))
    full = lambda shape: pl.BlockSpec(shape, lambda b, q, s: (0,) * len(shape))
    grid_spec = pltpu.PrefetchScalarGridSpec(
        num_scalar_prefetch=1,
        grid=(B, nb),
        in_specs=[
            pl.BlockSpec((MOBA_BLOCK, width), lambda b, q, s: (b * nb + q, 2)),
            pl.BlockSpec((S, width), lambda b, q, s: (b, 3)),
            pl.BlockSpec((S, width), lambda b, q, s: (b, 4)),
            full((1, LANES)),
            full((1, LANES)),
            full((2, MOBA_BLOCK, MOBA_HEADS * MOBA_BLOCK)),
        ],
        out_specs=pl.BlockSpec((MOBA_BLOCK, width), lambda b, q, s: (b * nb + q, 0)),
        scratch_shapes=[
            pltpu.VMEM((nb, MOBA_BLOCK, width), BF16),
            pltpu.VMEM((nb, MOBA_HEADS, VT_ROWS, MOBA_BLOCK), BF16),
            pltpu.VMEM((nb, width), F32),
            pltpu.VMEM((nb, MOBA_HEADS * MOBA_BLOCK), F32),
            pltpu.VMEM((MOBA_HEADS // 2, 2 * MOBA_BLOCK, LANES), BF16),
            pltpu.VMEM((1, MOBA_HEADS * MOBA_BLOCK), F32),
            pltpu.VMEM((MOBA_HEADS, VT_ROWS, MOBA_BLOCK), F32),
            pltpu.VMEM((2, MOBA_BLOCK, MOBA_HEADS * MOBA_BLOCK), F32),
            pltpu.VMEM((2, 1, MOBA_HEADS * MOBA_BLOCK), F32),
        ],
    )
    return pl.pallas_call(
        functools.partial(_moba_kernel, nb=nb),
        out_shape=jax.ShapeDtypeStruct((B * S, width), BF16),
        grid_spec=grid_spec,
        compiler_params=pltpu.CompilerParams(
            dimension_semantics=("arbitrary", "arbitrary"), vmem_limit_bytes=VMEM_LIMIT),
        name="moba",
    )(jnp.asarray(slopes), proj, proj, proj, gq2, gk2, jnp.asarray(bias))


def _memkv_kernel(mem_ref, gmem_ref, w_ref, gk_ref, o_ref):
    mn = _rms(mem_ref[...], gmem_ref[...]).astype(BF16)
    kv = jnp.dot(mn, w_ref[...], preferred_element_type=F32)
    width = MEM_HEADS * MEM_HEAD_DIM
    for h in range(MEM_HEADS):
        sl = slice(h * MEM_HEAD_DIM, (h + 1) * MEM_HEAD_DIM)
        o_ref[:, sl] = _rms(kv[:, sl], gk_ref[...]).astype(BF16)
    o_ref[:, width:] = kv[:, width:].astype(BF16)


def _memkv(mem2, g_mem, w_mem_kv, g_k_mem, B):
    width = MEM_HEADS * MEM_HEAD_DIM
    return pl.pallas_call(
        _memkv_kernel,
        out_shape=jax.ShapeDtypeStruct((B * MEM_LEN, 2 * width), BF16),
        grid=(B,),
        in_specs=[
            pl.BlockSpec((MEM_LEN, D_MODEL), lambda b: (b, 0)),
            pl.BlockSpec((1, D_MODEL), lambda b: (0, 0)),
            pl.BlockSpec((D_MODEL, 2 * width), lambda b: (0, 0)),
            pl.BlockSpec((1, MEM_HEAD_DIM), lambda b: (0, 0)),
        ],
        out_specs=pl.BlockSpec((MEM_LEN, 2 * width), lambda b: (b, 0)),
        compiler_params=pltpu.CompilerParams(
            dimension_semantics=("arbitrary",), vmem_limit_bytes=VMEM_LIMIT),
        name="memkv",
    )(mem2, g_mem, w_mem_kv, g_k_mem)


def _merge_kernel(gu_ref, vn_ref, yb_ref, qm_ref, ga_ref, gb_ref, gc_ref, x_ref, kv_ref,
                  ws_ref, bsp_ref, gqm_ref, wa_ref, wb_ref, wc_ref, wo_ref, gffn_ref,
                  wrh_ref, wrl_ref, br_ref,
                  x1_ref, h2_hbm, tope_ref, topw_ref, rank_ref, cnt_ref, run_sc, hp_sc, hp_sem):
    i = pl.program_id(0)
    tm = x_ref.shape[0]

    @pl.when(i == 0)
    def _():
        run_sc[...] = jnp.zeros_like(run_sc)

    ya_chunks = []
    for c in range(tm // SGU_CHUNK):
        rows = slice(c * SGU_CHUNK, (c + 1) * SGU_CHUNK)
        groups = []
        for g in range(SGU_GROUPS):
            cols = slice(g * LANES, (g + 1) * LANES)
            groups.append(jnp.dot(ws_ref[g], vn_ref[rows, cols], preferred_element_type=F32))
        mixed = jnp.concatenate(groups, axis=1) + bsp_ref[...]
        ya_chunks.append(gu_ref[rows, :].astype(F32) * mixed)
    y_a = jnp.concatenate(ya_chunks, axis=0).astype(BF16)

    width = MEM_HEADS * MEM_HEAD_DIM
    yc_heads = []
    for h in range(MEM_HEADS):
        cols = slice(h * MEM_HEAD_DIM, (h + 1) * MEM_HEAD_DIM)
        qn = _rms(qm_ref[:, cols].astype(F32), gqm_ref[...]).astype(BF16)
        s = lax.dot_general(qn, kv_ref[:, cols], (((1,), (1,)), ((), ())),
                            preferred_element_type=F32) * (MEM_HEAD_DIM ** -0.5)
        s = s - jnp.max(s, axis=-1, keepdims=True)
        p = jnp.exp(s)
        p = p / jnp.sum(p, axis=-1, keepdims=True)
        vcols = slice(width + h * MEM_HEAD_DIM, width + (h + 1) * MEM_HEAD_DIM)
        yc_heads.append(jnp.dot(p.astype(BF16), kv_ref[:, vcols], preferred_element_type=F32))
    y_c = jnp.concatenate(yc_heads, axis=1).astype(BF16)

    merged = ga_ref[...].astype(F32) * jnp.dot(y_a, wa_ref[...], preferred_element_type=F32)
    merged = merged + gb_ref[...].astype(F32) * jnp.dot(yb_ref[...], wb_ref[...],
                                                       preferred_element_type=F32)
    merged = merged + gc_ref[...].astype(F32) * jnp.dot(y_c, wc_ref[...],
                                                       preferred_element_type=F32)
    x1 = x_ref[...] + jnp.dot(merged.astype(BF16), wo_ref[...], preferred_element_type=F32)
    x1_ref[...] = x1
    h2 = _rms(x1, gffn_ref[...])
    h2_hi = h2.astype(BF16)
    h2_hi32 = h2_hi.astype(F32)
    h2_lo = (h2 - h2_hi32).astype(BF16)
    bits = lax.bitcast_convert_type(h2_hi32, jnp.uint32)
    half = D_MODEL // 2
    slot = i % 2
    hp_sc[slot] = (bits[:, :half] >> 16) | (bits[:, half:] & jnp.uint32(0xFFFF0000))

    def h2_copies(step, s):
        rows = pl.ds(pl.multiple_of(step * tm, tm), tm)
        return [pltpu.make_async_copy(hp_sc.at[s, :, c * LANES:(c + 1) * LANES], h2_hbm.at[rows, c],
                                      hp_sem.at[s]) for c in range(half // LANES)]

    for cp in h2_copies(i, slot):
        cp.start()

    nt = (((1,), (1,)), ((), ()))
    logits = (lax.dot_general(wrh_ref[...], h2_hi, nt, preferred_element_type=F32)
              + lax.dot_general(wrh_ref[...], h2_lo, nt, preferred_element_type=F32)
              + lax.dot_general(wrl_ref[...], h2_hi, nt, preferred_element_type=F32)
              + br_ref[...])
    row = lax.broadcasted_iota(jnp.int32, (N_EXPERTS, tm), 0)
    work = logits
    top_e, top_l = [], []
    for _ in range(TOP_K):
        mx = jnp.max(work, axis=0, keepdims=True)
        idx = jnp.min(jnp.where(work == mx, row, N_EXPERTS), axis=0, keepdims=True)
        top_e.append(idx)
        top_l.append(mx)
        work = jnp.where(row == idx, -jnp.inf, work)
    exps = [jnp.exp(v - top_l[0]) for v in top_l]
    denom = exps[0] + exps[1] + exps[2] + exps[3]

    onehots = [row == e for e in top_e]
    multi = jnp.zeros((N_EXPERTS, tm), F32)
    for oh in onehots:
        multi = multi + jnp.where(oh, 1.0, 0.0)
    r_i = lax.broadcasted_iota(jnp.int32, (tm, tm), 0)
    c_i = lax.broadcasted_iota(jnp.int32, (tm, tm), 1)
    earlier = jnp.where(r_i < c_i, 1.0, 0.0).astype(BF16)
    pos = run_sc[...] + jnp.dot(multi.astype(BF16), earlier, preferred_element_type=F32)
    run_sc[...] = run_sc[...] + jnp.sum(multi, axis=1, keepdims=True)
    cnt_ref[...] = run_sc[...]

    for k in range(TOP_K):
        tope_ref[k:k + 1, :] = top_e[k]
        topw_ref[k:k + 1, :] = exps[k] / denom
        rank_ref[k:k + 1, :] = jnp.sum(jnp.where(onehots[k], pos, 0.0), axis=0,
                                       keepdims=True).astype(jnp.int32)

    @pl.when(i >= 1)
    def _():
        for cp in h2_copies(i - 1, 1 - slot):
            cp.wait()

    @pl.when(i == pl.num_programs(0) - 1)
    def _():
        for cp in h2_copies(i, slot):
            cp.wait()


def _merge(proj, yb, x2, kvmem, ws, bsp, g_q_mem, wa, wb, wc, wo, g_ffn, w_router, b_router, S):
    T = x2.shape[0]
    tm = MERGE_TM
    per_b = S // tm
    full = lambda shape: pl.BlockSpec(shape, lambda i: (0,) * len(shape))
    in_specs = [
        pl.BlockSpec((tm, SPLIT), lambda i: (i, 0)),
        pl.BlockSpec((tm, SPLIT), lambda i: (i, 1)),
        pl.BlockSpec((tm, SPLIT), lambda i: (i, 0)),
        pl.BlockSpec((tm, SPLIT), lambda i: (i, 5)),
        pl.BlockSpec((tm, D_MODEL), lambda i: (i, 3)),
        pl.BlockSpec((tm, D_MODEL), lambda i: (i, 4)),
        pl.BlockSpec((tm, D_MODEL), lambda i: (i, 5)),
        pl.BlockSpec((tm, D_MODEL), lambda i: (i, 0)),
        pl.BlockSpec((MEM_LEN, 2 * MEM_HEADS * MEM_HEAD_DIM), lambda i: (i // per_b, 0)),
        full((SGU_GROUPS, SGU_CHUNK, SGU_CHUNK)),
        full((SGU_CHUNK, SGU_WIDTH)),
        full((1, MEM_HEAD_DIM)),
        full((SGU_WIDTH, D_MODEL)),
        full((SGU_WIDTH, D_MODEL)),
        full((SGU_WIDTH, D_MODEL)),
        full((D_MODEL, D_MODEL)),
        full((1, D_MODEL)),
        full((N_EXPERTS, D_MODEL)),
        full((N_EXPERTS, D_MODEL)),
        full((N_EXPERTS, 1)),
    ]
    out_shape = (
        jax.ShapeDtypeStruct((T, D_MODEL), F32),
        jax.ShapeDtypeStruct((T, D_MODEL // 2 // LANES, LANES), jnp.uint32),
        jax.ShapeDtypeStruct((TOP_K, T), jnp.int32),
        jax.ShapeDtypeStruct((TOP_K, T), F32),
        jax.ShapeDtypeStruct((TOP_K, T), jnp.int32),
        jax.ShapeDtypeStruct((N_EXPERTS, 1), F32),
    )
    out_specs = (
        pl.BlockSpec((tm, D_MODEL), lambda i: (i, 0)),
        pl.BlockSpec(memory_space=pl.ANY),
        pl.BlockSpec((TOP_K, tm), lambda i: (0, i)),
        pl.BlockSpec((TOP_K, tm), lambda i: (0, i)),
        pl.BlockSpec((TOP_K, tm), lambda i: (0, i)),
        pl.BlockSpec((N_EXPERTS, 1), lambda i: (0, 0)),
    )
    wr_t = w_router.T
    wr_hi = wr_t.astype(BF16)
    wr_lo = (wr_t - wr_hi.astype(F32)).astype(BF16)
    return pl.pallas_call(
        _merge_kernel,
        out_shape=out_shape,
        grid=(T // tm,),
        in_specs=in_specs,
        out_specs=out_specs,
        scratch_shapes=[pltpu.VMEM((N_EXPERTS, 1), F32),
                        pltpu.VMEM((2, tm, D_MODEL // 2), jnp.uint32), pltpu.SemaphoreType.DMA((2,))],
        compiler_params=pltpu.CompilerParams(
            dimension_semantics=("arbitrary",), vmem_limit_bytes=VMEM_LIMIT),
        name="merge",
    )(proj, proj, yb, proj, proj, proj, proj, x2, kvmem, ws, bsp, g_q_mem,
      wa, wb, wc, wo, g_ffn, wr_hi, wr_lo, b_router.reshape(N_EXPERTS, 1))


def _dispatch_kernel(padlo_ref, padlen_ref, dest_ref, h_hbm, xbuf_ref, sem, z_sc, z_sem, h_sc, in_sem):
    i = pl.program_id(0)
    tm = h_sc.shape[1]

    @pl.when(i == 0)
    def _zero_padding_rows():
        z_sc[...] = jnp.zeros(z_sc.shape, z_sc.dtype)
        bits = range(EXPERT_BM.bit_length() - 2, -1, -1)

        def pieces(e):
            off, length = padlo_ref[e], padlen_ref[e]
            for b in bits:
                take = ((length >> b) & 1) == 1
                yield take, pltpu.make_async_copy(z_sc.at[pl.ds(0, 1 << b)],
                                                  xbuf_ref.at[pl.ds(off, 1 << b)], z_sem)
                off = off + jnp.where(take, 1 << b, 0)

        for e in range(N_EXPERTS):
            for take, cp in pieces(e):
                pl.when(take)(cp.start)
        for e in range(N_EXPERTS):
            for take, cp in pieces(e):
                pl.when(take)(cp.wait)

        zrows = z_sc.shape[0]
        first = (padlo_ref[N_EXPERTS - 1] + padlen_ref[N_EXPERTS - 1]) // zrows
        def tail_copy(j):
            return pltpu.make_async_copy(z_sc, xbuf_ref.at[pl.ds(pl.multiple_of(j * zrows, zrows), zrows)], z_sem)
        lax.fori_loop(first, xbuf_ref.shape[0] // zrows, lambda j, c: (tail_copy(j).start(), c)[1], 0)
        lax.fori_loop(first, xbuf_ref.shape[0] // zrows, lambda j, c: (tail_copy(j).wait(), c)[1], 0)

    n_steps = pl.num_programs(0)
    nbuf = h_sc.shape[0]

    def fetch(step, s):
        rows = pl.ds(pl.multiple_of(step * tm, tm), tm)
        return pltpu.make_async_copy(h_hbm.at[rows], h_sc.at[s], in_sem.at[s])

    def drain(s):
        for _ in range(TOP_K):
            pltpu.make_async_copy(h_sc.at[s], xbuf_ref.at[pl.ds(0, tm)], sem.at[s]).wait()

    @pl.when(i == 0)
    def _():
        fetch(0, 0).start()

    @pl.when(i + 1 < n_steps)
    def _():
        fetch(i + 1, (i + 1) % nbuf).start()

    fetch(i, i % nbuf).wait()
    for s in range(nbuf):
        @pl.when(i % nbuf == s)
        def _():
            for r in range(tm):
                for k in range(TOP_K):
                    pltpu.make_async_copy(h_sc.at[s, r], xbuf_ref.at[dest_ref[r * TOP_K + k]],
                                          sem.at[s]).start(priority=k % 2)

            @pl.when(i >= 1)
            def _():
                drain((s - 1) % nbuf)

            @pl.when(i == n_steps - 1)
            def _():
                drain(s)


def _dispatch(pad_lo, pad_len, dest_flat, h2, n_rows):
    T, slabs, lanes = h2.shape
    tm = ROUTE_TM
    grid_spec = pltpu.PrefetchScalarGridSpec(
        num_scalar_prefetch=2,
        grid=(T // tm,),
        in_specs=[
            pl.BlockSpec((tm * TOP_K,), lambda i, lo, ln: (i,), memory_space=pltpu.SMEM),
            pl.BlockSpec(memory_space=pl.ANY),
        ],
        out_specs=pl.BlockSpec(memory_space=pl.ANY),
        scratch_shapes=[pltpu.SemaphoreType.DMA((3,)), pltpu.VMEM((EXPERT_BM // 2, slabs, lanes), h2.dtype),
                        pltpu.SemaphoreType.DMA, pltpu.VMEM((3, tm, slabs, lanes), h2.dtype),
                        pltpu.SemaphoreType.DMA((3,))],
    )
    return pl.pallas_call(
        _dispatch_kernel,
        out_shape=jax.ShapeDtypeStruct((n_rows, slabs, lanes), h2.dtype),
        grid_spec=grid_spec,
        compiler_params=pltpu.CompilerParams(
            dimension_semantics=("arbitrary",), vmem_limit_bytes=VMEM_LIMIT),
        name="dispatch",
    )(pad_lo, pad_len, dest_flat, h2)


FF_CHUNK = 512


def _expert_kernel(be_ref, nused_ref, x_hbm, w1_ref, b1_ref, w2_ref, b2_ref,
                   y_hbm, w1_sc, w2_sc, x_sc, x_sem, y_sc, y_sem):
    i = pl.program_id(0)
    bm = x_sc.shape[1]
    n_used = nused_ref[0]

    def x_copies(blk, slot):
        rows = pl.ds(pl.multiple_of(blk * bm, bm), bm)
        return [pltpu.make_async_copy(x_hbm.at[rows, c], x_sc.at[slot, :, c * LANES:(c + 1) * LANES],
                                      x_sem.at[slot]) for c in range(x_hbm.shape[1])]

    def y_copies(blk, slot):
        rows = pl.ds(pl.multiple_of(blk * bm, bm), bm)
        return [pltpu.make_async_copy(y_sc.at[slot, :, c * LANES:(c + 1) * LANES], y_hbm.at[rows, c],
                                      y_sem.at[slot]) for c in range(y_hbm.shape[1])]

    @pl.when(i < n_used)
    def _():
        slot = i % 2

        @pl.when(i == 0)
        def _():
            for cp in x_copies(0, 0):
                cp.start()

        @pl.when(i + 1 < n_used)
        def _():
            for cp in x_copies(i + 1, 1 - slot):
                cp.start()

        @pl.when(jnp.logical_or(i == 0, be_ref[i] != be_ref[jnp.maximum(i - 1, 0)]))
        def _():
            w1_sc[...] = w1_ref[0].astype(BF16)
            w2_sc[...] = w2_ref[0].astype(BF16)

        for cp in x_copies(i, slot):
            cp.wait()
        packed = x_sc[slot]
        lo = lax.bitcast_convert_type(packed << 16, F32)
        hi = lax.bitcast_convert_type(packed & jnp.uint32(0xFFFF0000), F32)
        xb = jnp.concatenate([lo, hi], axis=1).astype(BF16)
        y = b2_ref[0]
        for c in range(D_FF // FF_CHUNK):
            g_cols = slice(c * FF_CHUNK, (c + 1) * FF_CHUNK)
            l_cols = slice(D_FF + c * FF_CHUNK, D_FF + (c + 1) * FF_CHUNK)
            glu = jnp.dot(xb, w1_sc[:, g_cols], preferred_element_type=F32) + b1_ref[0, :, g_cols]
            lin = jnp.dot(xb, w1_sc[:, l_cols], preferred_element_type=F32) + b1_ref[0, :, l_cols]
            glu = jnp.minimum(glu, SWIGLU_LIMIT)
            lin = jnp.clip(lin, -SWIGLU_LIMIT, SWIGLU_LIMIT)
            act = glu * jax.nn.sigmoid(SWIGLU_ALPHA * glu) * (lin + 1.0)
            y = y + jnp.dot(act.astype(BF16), w2_sc[g_cols, :], preferred_element_type=F32)
        y_sc[slot] = y
        for cp in y_copies(i, slot):
            cp.start()

        @pl.when(i >= 1)
        def _():
            for cp in y_copies(i - 1, 1 - slot):
                cp.wait()

        @pl.when(i == n_used - 1)
        def _():
            for cp in y_copies(i, slot):
                cp.wait()

    @pl.when(i >= n_used)
    def _():
        y_sc[0] = jnp.zeros(y_sc.shape[1:], F32)
        for cp in y_copies(i, 0):
            cp.start()
        for cp in y_copies(i, 0):
            cp.wait()


def _experts(block_e, n_used, xbuf, w1, b1, w2, b2):
    n_rows, slabs, lanes = xbuf.shape
    bm = EXPERT_BM
    grid_spec = pltpu.PrefetchScalarGridSpec(
        num_scalar_prefetch=2,
        grid=(n_rows // bm,),
        in_specs=[
            pl.BlockSpec(memory_space=pl.ANY),
            pl.BlockSpec((1, D_MODEL, 2 * D_FF), lambda i, be, nu: (be[i], 0, 0)),
            pl.BlockSpec((1, 1, 2 * D_FF), lambda i, be, nu: (be[i], 0, 0)),
            pl.BlockSpec((1, D_FF, D_MODEL), lambda i, be, nu: (be[i], 0, 0)),
            pl.BlockSpec((1, 1, D_MODEL), lambda i, be, nu: (be[i], 0, 0)),
        ],
        out_specs=pl.BlockSpec(memory_space=pl.ANY),
        scratch_shapes=[pltpu.VMEM((D_MODEL, 2 * D_FF), BF16), pltpu.VMEM((D_FF, D_MODEL), BF16),
                        pltpu.VMEM((2, bm, slabs * lanes), jnp.uint32), pltpu.SemaphoreType.DMA((2,)),
                        pltpu.VMEM((2, bm, D_MODEL), F32), pltpu.SemaphoreType.DMA((2,))],
    )
    return pl.pallas_call(
        _expert_kernel,
        out_shape=jax.ShapeDtypeStruct((n_rows, D_MODEL // LANES, LANES), F32),
        grid_spec=grid_spec,
        compiler_params=pltpu.CompilerParams(
            dimension_semantics=("arbitrary",), vmem_limit_bytes=VMEM_LIMIT),
        name="experts",
    )(block_e, n_used, xbuf, w1, b1, w2, b2)


def _combine_kernel(dest_ref, nxt_ref, w_ref, x1_hbm, ybuf_hbm, out_hbm, g_sc, x1_sc, res_sc,
                    g_sem, x_sem, o_sem):
    i = pl.program_id(0)
    n_steps = pl.num_programs(0)
    tm = g_sc.shape[2]
    chunks = D_MODEL // LANES
    slot = i % 2

    def gather(idx_ref, s):
        for r in range(tm):
            for k in range(TOP_K):
                pltpu.make_async_copy(ybuf_hbm.at[idx_ref[r * TOP_K + k]], g_sc.at[s, k, r],
                                      g_sem.at[s]).start(priority=k % 2)

    def x1_copies(step, s):
        rows = pl.ds(pl.multiple_of(step * tm, tm), tm)
        return [pltpu.make_async_copy(x1_hbm.at[rows, c * LANES:(c + 1) * LANES], x1_sc.at[s, :, c, :],
                                      x_sem.at[s]) for c in range(chunks)]

    def out_copies(step, s):
        rows = pl.ds(pl.multiple_of(step * tm, tm), tm)
        return [pltpu.make_async_copy(res_sc.at[s, :, c, :], out_hbm.at[rows, c * LANES:(c + 1) * LANES],
                                      o_sem.at[s]) for c in range(chunks)]

    @pl.when(i == 0)
    def _():
        for cp in x1_copies(0, 0):
            cp.start()

    @pl.when(i + 1 < n_steps)
    def _():
        for cp in x1_copies(i + 1, 1 - slot):
            cp.start()

    @pl.when(i == 0)
    def _():
        gather(dest_ref, 0)

    for parity in range(2):
        @pl.when(jnp.logical_and(i + 1 < n_steps, slot == parity))
        def _():
            gather(nxt_ref, 1 - parity)

    for k in range(TOP_K):
        pltpu.make_async_copy(ybuf_hbm.at[pl.ds(0, tm)], g_sc.at[slot, k], g_sem.at[slot]).wait()
    for cp in x1_copies(i, slot):
        cp.wait()

    for parity in range(2):
        @pl.when(slot == parity)
        def _():
            for r in range(tm):
                acc = x1_sc[parity, r]
                for k in range(TOP_K):
                    acc = acc + w_ref[r * TOP_K + k] * g_sc[parity, k, r]
                res_sc[parity, r] = acc
    for cp in out_copies(i, slot):
        cp.start()

    @pl.when(i >= 1)
    def _():
        for cp in out_copies(i - 1, 1 - slot):
            cp.wait()

    @pl.when(i == n_steps - 1)
    def _():
        for cp in out_copies(i, slot):
            cp.wait()


def _combine(dest_flat, x1, w_flat, ybuf):
    T = x1.shape[0]
    tm = ROUTE_TM
    tile = (D_MODEL // LANES, LANES)
    return pl.pallas_call(
        _combine_kernel,
        out_shape=jax.ShapeDtypeStruct((T, D_MODEL), F32),
        grid=(T // tm,),
        in_specs=[
            pl.BlockSpec((tm * TOP_K,), lambda i: (i,), memory_space=pltpu.SMEM),
            pl.BlockSpec((tm * TOP_K,), lambda i: (jnp.minimum(i + 1, T // tm - 1),),
                         memory_space=pltpu.SMEM),
            pl.BlockSpec((tm * TOP_K,), lambda i: (i,), memory_space=pltpu.SMEM),
            pl.BlockSpec(memory_space=pl.ANY),
            pl.BlockSpec(memory_space=pl.ANY),
        ],
        out_specs=pl.BlockSpec(memory_space=pl.ANY),
        scratch_shapes=[pltpu.VMEM((2, TOP_K, tm) + tile, F32), pltpu.VMEM((2, tm) + tile, F32),
                        pltpu.VMEM((2, tm) + tile, F32), pltpu.SemaphoreType.DMA((2,)),
                        pltpu.SemaphoreType.DMA((2,)), pltpu.SemaphoreType.DMA((2,))],
        compiler_params=pltpu.CompilerParams(
            dimension_semantics=("arbitrary",), vmem_limit_bytes=VMEM_LIMIT),
        name="combine",
    )(dest_flat, dest_flat, w_flat, x1, ybuf)


def _layer(x2, mem2, B, S, g_mix, w_in, g_sgu_v, w_spatial, b_spatial, g_q_moba, g_k_moba,
           g_mem, w_mem_kv, g_q_mem, g_k_mem, w_br_sgu, w_br_moba, w_br_mem, w_out,
           g_ffn, w_router, b_router, w_gate_up, b_gate_up, w_down, b_down):
    T = B * S
    row = lambda v: v.reshape(1, -1)
    proj = _inproj(x2, row(g_mix), w_in.astype(BF16), row(g_sgu_v))
    yb = _moba(proj, g_q_moba, g_k_moba, B, S)
    kvmem = _memkv(mem2, row(g_mem), w_mem_kv.astype(BF16), row(g_k_mem), B)

    causal = jnp.tril(jnp.ones((SGU_CHUNK, SGU_CHUNK), dtype=bool))
    ws = jnp.where(causal[None], w_spatial, 0.0).astype(BF16)
    bsp = jnp.repeat(b_spatial.T, LANES, axis=1)
    x1, h2, top_e, top_w, rank, counts = _merge(
        proj, yb, x2, kvmem, ws, bsp, row(g_q_mem), w_br_sgu.astype(BF16),
        w_br_moba.astype(BF16), w_br_mem.astype(BF16), w_out.astype(BF16), row(g_ffn),
        w_router, b_router, S)

    bm = EXPERT_BM
    n_blocks = -(-(T * TOP_K + N_EXPERTS * (bm - 1)) // bm)
    cnt = counts.reshape(N_EXPERTS).astype(jnp.int32)
    padded = (cnt + bm - 1) // bm * bm
    pad_ends = jnp.cumsum(padded)
    pad_starts = pad_ends - padded
    e_ids = jnp.arange(N_EXPERTS, dtype=jnp.int32)[:, None, None]
    start_of = jnp.sum(jnp.where(top_e[None] == e_ids, pad_starts[:, None, None], 0), axis=0)
    dest = (start_of + rank).T.reshape(T * TOP_K)
    top_w = top_w.T.reshape(T * TOP_K)
    block_row0 = jnp.arange(n_blocks, dtype=jnp.int32) * bm
    block_e = jnp.minimum(
        jnp.sum((pad_ends[None, :] <= block_row0[:, None]).astype(jnp.int32), axis=1), N_EXPERTS - 1)
    n_used = (pad_ends[-1:] // bm).astype(jnp.int32)

    xbuf = _dispatch(pad_starts + cnt, padded - cnt, dest, h2, n_blocks * bm)
    ybuf = _experts(block_e, n_used, xbuf, w_gate_up, b_gate_up.reshape(N_EXPERTS, 1, 2 * D_FF),
                    w_down, b_down.reshape(N_EXPERTS, 1, D_MODEL))
    return _combine(dest, x1, top_w, ybuf)


def kernel(x, mem, g_mix, w_in, g_sgu_v, w_spatial, b_spatial, g_q_moba, g_k_moba, g_mem, w_mem_kv, g_q_mem, g_k_mem, w_br_sgu, w_br_moba, w_br_mem, w_out, g_ffn, w_router, b_router, w_gate_up, b_gate_up, w_down, b_down):
    B, S, D = x.shape
    x2 = x.reshape(B * S, D)
    mem2 = mem.reshape(B * MEM_LEN, D)
    params = (g_mix, w_in, g_sgu_v, w_spatial, b_spatial, g_q_moba, g_k_moba, g_mem, w_mem_kv,
              g_q_mem, g_k_mem, w_br_sgu, w_br_moba, w_br_mem, w_out, g_ffn, w_router, b_router,
              w_gate_up, b_gate_up, w_down, b_down)
    for l in range(g_mix.shape[0]):
        x2 = _layer(x2, mem2, B, S, *(p[l] for p in params))
    return x2.reshape(B, S, D)
```

```python
import functools

import numpy as np
import jax
import jax.numpy as jnp
from jax import lax
from jax.experimental import pallas as pl
from jax.experimental.pallas import tpu as pltpu

F32 = jnp.float32
BF16 = jnp.bfloat16

D_MODEL = 1024
MEM_LEN = 256
SGU_GROUPS = 4
SGU_CHUNK = 128
SGU_WIDTH = 512
MOBA_HEADS = 8
MOBA_HEAD_DIM = 64
MOBA_BLOCK = 256
MOBA_TOPK = 3
MEM_HEADS = 4
MEM_HEAD_DIM = 128
SPLIT = 512
IN_WIDTH = 6144
N_SPLITS = IN_WIDTH // SPLIT
N_EXPERTS = 32
TOP_K = 4
D_FF = 1024
SWIGLU_ALPHA = 1.702
SWIGLU_LIMIT = 7.0
EPS = 1e-6
NEG_INF = -1e30
LANES = 128

VMEM_LIMIT = 56 * 1024 * 1024

INPROJ_TM = 256
MERGE_TM = 512
ROUTE_TM = 256
EXPERT_BM = 512


def _gelu(x):
    return 0.5 * x * (1.0 + lax.erf(x * np.float32(np.sqrt(0.5))))


def _rms(x, g):
    return x * lax.rsqrt(jnp.mean(x * x, axis=-1, keepdims=True) + EPS) * g


def _inproj_kernel(x_ref, gmix_ref, w_ref, gsgu_ref, o_ref):
    h = _rms(x_ref[...], gmix_ref[...]).astype(BF16)
    for c in range(N_SPLITS):
        sl = slice(c * SPLIT, (c + 1) * SPLIT)
        acc = jnp.dot(h, w_ref[:, sl], preferred_element_type=F32)
        if c == 0:
            val = _gelu(acc)
        elif c == 1:
            val = _rms(_gelu(acc), gsgu_ref[...])
        elif c >= 6:
            val = jax.nn.sigmoid(acc)
        else:
            val = acc
        o_ref[:, sl] = val.astype(BF16)


def _inproj(x2, g_mix, w_in, g_sgu_v):
    T = x2.shape[0]
    tm = INPROJ_TM
    return pl.pallas_call(
        _inproj_kernel,
        out_shape=jax.ShapeDtypeStruct((T, IN_WIDTH), BF16),
        grid=(T // tm,),
        in_specs=[
            pl.BlockSpec((tm, D_MODEL), lambda i: (i, 0)),
            pl.BlockSpec((1, D_MODEL), lambda i: (0, 0)),
            pl.BlockSpec((D_MODEL, IN_WIDTH), lambda i: (0, 0)),
            pl.BlockSpec((1, SGU_WIDTH), lambda i: (0, 0)),
        ],
        out_specs=pl.BlockSpec((tm, IN_WIDTH), lambda i: (i, 0)),
        compiler_params=pltpu.CompilerParams(
            dimension_semantics=("arbitrary",), vmem_limit_bytes=VMEM_LIMIT),
        name="inproj",
    )(x2, g_mix, w_in, g_sgu_v)


def _pair_norm(x, g, head0):
    sq = x * x
    s0 = jnp.sum(jnp.where(head0, sq, 0.0), axis=-1, keepdims=True)
    s1 = jnp.sum(jnp.where(head0, 0.0, sq), axis=-1, keepdims=True)
    inv = jnp.where(head0, lax.rsqrt(s0 * (1.0 / MOBA_HEAD_DIM) + EPS),
                    lax.rsqrt(s1 * (1.0 / MOBA_HEAD_DIM) + EPS))
    return x * inv * g


VT_ROWS = MOBA_HEAD_DIM + 16
LOG2E = float(np.log2(np.e))


def _moba_kernel(slopes_ref, q_ref, k_ref, v_ref, gq_ref, gk_ref, bias_ref, o_ref,
                 kn_sc, vt_sc, kmean_sc, mrow_sc, qh_sc, m_sc, acc_sc, t_sc, cmax_sc, *, nb):
    qi = pl.program_id(1)
    bs = MOBA_BLOCK
    pairs = MOBA_HEADS // 2
    head0 = lax.broadcasted_iota(jnp.int32, (1, LANES), 1) < MOBA_HEAD_DIM
    nt = (((1,), (1,)), ((), ()))

    @pl.when(qi == 0)
    def _prepare_kv():
        ones_rows = jnp.where(
            lax.broadcasted_iota(jnp.int32, (VT_ROWS - MOBA_HEAD_DIM, bs), 0) == 0, 1.0, 0.0).astype(BF16)

        def body(j, carry):
            rows = pl.ds(pl.multiple_of(j * bs, bs), bs)
            means = []
            for g in range(pairs):
                cols = slice(g * LANES, (g + 1) * LANES)
                kn = _pair_norm(k_ref[rows, cols].astype(F32), gk_ref[...], head0)
                kn_sc[j, :, cols] = kn.astype(BF16)
                means.append(jnp.mean(kn, axis=0, keepdims=True))
                vt = v_ref[rows, cols].astype(F32).T.astype(BF16)
                for hh in range(2):
                    h = 2 * g + hh
                    vt_sc[j, h, :MOBA_HEAD_DIM, :] = vt[hh * MOBA_HEAD_DIM:(hh + 1) * MOBA_HEAD_DIM]
                    vt_sc[j, h, MOBA_HEAD_DIM:, :] = ones_rows
            kmean_sc[pl.ds(j, 1), :] = jnp.concatenate(means, axis=1)
            return carry
        lax.fori_loop(0, nb, body, 0)

    blk = lax.broadcasted_iota(jnp.int32, (nb, 1), 0)
    eligible = blk < qi
    dist = ((qi - blk) * bs).astype(F32)
    for g in range(pairs):
        cols = slice(g * LANES, (g + 1) * LANES)
        qn = _pair_norm(q_ref[:, cols].astype(F32), gq_ref[...], head0)
        qpair = jnp.concatenate([jnp.where(head0, qn, 0.0), jnp.where(head0, 0.0, qn)], axis=0)
        qh_sc[g] = (qpair * (LOG2E * MOBA_HEAD_DIM ** -0.5)).astype(BF16)
        q_hi = qpair.astype(BF16)
        q_lo = (qpair - q_hi.astype(F32)).astype(BF16)
        km = kmean_sc[:, cols]
        km_hi = km.astype(BF16)
        km_lo = (km - km_hi.astype(F32)).astype(BF16)
        gate = (lax.dot_general(km_hi, q_hi, nt, preferred_element_type=F32)
                + lax.dot_general(km_hi, q_lo, nt, preferred_element_type=F32)
                + lax.dot_general(km_lo, q_hi, nt, preferred_element_type=F32))
        work = jnp.where(eligible, gate, -jnp.inf)
        sel = jnp.zeros((nb, 2 * bs), jnp.bool_)
        for _ in range(MOBA_TOPK):
            best = jnp.max(work, axis=0, keepdims=True)
            first = jnp.min(jnp.where(work == best, blk, nb), axis=0, keepdims=True)
            pick = jnp.logical_and(blk == first, best > -jnp.inf)
            sel = jnp.logical_or(sel, pick)
            work = jnp.where(pick, -jnp.inf, work)
        for hh in range(2):
            h = 2 * g + hh
            mrow_sc[:, h * bs:(h + 1) * bs] = jnp.where(
                blk == qi, 0.0,
                jnp.where(sel[:, hh * bs:(hh + 1) * bs], (-LOG2E * slopes_ref[h]) * dist, NEG_INF))
    m_sc[...] = jnp.full(m_sc.shape, NEG_INF, F32)
    acc_sc[...] = jnp.zeros(acc_sc.shape, F32)

    def score(j, own, slot):
        scores = [lax.dot_general(kn_sc[j, :, g * LANES:(g + 1) * LANES], qh_sc[g], nt,
                                  preferred_element_type=F32) for g in range(pairs)]
        t = jnp.concatenate(scores, axis=1) + bias_ref[1 if own else 0]
        t_sc[slot] = t
        cmax_sc[slot] = jnp.max(t, axis=0, keepdims=True)

    def accumulate(j, slot):
        mrow = mrow_sc[pl.ds(j, 1), :]
        m_old = m_sc[...]
        m_new = jnp.maximum(m_old, cmax_sc[slot] + mrow)
        shift = m_new - mrow
        alpha = jnp.exp2(m_old - m_new)
        m_sc[...] = m_new
        for h in range(MOBA_HEADS):
            cols = slice(h * bs, (h + 1) * bs)
            p = jnp.exp2(t_sc[slot, :, cols] - shift[:, cols]).astype(BF16)
            pv = jnp.dot(vt_sc[j, h], p, preferred_element_type=F32)
            acc_sc[h] = alpha[:, cols] * acc_sc[h] + pv

    n_past = qi
    score(qi, True, 0)

    def two_blocks(p, carry):
        score(2 * p, False, 1)
        accumulate(jnp.where(p == 0, qi, 2 * p - 1), 0)
        score(2 * p + 1, False, 0)
        accumulate(2 * p, 1)
        return carry
    lax.fori_loop(0, n_past // 2, two_blocks, 0)

    @pl.when(n_past % 2 == 0)
    def _():
        accumulate(jnp.where(n_past == 0, qi, n_past - 1), 0)

    @pl.when(n_past % 2 == 1)
    def _():
        score(n_past - 1, False, 1)
        accumulate(jnp.where(n_past == 1, qi, n_past - 2), 0)
        accumulate(n_past - 1, 1)

    for g in range(pairs):
        halves = []
        for hh in range(2):
            acc = acc_sc[2 * g + hh]
            halves.append(acc[:MOBA_HEAD_DIM] / acc[MOBA_HEAD_DIM:MOBA_HEAD_DIM + 1])
        out_t = jnp.concatenate(halves, axis=0)
        o_ref[:, g * LANES:(g + 1) * LANES] = out_t.T.astype(BF16)


def _moba_bias_tiles():
    slopes = 2.0 ** (-8.0 * np.arange(1, MOBA_HEADS + 1, dtype=np.float32) / MOBA_HEADS)
    kk = np.arange(MOBA_BLOCK)[:, None]
    qq = np.arange(MOBA_BLOCK)[None, :]
    rel = (qq - kk).astype(np.float64)
    past = -slopes.astype(np.float64)[:, None, None] * rel[None] * LOG2E
    own = np.where((kk <= qq)[None], past, NEG_INF)
    side_by_side = lambda a: a.transpose(1, 0, 2).reshape(MOBA_BLOCK, MOBA_HEADS * MOBA_BLOCK)
    return slopes.astype(np.float32), np.stack([side_by_side(past), side_by_side(own)]).astype(np.float32)


def _moba(proj, g_q, g_k, B, S):
    nb = S // MOBA_BLOCK
    width = MOBA_HEADS * MOBA_HEAD_DIM
    slopes, bias = _moba_bias_tiles()
    gq2 = jnp.tile(g_q.reshape(1, MOBA_HEAD_DIM), (1, 2))
    gk2 = jnp.tile(g_k.reshape(1, MOBA_HEAD_DIM), (1, 2))
    full = lambda shape: pl.BlockSpec(shape, lambda b, q, s: (0,) * len(shape))
    grid_spec = pltpu.PrefetchScalarGridSpec(
        num_scalar_prefetch=1,
        grid=(B, nb),
        in_specs=[
            pl.BlockSpec((MOBA_BLOCK, width), lambda b, q, s: (b * nb + q, 2)),
            pl.BlockSpec((S, width), lambda b, q, s: (b, 3)),
            pl.BlockSpec((S, width), lambda b, q, s: (b, 4)),
            full((1, LANES)),
            full((1, LANES)),
            full((2, MOBA_BLOCK, MOBA_HEADS * MOBA_BLOCK)),
        ],
        out_specs=pl.BlockSpec((MOBA_BLOCK, width), lambda b, q, s: (b * nb + q, 0)),
        scratch_shapes=[
            pltpu.VMEM((nb, MOBA_BLOCK, width), BF16),
            pltpu.VMEM((nb, MOBA_HEADS, VT_ROWS, MOBA_BLOCK), BF16),
            pltpu.VMEM((nb, width), F32),
            pltpu.VMEM((nb, MOBA_HEADS * MOBA_BLOCK), F32),
            pltpu.VMEM((MOBA_HEADS // 2, 2 * MOBA_BLOCK, LANES), BF16),
            pltpu.VMEM((1, MOBA_HEADS * MOBA_BLOCK), F32),
            pltpu.VMEM((MOBA_HEADS, VT_ROWS, MOBA_BLOCK), F32),
            pltpu.VMEM((2, MOBA_BLOCK, MOBA_HEADS * MOBA_BLOCK), F32),
            pltpu.VMEM((2, 1, MOBA_HEADS * MOBA_BLOCK), F32),
        ],
    )
    return pl.pallas_call(
        functools.partial(_moba_kernel, nb=nb),
        out_shape=jax.ShapeDtypeStruct((B * S, width), BF16),
        grid_spec=grid_spec,
        compiler_params=pltpu.CompilerParams(
            dimension_semantics=("arbitrary", "arbitrary"), vmem_limit_bytes=VMEM_LIMIT),
        name="moba",
    )(jnp.asarray(slopes), proj, proj, proj, gq2, gk2, jnp.asarray(bias))


def _memkv_kernel(mem_ref, gmem_ref, w_ref, gk_ref, o_ref):
    mn = _rms(mem_ref[...], gmem_ref[...]).astype(BF16)
    kv = jnp.dot(mn, w_ref[...], preferred_element_type=F32)
    width = MEM_HEADS * MEM_HEAD_DIM
    for h in range(MEM_HEADS):
        sl = slice(h * MEM_HEAD_DIM, (h + 1) * MEM_HEAD_DIM)
        o_ref[:, sl] = _rms(kv[:, sl], gk_ref[...]).astype(BF16)
    o_ref[:, width:] = kv[:, width:].astype(BF16)


def _memkv(mem2, g_mem, w_mem_kv, g_k_mem, B):
    width = MEM_HEADS * MEM_HEAD_DIM
    return pl.pallas_call(
        _memkv_kernel,
        out_shape=jax.ShapeDtypeStruct((B * MEM_LEN, 2 * width), BF16),
        grid=(B,),
        in_specs=[
            pl.BlockSpec((MEM_LEN, D_MODEL), lambda b: (b, 0)),
            pl.BlockSpec((1, D_MODEL), lambda b: (0, 0)),
            pl.BlockSpec((D_MODEL, 2 * width), lambda b: (0, 0)),
            pl.BlockSpec((1, MEM_HEAD_DIM), lambda b: (0, 0)),
        ],
        out_specs=pl.BlockSpec((MEM_LEN, 2 * width), lambda b: (b, 0)),
        compiler_params=pltpu.CompilerParams(
            dimension_semantics=("arbitrary",), vmem_limit_bytes=VMEM_LIMIT),
        name="memkv",
    )(mem2, g_mem, w_mem_kv, g_k_mem)


def _merge_kernel(gu_ref, vn_ref, yb_ref, qm_ref, ga_ref, gb_ref, gc_ref, x_ref, kv_ref,
                  ws_ref, bsp_ref, gqm_ref, wa_ref, wb_ref, wc_ref, wo_ref, gffn_ref,
                  wrh_ref, wrl_ref, br_ref,
                  x1_ref, h2_hbm, tope_ref, topw_ref, rank_ref, cnt_ref, run_sc, hp_sc, hp_sem):
    i = pl.program_id(0)
    tm = x_ref.shape[0]

    @pl.when(i == 0)
    def _():
        run_sc[...] = jnp.zeros_like(run_sc)

    ya_chunks = []
    for c in range(tm // SGU_CHUNK):
        rows = slice(c * SGU_CHUNK, (c + 1) * SGU_CHUNK)
        groups = []
        for g in range(SGU_GROUPS):
            cols = slice(g * LANES, (g + 1) * LANES)
            groups.append(jnp.dot(ws_ref[g], vn_ref[rows, cols], preferred_element_type=F32))
        mixed = jnp.concatenate(groups, axis=1) + bsp_ref[...]
        ya_chunks.append(gu_ref[rows, :].astype(F32) * mixed)
    y_a = jnp.concatenate(ya_chunks, axis=0).astype(BF16)

    width = MEM_HEADS * MEM_HEAD_DIM
    yc_heads = []
    for h in range(MEM_HEADS):
        cols = slice(h * MEM_HEAD_DIM, (h + 1) * MEM_HEAD_DIM)
        qn = _rms(qm_ref[:, cols].astype(F32), gqm_ref[...]).astype(BF16)
        s = lax.dot_general(qn, kv_ref[:, cols], (((1,), (1,)), ((), ())),
                            preferred_element_type=F32) * (MEM_HEAD_DIM ** -0.5)
        s = s - jnp.max(s, axis=-1, keepdims=True)
        p = jnp.exp(s)
        p = p / jnp.sum(p, axis=-1, keepdims=True)
        vcols = slice(width + h * MEM_HEAD_DIM, width + (h + 1) * MEM_HEAD_DIM)
        yc_heads.append(jnp.dot(p.astype(BF16), kv_ref[:, vcols], preferred_element_type=F32))
    y_c = jnp.concatenate(yc_heads, axis=1).astype(BF16)

    merged = ga_ref[...].astype(F32) * jnp.dot(y_a, wa_ref[...], preferred_element_type=F32)
    merged = merged + gb_ref[...].astype(F32) * jnp.dot(yb_ref[...], wb_ref[...],
                                                       preferred_element_type=F32)
    merged = merged + gc_ref[...].astype(F32) * jnp.dot(y_c, wc_ref[...],
                                                       preferred_element_type=F32)
    x1 = x_ref[...] + jnp.dot(merged.astype(BF16), wo_ref[...], preferred_element_type=F32)
    x1_ref[...] = x1
    h2 = _rms(x1, gffn_ref[...])
    h2_hi = h2.astype(BF16)
    h2_hi32 = h2_hi.astype(F32)
    h2_lo = (h2 - h2_hi32).astype(BF16)
    bits = lax.bitcast_convert_type(h2_hi32, jnp.uint32)
    half = D_MODEL // 2
    slot = i % 2
    hp_sc[slot] = (bits[:, :half] >> 16) | (bits[:, half:] & jnp.uint32(0xFFFF0000))

    def h2_copies(step, s):
        rows = pl.ds(pl.multiple_of(step * tm, tm), tm)
        return [pltpu.make_async_copy(hp_sc.at[s, :, c * LANES:(c + 1) * LANES], h2_hbm.at[rows, c],
                                      hp_sem.at[s]) for c in range(half // LANES)]

    for cp in h2_copies(i, slot):
        cp.start()

    nt = (((1,), (1,)), ((), ()))
    logits = (lax.dot_general(wrh_ref[...], h2_hi, nt, preferred_element_type=F32)
              + lax.dot_general(wrh_ref[...], h2_lo, nt, preferred_element_type=F32)
              + lax.dot_general(wrl_ref[...], h2_hi, nt, preferred_element_type=F32)
              + br_ref[...])
    row = lax.broadcasted_iota(jnp.int32, (N_EXPERTS, tm), 0)
    work = logits
    top_e, top_l = [], []
    for _ in range(TOP_K):
        mx = jnp.max(work, axis=0, keepdims=True)
        idx = jnp.min(jnp.where(work == mx, row, N_EXPERTS), axis=0, keepdims=True)
        top_e.append(idx)
        top_l.append(mx)
        work = jnp.where(row == idx, -jnp.inf, work)
    exps = [jnp.exp(v - top_l[0]) for v in top_l]
    denom = exps[0] + exps[1] + exps[2] + exps[3]

    onehots = [row == e for e in top_e]
    multi = jnp.zeros((N_EXPERTS, tm), F32)
    for oh in onehots:
        multi = multi + jnp.where(oh, 1.0, 0.0)
    r_i = lax.broadcasted_iota(jnp.int32, (tm, tm), 0)
    c_i = lax.broadcasted_iota(jnp.int32, (tm, tm), 1)
    earlier = jnp.where(r_i < c_i, 1.0, 0.0).astype(BF16)
    pos = run_sc[...] + jnp.dot(multi.astype(BF16), earlier, preferred_element_type=F32)
    run_sc[...] = run_sc[...] + jnp.sum(multi, axis=1, keepdims=True)
    cnt_ref[...] = run_sc[...]

    for k in range(TOP_K):
        tope_ref[k:k + 1, :] = top_e[k]
        topw_ref[k:k + 1, :] = exps[k] / denom
        rank_ref[k:k + 1, :] = jnp.sum(jnp.where(onehots[k], pos, 0.0), axis=0,
                                       keepdims=True).astype(jnp.int32)

    @pl.when(i >= 1)
    def _():
        for cp in h2_copies(i - 1, 1 - slot):
            cp.wait()

    @pl.when(i == pl.num_programs(0) - 1)
    def _():
        for cp in h2_copies(i, slot):
            cp.wait()


def _merge(proj, yb, x2, kvmem, ws, bsp, g_q_mem, wa, wb, wc, wo, g_ffn, w_router, b_router, S):
    T = x2.shape[0]
    tm = MERGE_TM
    per_b = S // tm
    full = lambda shape: pl.BlockSpec(shape, lambda i: (0,) * len(shape))
    in_specs = [
        pl.BlockSpec((tm, SPLIT), lambda i: (i, 0)),
        pl.BlockSpec((tm, SPLIT), lambda i: (i, 1)),
        pl.BlockSpec((tm, SPLIT), lambda i: (i, 0)),
        pl.BlockSpec((tm, SPLIT), lambda i: (i, 5)),
        pl.BlockSpec((tm, D_MODEL), lambda i: (i, 3)),
        pl.BlockSpec((tm, D_MODEL), lambda i: (i, 4)),
        pl.BlockSpec((tm, D_MODEL), lambda i: (i, 5)),
        pl.BlockSpec((tm, D_MODEL), lambda i: (i, 0)),
        pl.BlockSpec((MEM_LEN, 2 * MEM_HEADS * MEM_HEAD_DIM), lambda i: (i // per_b, 0)),
        full((SGU_GROUPS, SGU_CHUNK, SGU_CHUNK)),
        full((SGU_CHUNK, SGU_WIDTH)),
        full((1, MEM_HEAD_DIM)),
        full((SGU_WIDTH, D_MODEL)),
        full((SGU_WIDTH, D_MODEL)),
        full((SGU_WIDTH, D_MODEL)),
        full((D_MODEL, D_MODEL)),
        full((1, D_MODEL)),
        full((N_EXPERTS, D_MODEL)),
        full((N_EXPERTS, D_MODEL)),
        full((N_EXPERTS, 1)),
    ]
    out_shape = (
        jax.ShapeDtypeStruct((T, D_MODEL), F32),
        jax.ShapeDtypeStruct((T, D_MODEL // 2 // LANES, LANES), jnp.uint32),
        jax.ShapeDtypeStruct((TOP_K, T), jnp.int32),
        jax.ShapeDtypeStruct((TOP_K, T), F32),
        jax.ShapeDtypeStruct((TOP_K, T), jnp.int32),
        jax.ShapeDtypeStruct((N_EXPERTS, 1), F32),
    )
    out_specs = (
        pl.BlockSpec((tm, D_MODEL), lambda i: (i, 0)),
        pl.BlockSpec(memory_space=pl.ANY),
        pl.BlockSpec((TOP_K, tm), lambda i: (0, i)),
        pl.BlockSpec((TOP_K, tm), lambda i: (0, i)),
        pl.BlockSpec((TOP_K, tm), lambda i: (0, i)),
        pl.BlockSpec((N_EXPERTS, 1), lambda i: (0, 0)),
    )
    wr_t = w_router.T
    wr_hi = wr_t.astype(BF16)
    wr_lo = (wr_t - wr_hi.astype(F32)).astype(BF16)
    return pl.pallas_call(
        _merge_kernel,
        out_shape=out_shape,
        grid=(T // tm,),
        in_specs=in_specs,
        out_specs=out_specs,
        scratch_shapes=[pltpu.VMEM((N_EXPERTS, 1), F32),
                        pltpu.VMEM((2, tm, D_MODEL // 2), jnp.uint32), pltpu.SemaphoreType.DMA((2,))],
        compiler_params=pltpu.CompilerParams(
            dimension_semantics=("arbitrary",), vmem_limit_bytes=VMEM_LIMIT),
        name="merge",
    )(proj, proj, yb, proj, proj, proj, proj, x2, kvmem, ws, bsp, g_q_mem,
      wa, wb, wc, wo, g_ffn, wr_hi, wr_lo, b_router.reshape(N_EXPERTS, 1))


def _dispatch_kernel(padlo_ref, padlen_ref, dest_ref, h_hbm, xbuf_ref, sem, z_sc, z_sem, h_sc, in_sem):
    i = pl.program_id(0)
    tm = h_sc.shape[1]

    @pl.when(i == 0)
    def _zero_padding_rows():
        z_sc[...] = jnp.zeros(z_sc.shape, z_sc.dtype)
        bits = range(EXPERT_BM.bit_length() - 2, -1, -1)

        def pieces(e):
            off, length = padlo_ref[e], padlen_ref[e]
            for b in bits:
                take = ((length >> b) & 1) == 1
                yield take, pltpu.make_async_copy(z_sc.at[pl.ds(0, 1 << b)],
                                                  xbuf_ref.at[pl.ds(off, 1 << b)], z_sem)
                off = off + jnp.where(take, 1 << b, 0)

        for e in range(N_EXPERTS):
            for take, cp in pieces(e):
                pl.when(take)(cp.start)
        for e in range(N_EXPERTS):
            for take, cp in pieces(e):
                pl.when(take)(cp.wait)

        zrows = z_sc.shape[0]
        first = (padlo_ref[N_EXPERTS - 1] + padlen_ref[N_EXPERTS - 1]) // zrows
        def tail_copy(j):
            return pltpu.make_async_copy(z_sc, xbuf_ref.at[pl.ds(pl.multiple_of(j * zrows, zrows), zrows)], z_sem)
        lax.fori_loop(first, xbuf_ref.shape[0] // zrows, lambda j, c: (tail_copy(j).start(), c)[1], 0)
        lax.fori_loop(first, xbuf_ref.shape[0] // zrows, lambda j, c: (tail_copy(j).wait(), c)[1], 0)

    n_steps = pl.num_programs(0)
    nbuf = h_sc.shape[0]

    def fetch(step, s):
        rows = pl.ds(pl.multiple_of(step * tm, tm), tm)
        return pltpu.make_async_copy(h_hbm.at[rows], h_sc.at[s], in_sem.at[s])

    def drain(s):
        for _ in range(TOP_K):
            pltpu.make_async_copy(h_sc.at[s], xbuf_ref.at[pl.ds(0, tm)], sem.at[s]).wait()

    @pl.when(i == 0)
    def _():
        fetch(0, 0).start()

    @pl.when(i + 1 < n_steps)
    def _():
        fetch(i + 1, (i + 1) % nbuf).start()

    fetch(i, i % nbuf).wait()
    for s in range(nbuf):
        @pl.when(i % nbuf == s)
        def _():
            for r in range(tm):
                for k in range(TOP_K):
                    pltpu.make_async_copy(h_sc.at[s, r], xbuf_ref.at[dest_ref[r * TOP_K + k]],
                                          sem.at[s]).start(priority=k % 2)

            @pl.when(i >= 1)
            def _():
                drain((s - 1) % nbuf)

            @pl.when(i == n_steps - 1)
            def _():
                drain(s)


def _dispatch(pad_lo, pad_len, dest_flat, h2, n_rows):
    T, slabs, lanes = h2.shape
    tm = ROUTE_TM
    grid_spec = pltpu.PrefetchScalarGridSpec(
        num_scalar_prefetch=2,
        grid=(T // tm,),
        in_specs=[
            pl.BlockSpec((tm * TOP_K,), lambda i, lo, ln: (i,), memory_space=pltpu.SMEM),
            pl.BlockSpec(memory_space=pl.ANY),
        ],
        out_specs=pl.BlockSpec(memory_space=pl.ANY),
        scratch_shapes=[pltpu.SemaphoreType.DMA((3,)), pltpu.VMEM((EXPERT_BM // 2, slabs, lanes), h2.dtype),
                        pltpu.SemaphoreType.DMA, pltpu.VMEM((3, tm, slabs, lanes), h2.dtype),
                        pltpu.SemaphoreType.DMA((3,))],
    )
    return pl.pallas_call(
        _dispatch_kernel,
        out_shape=jax.ShapeDtypeStruct((n_rows, slabs, lanes), h2.dtype),
        grid_spec=grid_spec,
        compiler_params=pltpu.CompilerParams(
            dimension_semantics=("arbitrary",), vmem_limit_bytes=VMEM_LIMIT),
        name="dispatch",
    )(pad_lo, pad_len, dest_flat, h2)


FF_CHUNK = 512


def _expert_kernel(be_ref, nused_ref, x_hbm, w1_ref, b1_ref, w2_ref, b2_ref,
                   y_hbm, w1_sc, w2_sc, x_sc, x_sem, y_sc, y_sem):
    i = pl.program_id(0)
    bm = x_sc.shape[1]
    n_used = nused_ref[0]

    def x_copies(blk, slot):
        rows = pl.ds(pl.multiple_of(blk * bm, bm), bm)
        return [pltpu.make_async_copy(x_hbm.at[rows, c], x_sc.at[slot, :, c * LANES:(c + 1) * LANES],
                                      x_sem.at[slot]) for c in range(x_hbm.shape[1])]

    def y_copies(blk, slot):
        rows = pl.ds(pl.multiple_of(blk * bm, bm), bm)
        return [pltpu.make_async_copy(y_sc.at[slot, :, c * LANES:(c + 1) * LANES], y_hbm.at[rows, c],
                                      y_sem.at[slot]) for c in range(y_hbm.shape[1])]

    @pl.when(i < n_used)
    def _():
        slot = i % 2

        @pl.when(i == 0)
        def _():
            for cp in x_copies(0, 0):
                cp.start()

        @pl.when(i + 1 < n_used)
        def _():
            for cp in x_copies(i + 1, 1 - slot):
                cp.start()

        @pl.when(jnp.logical_or(i == 0, be_ref[i] != be_ref[jnp.maximum(i - 1, 0)]))
        def _():
            w1_sc[...] = w1_ref[0].astype(BF16)
            w2_sc[...] = w2_ref[0].astype(BF16)

        for cp in x_copies(i, slot):
            cp.wait()
        packed = x_sc[slot]
        lo = lax.bitcast_convert_type(packed << 16, F32)
        hi = lax.bitcast_convert_type(packed & jnp.uint32(0xFFFF0000), F32)
        xb = jnp.concatenate([lo, hi], axis=1).astype(BF16)
        y = b2_ref[0]
        for c in range(D_FF // FF_CHUNK):
            g_cols = slice(c * FF_CHUNK, (c + 1) * FF_CHUNK)
            l_cols = slice(D_FF + c * FF_CHUNK, D_FF + (c + 1) * FF_CHUNK)
            glu = jnp.dot(xb, w1_sc[:, g_cols], preferred_element_type=F32) + b1_ref[0, :, g_cols]
            lin = jnp.dot(xb, w1_sc[:, l_cols], preferred_element_type=F32) + b1_ref[0, :, l_cols]
            glu = jnp.minimum(glu, SWIGLU_LIMIT)
            lin = jnp.clip(lin, -SWIGLU_LIMIT, SWIGLU_LIMIT)
            act = glu * jax.nn.sigmoid(SWIGLU_ALPHA * glu) * (lin + 1.0)
            y = y + jnp.dot(act.astype(BF16), w2_sc[g_cols, :], preferred_element_type=F32)
        y_sc[slot] = y
        for cp in y_copies(i, slot):
            cp.start()

        @pl.when(i >= 1)
        def _():
            for cp in y_copies(i - 1, 1 - slot):
                cp.wait()

        @pl.when(i == n_used - 1)
        def _():
            for cp in y_copies(i, slot):
                cp.wait()

    @pl.when(i >= n_used)
    def _():
        y_sc[0] = jnp.zeros(y_sc.shape[1:], F32)
        for cp in y_copies(i, 0):
            cp.start()
        for cp in y_copies(i, 0):
            cp.wait()


def _experts(block_e, n_used, xbuf, w1, b1, w2, b2):
    n_rows, slabs, lanes = xbuf.shape
    bm = EXPERT_BM
    grid_spec = pltpu.PrefetchScalarGridSpec(
        num_scalar_prefetch=2,
        grid=(n_rows // bm,),
        in_specs=[
            pl.BlockSpec(memory_space=pl.ANY),
            pl.BlockSpec((1, D_MODEL, 2 * D_FF), lambda i, be, nu: (be[i], 0, 0)),
            pl.BlockSpec((1, 1, 2 * D_FF), lambda i, be, nu: (be[i], 0, 0)),
            pl.BlockSpec((1, D_FF, D_MODEL), lambda i, be, nu: (be[i], 0, 0)),
            pl.BlockSpec((1, 1, D_MODEL), lambda i, be, nu: (be[i], 0, 0)),
        ],
        out_specs=pl.BlockSpec(memory_space=pl.ANY),
        scratch_shapes=[pltpu.VMEM((D_MODEL, 2 * D_FF), BF16), pltpu.VMEM((D_FF, D_MODEL), BF16),
                        pltpu.VMEM((2, bm, slabs * lanes), jnp.uint32), pltpu.SemaphoreType.DMA((2,)),
                        pltpu.VMEM((2, bm, D_MODEL), F32), pltpu.SemaphoreType.DMA((2,))],
    )
    return pl.pallas_call(
        _expert_kernel,
        out_shape=jax.ShapeDtypeStruct((n_rows, D_MODEL // LANES, LANES), F32),
        grid_spec=grid_spec,
        compiler_params=pltpu.CompilerParams(
            dimension_semantics=("arbitrary",), vmem_limit_bytes=VMEM_LIMIT),
        name="experts",
    )(block_e, n_used, xbuf, w1, b1, w2, b2)


def _combine_kernel(dest_ref, nxt_ref, w_ref, x1_hbm, ybuf_hbm, out_hbm, g_sc, x1_sc, res_sc,
                    g_sem, x_sem, o_sem):
    i = pl.program_id(0)
    n_steps = pl.num_programs(0)
    tm = g_sc.shape[2]
    chunks = D_MODEL // LANES
    slot = i % 2

    def gather(idx_ref, s):
        for r in range(tm):
            for k in range(TOP_K):
                pltpu.make_async_copy(ybuf_hbm.at[idx_ref[r * TOP_K + k]], g_sc.at[s, k, r],
                                      g_sem.at[s]).start(priority=k % 2)

    def x1_copies(step, s):
        rows = pl.ds(pl.multiple_of(step * tm, tm), tm)
        return [pltpu.make_async_copy(x1_hbm.at[rows, c * LANES:(c + 1) * LANES], x1_sc.at[s, :, c, :],
                                      x_sem.at[s]) for c in range(chunks)]

    def out_copies(step, s):
        rows = pl.ds(pl.multiple_of(step * tm, tm), tm)
        return [pltpu.make_async_copy(res_sc.at[s, :, c, :], out_hbm.at[rows, c * LANES:(c + 1) * LANES],
                                      o_sem.at[s]) for c in range(chunks)]

    @pl.when(i == 0)
    def _():
        for cp in x1_copies(0, 0):
            cp.start()

    @pl.when(i + 1 < n_steps)
    def _():
        for cp in x1_copies(i + 1, 1 - slot):
            cp.start()

    @pl.when(i == 0)
    def _():
        gather(dest_ref, 0)

    for parity in range(2):
        @pl.when(jnp.logical_and(i + 1 < n_steps, slot == parity))
        def _():
            gather(nxt_ref, 1 - parity)

    for k in range(TOP_K):
        pltpu.make_async_copy(ybuf_hbm.at[pl.ds(0, tm)], g_sc.at[slot, k], g_sem.at[slot]).wait()
    for cp in x1_copies(i, slot):
        cp.wait()

    for parity in range(2):
        @pl.when(slot == parity)
        def _():
            for r in range(tm):
                acc = x1_sc[parity, r]
                for k in range(TOP_K):
                    acc = acc + w_ref[r * TOP_K + k] * g_sc[parity, k, r]
                res_sc[parity, r] = acc
    for cp in out_copies(i, slot):
        cp.start()

    @pl.when(i >= 1)
    def _():
        for cp in out_copies(i - 1, 1 - slot):
            cp.wait()

    @pl.when(i == n_steps - 1)
    def _():
        for cp in out_copies(i, slot):
            cp.wait()


def _combine(dest_flat, x1, w_flat, ybuf):
    T = x1.shape[0]
    tm = ROUTE_TM
    tile = (D_MODEL // LANES, LANES)
    return pl.pallas_call(
        _combine_kernel,
        out_shape=jax.ShapeDtypeStruct((T, D_MODEL), F32),
        grid=(T // tm,),
        in_specs=[
            pl.BlockSpec((tm * TOP_K,), lambda i: (i,), memory_space=pltpu.SMEM),
            pl.BlockSpec((tm * TOP_K,), lambda i: (jnp.minimum(i + 1, T // tm - 1),),
                         memory_space=pltpu.SMEM),
            pl.BlockSpec((tm * TOP_K,), lambda i: (i,), memory_space=pltpu.SMEM),
            pl.BlockSpec(memory_space=pl.ANY),
            pl.BlockSpec(memory_space=pl.ANY),
        ],
        out_specs=pl.BlockSpec(memory_space=pl.ANY),
        scratch_shapes=[pltpu.VMEM((2, TOP_K, tm) + tile, F32), pltpu.VMEM((2, tm) + tile, F32),
                        pltpu.VMEM((2, tm) + tile, F32), pltpu.SemaphoreType.DMA((2,)),
                        pltpu.SemaphoreType.DMA((2,)), pltpu.SemaphoreType.DMA((2,))],
        compiler_params=pltpu.CompilerParams(
            dimension_semantics=("arbitrary",), vmem_limit_bytes=VMEM_LIMIT),
        name="combine",
    )(dest_flat, dest_flat, w_flat, x1, ybuf)


def _layer(x2, mem2, B, S, g_mix, w_in, g_sgu_v, w_spatial, b_spatial, g_q_moba, g_k_moba,
           g_mem, w_mem_kv, g_q_mem, g_k_mem, w_br_sgu, w_br_moba, w_br_mem, w_out,
           g_ffn, w_router, b_router, w_gate_up, b_gate_up, w_down, b_down):
    T = B * S
    row = lambda v: v.reshape(1, -1)
    proj = _inproj(x2, row(g_mix), w_in.astype(BF16), row(g_sgu_v))
    yb = _moba(proj, g_q_moba, g_k_moba, B, S)
    kvmem = _memkv(mem2, row(g_mem), w_mem_kv.astype(BF16), row(g_k_mem), B)

    causal = jnp.tril(jnp.ones((SGU_CHUNK, SGU_CHUNK), dtype=bool))
    ws = jnp.where(causal[None], w_spatial, 0.0).astype(BF16)
    bsp = jnp.repeat(b_spatial.T, LANES, axis=1)
    x1, h2, top_e, top_w, rank, counts = _merge(
        proj, yb, x2, kvmem, ws, bsp, row(g_q_mem), w_br_sgu.astype(BF16),
        w_br_moba.astype(BF16), w_br_mem.astype(BF16), w_out.astype(BF16), row(g_ffn),
        w_router, b_router, S)

    bm = EXPERT_BM
    n_blocks = -(-(T * TOP_K + N_EXPERTS * (bm - 1)) // bm)
    cnt = counts.reshape(N_EXPERTS).astype(jnp.int32)
    padded = (cnt + bm - 1) // bm * bm
    pad_ends = jnp.cumsum(padded)
    pad_starts = pad_ends - padded
    e_ids = jnp.arange(N_EXPERTS, dtype=jnp.int32)[:, None, None]
    start_of = jnp.sum(jnp.where(top_e[None] == e_ids, pad_starts[:, None, None], 0), axis=0)
    dest = (start_of + rank).T.reshape(T * TOP_K)
    top_w = top_w.T.reshape(T * TOP_K)
    block_row0 = jnp.arange(n_blocks, dtype=jnp.int32) * bm
    block_e = jnp.minimum(
        jnp.sum((pad_ends[None, :] <= block_row0[:, None]).astype(jnp.int32), axis=1), N_EXPERTS - 1)
    n_used = (pad_ends[-1:] // bm).astype(jnp.int32)

    xbuf = _dispatch(pad_starts + cnt, padded - cnt, dest, h2, n_blocks * bm)
    ybuf = _experts(block_e, n_used, xbuf, w_gate_up, b_gate_up.reshape(N_EXPERTS, 1, 2 * D_FF),
                    w_down, b_down.reshape(N_EXPERTS, 1, D_MODEL))
    return _combine(dest, x1, top_w, ybuf)


def kernel(x, mem, g_mix, w_in, g_sgu_v, w_spatial, b_spatial, g_q_moba, g_k_moba, g_mem, w_mem_kv, g_q_mem, g_k_mem, w_br_sgu, w_br_moba, w_br_mem, w_out, g_ffn, w_router, b_router, w_gate_up, b_gate_up, w_down, b_down):
    B, S, D = x.shape
    x2 = x.reshape(B * S, D)
    mem2 = mem.reshape(B * MEM_LEN, D)
    params = (g_mix, w_in, g_sgu_v, w_spatial, b_spatial, g_q_moba, g_k_moba, g_mem, w_mem_kv,
              g_q_mem, g_k_mem, w_br_sgu, w_br_moba, w_br_mem, w_out, g_ffn, w_router, b_router,
              w_gate_up, b_gate_up, w_down, b_down)
    for l in range(g_mix.shape[0]):
        x2 = _layer(x2, mem2, B, S, *(p[l] for p in params))
    return x2.reshape(B, S, D)
```

```python
import functools

import numpy as np
import jax
import jax.numpy as jnp
from jax import lax
from jax.experimental import pallas as pl
from jax.experimental.pallas import tpu as pltpu

F32 = jnp.float32
BF16 = jnp.bfloat16

D_MODEL = 1024
MEM_LEN = 256
SGU_GROUPS = 4
SGU_CHUNK = 128
SGU_WIDTH = 512
MOBA_HEADS = 8
MOBA_HEAD_DIM = 64
MOBA_BLOCK = 256
MOBA_TOPK = 3
MEM_HEADS = 4
MEM_HEAD_DIM = 128
SPLIT = 512
IN_WIDTH = 6144
N_SPLITS = IN_WIDTH // SPLIT
N_EXPERTS = 32
TOP_K = 4
D_FF = 1024
SWIGLU_ALPHA = 1.702
SWIGLU_LIMIT = 7.0
EPS = 1e-6
NEG_INF = -1e30
LANES = 128

VMEM_LIMIT = 56 * 1024 * 1024

INPROJ_TM = 256
MERGE_TM = 512
ROUTE_TM = 256
EXPERT_BM = 512


def _gelu(x):
    return 0.5 * x * (1.0 + lax.erf(x * np.float32(np.sqrt(0.5))))


def _rms(x, g):
    return x * lax.rsqrt(jnp.mean(x * x, axis=-1, keepdims=True) + EPS) * g


def _inproj_kernel(x_ref, gmix_ref, w_ref, gsgu_ref, o_ref):
    h = _rms(x_ref[...], gmix_ref[...]).astype(BF16)
    for c in range(N_SPLITS):
        sl = slice(c * SPLIT, (c + 1) * SPLIT)
        acc = jnp.dot(h, w_ref[:, sl], preferred_element_type=F32)
        if c == 0:
            val = _gelu(acc)
        elif c == 1:
            val = _rms(_gelu(acc), gsgu_ref[...])
        elif c >= 6:
            val = jax.nn.sigmoid(acc)
        else:
            val = acc
        o_ref[:, sl] = val.astype(BF16)


def _inproj(x2, g_mix, w_in, g_sgu_v):
    T = x2.shape[0]
    tm = INPROJ_TM
    return pl.pallas_call(
        _inproj_kernel,
        out_shape=jax.ShapeDtypeStruct((T, IN_WIDTH), BF16),
        grid=(T // tm,),
        in_specs=[
            pl.BlockSpec((tm, D_MODEL), lambda i: (i, 0)),
            pl.BlockSpec((1, D_MODEL), lambda i: (0, 0)),
            pl.BlockSpec((D_MODEL, IN_WIDTH), lambda i: (0, 0)),
            pl.BlockSpec((1, SGU_WIDTH), lambda i: (0, 0)),
        ],
        out_specs=pl.BlockSpec((tm, IN_WIDTH), lambda i: (i, 0)),
        compiler_params=pltpu.CompilerParams(
            dimension_semantics=("arbitrary",), vmem_limit_bytes=VMEM_LIMIT),
        name="inproj",
    )(x2, g_mix, w_in, g_sgu_v)


def _pair_norm(x, g, head0):
    sq = x * x
    s0 = jnp.sum(jnp.where(head0, sq, 0.0), axis=-1, keepdims=True)
    s1 = jnp.sum(jnp.where(head0, 0.0, sq), axis=-1, keepdims=True)
    inv = jnp.where(head0, lax.rsqrt(s0 * (1.0 / MOBA_HEAD_DIM) + EPS),
                    lax.rsqrt(s1 * (1.0 / MOBA_HEAD_DIM) + EPS))
    return x * inv * g


VT_ROWS = MOBA_HEAD_DIM + 16
LOG2E = float(np.log2(np.e))


def _moba_kernel(slopes_ref, q_ref, k_ref, v_ref, gq_ref, gk_ref, bias_ref, o_ref,
                 kn_sc, vt_sc, kmean_sc, mrow_sc, qh_sc, m_sc, acc_sc, t_sc, cmax_sc, *, nb):
    qi = pl.program_id(1)
    bs = MOBA_BLOCK
    pairs = MOBA_HEADS // 2
    head0 = lax.broadcasted_iota(jnp.int32, (1, LANES), 1) < MOBA_HEAD_DIM
    nt = (((1,), (1,)), ((), ()))

    @pl.when(qi == 0)
    def _prepare_kv():
        ones_rows = jnp.where(
            lax.broadcasted_iota(jnp.int32, (VT_ROWS - MOBA_HEAD_DIM, bs), 0) == 0, 1.0, 0.0).astype(BF16)

        def body(j, carry):
            rows = pl.ds(pl.multiple_of(j * bs, bs), bs)
            means = []
            for g in range(pairs):
                cols = slice(g * LANES, (g + 1) * LANES)
                kn = _pair_norm(k_ref[rows, cols].astype(F32), gk_ref[...], head0)
                kn_sc[j, :, cols] = kn.astype(BF16)
                means.append(jnp.mean(kn, axis=0, keepdims=True))
                vt = v_ref[rows, cols].astype(F32).T.astype(BF16)
                for hh in range(2):
                    h = 2 * g + hh
                    vt_sc[j, h, :MOBA_HEAD_DIM, :] = vt[hh * MOBA_HEAD_DIM:(hh + 1) * MOBA_HEAD_DIM]
                    vt_sc[j, h, MOBA_HEAD_DIM:, :] = ones_rows
            kmean_sc[pl.ds(j, 1), :] = jnp.concatenate(means, axis=1)
            return carry
        lax.fori_loop(0, nb, body, 0)

    blk = lax.broadcasted_iota(jnp.int32, (nb, 1), 0)
    eligible = blk < qi
    dist = ((qi - blk) * bs).astype(F32)
    for g in range(pairs):
        cols = slice(g * LANES, (g + 1) * LANES)
        qn = _pair_norm(q_ref[:, cols].astype(F32), gq_ref[...], head0)
        qpair = jnp.concatenate([jnp.where(head0, qn, 0.0), jnp.where(head0, 0.0, qn)], axis=0)
        qh_sc[g] = (qpair * (LOG2E * MOBA_HEAD_DIM ** -0.5)).astype(BF16)
        q_hi = qpair.astype(BF16)
        q_lo = (qpair - q_hi.astype(F32)).astype(BF16)
        km = kmean_sc[:, cols]
        km_hi = km.astype(BF16)
        km_lo = (km - km_hi.astype(F32)).astype(BF16)
        gate = (lax.dot_general(km_hi, q_hi, nt, preferred_element_type=F32)
                + lax.dot_general(km_hi, q_lo, nt, preferred_element_type=F32)
                + lax.dot_general(km_lo, q_hi, nt, preferred_element_type=F32))
        work = jnp.where(eligible, gate, -jnp.inf)
        sel = jnp.zeros((nb, 2 * bs), jnp.bool_)
        for _ in range(MOBA_TOPK):
            best = jnp.max(work, axis=0, keepdims=True)
            first = jnp.min(jnp.where(work == best, blk, nb), axis=0, keepdims=True)
            pick = jnp.logical_and(blk == first, best > -jnp.inf)
            sel = jnp.logical_or(sel, pick)
            work = jnp.where(pick, -jnp.inf, work)
        for hh in range(2):
            h = 2 * g + hh
            mrow_sc[:, h * bs:(h + 1) * bs] = jnp.where(
                blk == qi, 0.0,
                jnp.where(sel[:, hh * bs:(hh + 1) * bs], (-LOG2E * slopes_ref[h]) * dist, NEG_INF))
    m_sc[...] = jnp.full(m_sc.shape, NEG_INF, F32)
    acc_sc[...] = jnp.zeros(acc_sc.shape, F32)

    def score(j, own, slot):
        scores = [lax.dot_general(kn_sc[j, :, g * LANES:(g + 1) * LANES], qh_sc[g], nt,
                                  preferred_element_type=F32) for g in range(pairs)]
        t = jnp.concatenate(scores, axis=1) + bias_ref[1 if own else 0]
        t_sc[slot] = t
        cmax_sc[slot] = jnp.max(t, axis=0, keepdims=True)

    def accumulate(j, slot):
        mrow = mrow_sc[pl.ds(j, 1), :]
        m_old = m_sc[...]
        m_new = jnp.maximum(m_old, cmax_sc[slot] + mrow)
        shift = m_new - mrow
        alpha = jnp.exp2(m_old - m_new)
        m_sc[...] = m_new
        for h in range(MOBA_HEADS):
            cols = slice(h * bs, (h + 1) * bs)
            p = jnp.exp2(t_sc[slot, :, cols] - shift[:, cols]).astype(BF16)
            pv = jnp.dot(vt_sc[j, h], p, preferred_element_type=F32)
            acc_sc[h] = alpha[:, cols] * acc_sc[h] + pv

    n_past = qi
    score(qi, True, 0)

    def two_blocks(p, carry):
        score(2 * p, False, 1)
        accumulate(jnp.where(p == 0, qi, 2 * p - 1), 0)
        score(2 * p + 1, False, 0)
        accumulate(2 * p, 1)
        return carry
    lax.fori_loop(0, n_past // 2, two_blocks, 0)

    @pl.when(n_past % 2 == 0)
    def _():
        accumulate(jnp.where(n_past == 0, qi, n_past - 1), 0)

    @pl.when(n_past % 2 == 1)
    def _():
        score(n_past - 1, False, 1)
        accumulate(jnp.where(n_past == 1, qi, n_past - 2), 0)
        accumulate(n_past - 1, 1)

    for g in range(pairs):
        halves = []
        for hh in range(2):
            acc = acc_sc[2 * g + hh]
            halves.append(acc[:MOBA_HEAD_DIM] / acc[MOBA_HEAD_DIM:MOBA_HEAD_DIM + 1])
        out_t = jnp.concatenate(halves, axis=0)
        o_ref[:, g * LANES:(g + 1) * LANES] = out_t.T.astype(BF16)


def _moba_bias_tiles():
    slopes = 2.0 ** (-8.0 * np.arange(1, MOBA_HEADS + 1, dtype=np.float32) / MOBA_HEADS)
    kk = np.arange(MOBA_BLOCK)[:, None]
    qq = np.arange(MOBA_BLOCK)[None, :]
    rel = (qq - kk).astype(np.float64)
    past = -slopes.astype(np.float64)[:, None, None] * rel[None] * LOG2E
    own = np.where((kk <= qq)[None], past, NEG_INF)
    side_by_side = lambda a: a.transpose(1, 0, 2).reshape(MOBA_BLOCK, MOBA_HEADS * MOBA_BLOCK)
    return slopes.astype(np.float32), np.stack([side_by_side(past), side_by_side(own)]).astype(np.float32)


def _moba(proj, g_q, g_k, B, S):
    nb = S // MOBA_BLOCK
    width = MOBA_HEADS * MOBA_HEAD_DIM
    slopes, bias = _moba_bias_tiles()
    gq2 = jnp.tile(g_q.reshape(1, MOBA_HEAD_DIM), (1, 2))
    gk2 = jnp.tile(g_k.reshape(1, MOBA_HEAD_DIM), (1, 2))
    full = lambda shape: pl.BlockSpec(shape, lambda b, q, s: (0,) * len(shape))
    grid_spec = pltpu.PrefetchScalarGridSpec(
        num_scalar_prefetch=1,
        grid=(B, nb),
        in_specs=[
            pl.BlockSpec((MOBA_BLOCK, width), lambda b, q, s: (b * nb + q, 2)),
            pl.BlockSpec((S, width), lambda b, q, s: (b, 3)),
            pl.BlockSpec((S, width), lambda b, q, s: (b, 4)),
            full((1, LANES)),
            full((1, LANES)),
            full((2, MOBA_BLOCK, MOBA_HEADS * MOBA_BLOCK)),
        ],
        out_specs=pl.BlockSpec((MOBA_BLOCK, width), lambda b, q, s: (b * nb + q, 0)),
        scratch_shapes=[
            pltpu.VMEM((nb, MOBA_BLOCK, width), BF16),
            pltpu.VMEM((nb, MOBA_HEADS, VT_ROWS, MOBA_BLOCK), BF16),
            pltpu.VMEM((nb, width), F32),
            pltpu.VMEM((nb, MOBA_HEADS * MOBA_BLOCK), F32),
            pltpu.VMEM((MOBA_HEADS // 2, 2 * MOBA_BLOCK, LANES), BF16),
            pltpu.VMEM((1, MOBA_HEADS * MOBA_BLOCK), F32),
            pltpu.VMEM((MOBA_HEADS, VT_ROWS, MOBA_BLOCK), F32),
            pltpu.VMEM((2, MOBA_BLOCK, MOBA_HEADS * MOBA_BLOCK), F32),
            pltpu.VMEM((2, 1, MOBA_HEADS * MOBA_BLOCK), F32),
        ],
    )
    return pl.pallas_call(
        functools.partial(_moba_kernel, nb=nb),
        out_shape=jax.ShapeDtypeStruct((B * S, width), BF16),
        grid_spec=grid_spec,
        compiler_params=pltpu.CompilerParams(
            dimension_semantics=("arbitrary", "arbitrary"), vmem_limit_bytes=VMEM_LIMIT),
        name="moba",
    )(jnp.asarray(slopes), proj, proj, proj, gq2, gk2, jnp.asarray(bias))


def _memkv_kernel(mem_ref, gmem_ref, w_ref, gk_ref, o_ref):
    mn = _rms(mem_ref[...], gmem_ref[...]).astype(BF16)
    kv = jnp.dot(mn, w_ref[...], preferred_element_type=F32)
    width = MEM_HEADS * MEM_HEAD_DIM
    for h in range(MEM_HEADS):
        sl = slice(h * MEM_HEAD_DIM, (h + 1) * MEM_HEAD_DIM)
        o_ref[:, sl] = _rms(kv[:, sl], gk_ref[...]).astype(BF16)
    o_ref[:, width:] = kv[:, width:].astype(BF16)


def _memkv(mem2, g_mem, w_mem_kv, g_k_mem, B):
    width = MEM_HEADS * MEM_HEAD_DIM
    return pl.pallas_call(
        _memkv_kernel,
        out_shape=jax.ShapeDtypeStruct((B * MEM_LEN, 2 * width), BF16),
        grid=(B,),
        in_specs=[
            pl.BlockSpec((MEM_LEN, D_MODEL), lambda b: (b, 0)),
            pl.BlockSpec((1, D_MODEL), lambda b: (0, 0)),
            pl.BlockSpec((D_MODEL, 2 * width), lambda b: (0, 0)),
            pl.BlockSpec((1, MEM_HEAD_DIM), lambda b: (0, 0)),
        ],
        out_specs=pl.BlockSpec((MEM_LEN, 2 * width), lambda b: (b, 0)),
        compiler_params=pltpu.CompilerParams(
            dimension_semantics=("arbitrary",), vmem_limit_bytes=VMEM_LIMIT),
        name="memkv",
    )(mem2, g_mem, w_mem_kv, g_k_mem)


def _merge_kernel(gu_ref, vn_ref, yb_ref, qm_ref, ga_ref, gb_ref, gc_ref, x_ref, kv_ref,
                  ws_ref, bsp_ref, gqm_ref, wa_ref, wb_ref, wc_ref, wo_ref, gffn_ref,
                  wrh_ref, wrl_ref, br_ref,
                  x1_ref, h2_hbm, tope_ref, topw_ref, rank_ref, cnt_ref, run_sc, hp_sc, hp_sem):
    i = pl.program_id(0)
    tm = x_ref.shape[0]

    @pl.when(i == 0)
    def _():
        run_sc[...] = jnp.zeros_like(run_sc)

    ya_chunks = []
    for c in range(tm // SGU_CHUNK):
        rows = slice(c * SGU_CHUNK, (c + 1) * SGU_CHUNK)
        groups = []
        for g in range(SGU_GROUPS):
            cols = slice(g * LANES, (g + 1) * LANES)
            groups.append(jnp.dot(ws_ref[g], vn_ref[rows, cols], preferred_element_type=F32))
        mixed = jnp.concatenate(groups, axis=1) + bsp_ref[...]
        ya_chunks.append(gu_ref[rows, :].astype(F32) * mixed)
    y_a = jnp.concatenate(ya_chunks, axis=0).astype(BF16)

    width = MEM_HEADS * MEM_HEAD_DIM
    yc_heads = []
    for h in range(MEM_HEADS):
        cols = slice(h * MEM_HEAD_DIM, (h + 1) * MEM_HEAD_DIM)
        qn = _rms(qm_ref[:, cols].astype(F32), gqm_ref[...]).astype(BF16)
        s = lax.dot_general(qn, kv_ref[:, cols], (((1,), (1,)), ((), ())),
                            preferred_element_type=F32) * (MEM_HEAD_DIM ** -0.5)
        s = s - jnp.max(s, axis=-1, keepdims=True)
        p = jnp.exp(s)
        p = p / jnp.sum(p, axis=-1, keepdims=True)
        vcols = slice(width + h * MEM_HEAD_DIM, width + (h + 1) * MEM_HEAD_DIM)
        yc_heads.append(jnp.dot(p.astype(BF16), kv_ref[:, vcols], preferred_element_type=F32))
    y_c = jnp.concatenate(yc_heads, axis=1).astype(BF16)

    merged = ga_ref[...].astype(F32) * jnp.dot(y_a, wa_ref[...], preferred_element_type=F32)
    merged = merged + gb_ref[...].astype(F32) * jnp.dot(yb_ref[...], wb_ref[...],
                                                       preferred_element_type=F32)
    merged = merged + gc_ref[...].astype(F32) * jnp.dot(y_c, wc_ref[...],
                                                       preferred_element_type=F32)
    x1 = x_ref[...] + jnp.dot(merged.astype(BF16), wo_ref[...], preferred_element_type=F32)
    x1_ref[...] = x1
    h2 = _rms(x1, gffn_ref[...])
    h2_hi = h2.astype(BF16)
    h2_hi32 = h2_hi.astype(F32)
    h2_lo = (h2 - h2_hi32).astype(BF16)
    bits = lax.bitcast_convert_type(h2_hi32, jnp.uint32)
    half = D_MODEL // 2
    slot = i % 2
    hp_sc[slot] = (bits[:, :half] >> 16) | (bits[:, half:] & jnp.uint32(0xFFFF0000))

    def h2_copies(step, s):
        rows = pl.ds(pl.multiple_of(step * tm, tm), tm)
        return [pltpu.make_async_copy(hp_sc.at[s, :, c * LANES:(c + 1) * LANES], h2_hbm.at[rows, c],
                                      hp_sem.at[s]) for c in range(half // LANES)]

    for cp in h2_copies(i, slot):
        cp.start()

    nt = (((1,), (1,)), ((), ()))
    logits = (lax.dot_general(wrh_ref[...], h2_hi, nt, preferred_element_type=F32)
              + lax.dot_general(wrh_ref[...], h2_lo, nt, preferred_element_type=F32)
              + lax.dot_general(wrl_ref[...], h2_hi, nt, preferred_element_type=F32)
              + br_ref[...])
    row = lax.broadcasted_iota(jnp.int32, (N_EXPERTS, tm), 0)
    work = logits
    top_e, top_l = [], []
    for _ in range(TOP_K):
        mx = jnp.max(work, axis=0, keepdims=True)
        idx = jnp.min(jnp.where(work == mx, row, N_EXPERTS), axis=0, keepdims=True)
        top_e.append(idx)
        top_l.append(mx)
        work = jnp.where(row == idx, -jnp.inf, work)
    exps = [jnp.exp(v - top_l[0]) for v in top_l]
    denom = exps[0] + exps[1] + exps[2] + exps[3]

    onehots = [row == e for e in top_e]
    multi = jnp.zeros((N_EXPERTS, tm), F32)
    for oh in onehots:
        multi = multi + jnp.where(oh, 1.0, 0.0)
    r_i = lax.broadcasted_iota(jnp.int32, (tm, tm), 0)
    c_i = lax.broadcasted_iota(jnp.int32, (tm, tm), 1)
    earlier = jnp.where(r_i < c_i, 1.0, 0.0).astype(BF16)
    pos = run_sc[...] + jnp.dot(multi.astype(BF16), earlier, preferred_element_type=F32)
    run_sc[...] = run_sc[...] + jnp.sum(multi, axis=1, keepdims=True)
    cnt_ref[...] = run_sc[...]

    for k in range(TOP_K):
        tope_ref[k:k + 1, :] = top_e[k]
        topw_ref[k:k + 1, :] = exps[k] / denom
        rank_ref[k:k + 1, :] = jnp.sum(jnp.where(onehots[k], pos, 0.0), axis=0,
                                       keepdims=True).astype(jnp.int32)

    @pl.when(i >= 1)
    def _():
        for cp in h2_copies(i - 1, 1 - slot):
            cp.wait()

    @pl.when(i == pl.num_programs(0) - 1)
    def _():
        for cp in h2_copies(i, slot):
            cp.wait()


def _merge(proj, yb, x2, kvmem, ws, bsp, g_q_mem, wa, wb, wc, wo, g_ffn, w_router, b_router, S):
    T = x2.shape[0]
    tm = MERGE_TM
    per_b = S // tm
    full = lambda shape: pl.BlockSpec(shape, lambda i: (0,) * len(shape))
    in_specs = [
        pl.BlockSpec((tm, SPLIT), lambda i: (i, 0)),
        pl.BlockSpec((tm, SPLIT), lambda i: (i, 1)),
        pl.BlockSpec((tm, SPLIT), lambda i: (i, 0)),
        pl.BlockSpec((tm, SPLIT), lambda i: (i, 5)),
        pl.BlockSpec((tm, D_MODEL), lambda i: (i, 3)),
        pl.BlockSpec((tm, D_MODEL), lambda i: (i, 4)),
        pl.BlockSpec((tm, D_MODEL), lambda i: (i, 5)),
        pl.BlockSpec((tm, D_MODEL), lambda i: (i, 0)),
        pl.BlockSpec((MEM_LEN, 2 * MEM_HEADS * MEM_HEAD_DIM), lambda i: (i // per_b, 0)),
        full((SGU_GROUPS, SGU_CHUNK, SGU_CHUNK)),
        full((SGU_CHUNK, SGU_WIDTH)),
        full((1, MEM_HEAD_DIM)),
        full((SGU_WIDTH, D_MODEL)),
        full((SGU_WIDTH, D_MODEL)),
        full((SGU_WIDTH, D_MODEL)),
        full((D_MODEL, D_MODEL)),
        full((1, D_MODEL)),
        full((N_EXPERTS, D_MODEL)),
        full((N_EXPERTS, D_MODEL)),
        full((N_EXPERTS, 1)),
    ]
    out_shape = (
        jax.ShapeDtypeStruct((T, D_MODEL), F32),
        jax.ShapeDtypeStruct((T, D_MODEL // 2 // LANES, LANES), jnp.uint32),
        jax.ShapeDtypeStruct((TOP_K, T), jnp.int32),
        jax.ShapeDtypeStruct((TOP_K, T), F32),
        jax.ShapeDtypeStruct((TOP_K, T), jnp.int32),
        jax.ShapeDtypeStruct((N_EXPERTS, 1), F32),
    )
    out_specs = (
        pl.BlockSpec((tm, D_MODEL), lambda i: (i, 0)),
        pl.BlockSpec(memory_space=pl.ANY),
        pl.BlockSpec((TOP_K, tm), lambda i: (0, i)),
        pl.BlockSpec((TOP_K, tm), lambda i: (0, i)),
        pl.BlockSpec((TOP_K, tm), lambda i: (0, i)),
        pl.BlockSpec((N_EXPERTS, 1), lambda i: (0, 0)),
    )
    wr_t = w_router.T
    wr_hi = wr_t.astype(BF16)
    wr_lo = (wr_t - wr_hi.astype(F32)).astype(BF16)
    return pl.pallas_call(
        _merge_kernel,
        out_shape=out_shape,
        grid=(T // tm,),
        in_specs=in_specs,
        out_specs=out_specs,
        scratch_shapes=[pltpu.VMEM((N_EXPERTS, 1), F32),
                        pltpu.VMEM((2, tm, D_MODEL // 2), jnp.uint32), pltpu.SemaphoreType.DMA((2,))],
        compiler_params=pltpu.CompilerParams(
            dimension_semantics=("arbitrary",), vmem_limit_bytes=VMEM_LIMIT),
        name="merge",
    )(proj, proj, yb, proj, proj, proj, proj, x2, kvmem, ws, bsp, g_q_mem,
      wa, wb, wc, wo, g_ffn, wr_hi, wr_lo, b_router.reshape(N_EXPERTS, 1))


def _dispatch_kernel(padlo_ref, padlen_ref, dest_ref, h_hbm, xbuf_ref, sem, z_sc, z_sem, h_sc, in_sem):
    i = pl.program_id(0)
    tm = h_sc.shape[1]

    @pl.when(i == 0)
    def _zero_padding_rows():
        z_sc[...] = jnp.zeros(z_sc.shape, z_sc.dtype)
        bits = range(EXPERT_BM.bit_length() - 2, -1, -1)

        def pieces(e):
            off, length = padlo_ref[e], padlen_ref[e]
            for b in bits:
                take = ((length >> b) & 1) == 1
                yield take, pltpu.make_async_copy(z_sc.at[pl.ds(0, 1 << b)],
                                                  xbuf_ref.at[pl.ds(off, 1 << b)], z_sem)
                off = off + jnp.where(take, 1 << b, 0)

        for e in range(N_EXPERTS):
            for take, cp in pieces(e):
                pl.when(take)(cp.start)
        for e in range(N_EXPERTS):
            for take, cp in pieces(e):
                pl.when(take)(cp.wait)

        zrows = z_sc.shape[0]
        first = (padlo_ref[N_EXPERTS - 1] + padlen_ref[N_EXPERTS - 1]) // zrows
        def tail_copy(j):
            return pltpu.make_async_copy(z_sc, xbuf_ref.at[pl.ds(pl.multiple_of(j * zrows, zrows), zrows)], z_sem)
        lax.fori_loop(first, xbuf_ref.shape[0] // zrows, lambda j, c: (tail_copy(j).start(), c)[1], 0)
        lax.fori_loop(first, xbuf_ref.shape[0] // zrows, lambda j, c: (tail_copy(j).wait(), c)[1], 0)

    n_steps = pl.num_programs(0)
    nbuf = h_sc.shape[0]

    def fetch(step, s):
        rows = pl.ds(pl.multiple_of(step * tm, tm), tm)
        return pltpu.make_async_copy(h_hbm.at[rows], h_sc.at[s], in_sem.at[s])

    def drain(s):
        for _ in range(TOP_K):
            pltpu.make_async_copy(h_sc.at[s], xbuf_ref.at[pl.ds(0, tm)], sem.at[s]).wait()

    @pl.when(i == 0)
    def _():
        fetch(0, 0).start()

    @pl.when(i + 1 < n_steps)
    def _():
        fetch(i + 1, (i + 1) % nbuf).start()

    fetch(i, i % nbuf).wait()
    for s in range(nbuf):
        @pl.when(i % nbuf == s)
        def _():
            for r in range(tm):
                for k in range(TOP_K):
                    pltpu.make_async_copy(h_sc.at[s, r], xbuf_ref.at[dest_ref[k * tm + r]],
                                          sem.at[s]).start(priority=k % 2)

            @pl.when(i >= 1)
            def _():
                drain((s - 1) % nbuf)

            @pl.when(i == n_steps - 1)
            def _():
                drain(s)


def _dispatch(pad_lo, pad_len, dest_flat, h2, n_rows):
    T, slabs, lanes = h2.shape
    tm = ROUTE_TM
    grid_spec = pltpu.PrefetchScalarGridSpec(
        num_scalar_prefetch=2,
        grid=(T // tm,),
        in_specs=[
            pl.BlockSpec((tm * TOP_K,), lambda i, lo, ln: (i,), memory_space=pltpu.SMEM),
            pl.BlockSpec(memory_space=pl.ANY),
        ],
        out_specs=pl.BlockSpec(memory_space=pl.ANY),
        scratch_shapes=[pltpu.SemaphoreType.DMA((3,)), pltpu.VMEM((EXPERT_BM // 2, slabs, lanes), h2.dtype),
                        pltpu.SemaphoreType.DMA, pltpu.VMEM((3, tm, slabs, lanes), h2.dtype),
                        pltpu.SemaphoreType.DMA((3,))],
    )
    return pl.pallas_call(
        _dispatch_kernel,
        out_shape=jax.ShapeDtypeStruct((n_rows, slabs, lanes), h2.dtype),
        grid_spec=grid_spec,
        compiler_params=pltpu.CompilerParams(
            dimension_semantics=("arbitrary",), vmem_limit_bytes=VMEM_LIMIT),
        name="dispatch",
    )(pad_lo, pad_len, dest_flat, h2)


FF_CHUNK = 512


def _expert_kernel(be_ref, nused_ref, x_hbm, w1_ref, b1_ref, w2_ref, b2_ref,
                   y_hbm, w1_sc, w2_sc, x_sc, x_sem, y_sc, y_sem):
    i = pl.program_id(0)
    bm = x_sc.shape[1]
    n_used = nused_ref[0]

    def x_copies(blk, slot):
        rows = pl.ds(pl.multiple_of(blk * bm, bm), bm)
        return [pltpu.make_async_copy(x_hbm.at[rows, c], x_sc.at[slot, :, c * LANES:(c + 1) * LANES],
                                      x_sem.at[slot]) for c in range(x_hbm.shape[1])]

    def y_copies(blk, slot):
        rows = pl.ds(pl.multiple_of(blk * bm, bm), bm)
        return [pltpu.make_async_copy(y_sc.at[slot, :, c * LANES:(c + 1) * LANES], y_hbm.at[rows, c],
                                      y_sem.at[slot]) for c in range(y_hbm.shape[1])]

    @pl.when(i < n_used)
    def _():
        slot = i % 2

        @pl.when(i == 0)
        def _():
            for cp in x_copies(0, 0):
                cp.start()

        @pl.when(i + 1 < n_used)
        def _():
            for cp in x_copies(i + 1, 1 - slot):
                cp.start()

        @pl.when(jnp.logical_or(i == 0, be_ref[i] != be_ref[jnp.maximum(i - 1, 0)]))
        def _():
            w1_sc[...] = w1_ref[0].astype(BF16)
            w2_sc[...] = w2_ref[0].astype(BF16)

        for cp in x_copies(i, slot):
            cp.wait()
        packed = x_sc[slot]
        lo = lax.bitcast_convert_type(packed << 16, F32)
        hi = lax.bitcast_convert_type(packed & jnp.uint32(0xFFFF0000), F32)
        xb = jnp.concatenate([lo, hi], axis=1).astype(BF16)
        y = b2_ref[0]
        for c in range(D_FF // FF_CHUNK):
            g_cols = slice(c * FF_CHUNK, (c + 1) * FF_CHUNK)
            l_cols = slice(D_FF + c * FF_CHUNK, D_FF + (c + 1) * FF_CHUNK)
            glu = jnp.dot(xb, w1_sc[:, g_cols], preferred_element_type=F32) + b1_ref[0, :, g_cols]
            lin = jnp.dot(xb, w1_sc[:, l_cols], preferred_element_type=F32) + b1_ref[0, :, l_cols]
            glu = jnp.minimum(glu, SWIGLU_LIMIT)
            lin = jnp.clip(lin, -SWIGLU_LIMIT, SWIGLU_LIMIT)
            act = glu * jax.nn.sigmoid(SWIGLU_ALPHA * glu) * (lin + 1.0)
            y = y + jnp.dot(act.astype(BF16), w2_sc[g_cols, :], preferred_element_type=F32)
        y_sc[slot] = y
        for cp in y_copies(i, slot):
            cp.start()

        @pl.when(i >= 1)
        def _():
            for cp in y_copies(i - 1, 1 - slot):
                cp.wait()

        @pl.when(i == n_used - 1)
        def _():
            for cp in y_copies(i, slot):
                cp.wait()

    @pl.when(i >= n_used)
    def _():
        y_sc[0] = jnp.zeros(y_sc.shape[1:], F32)
        for cp in y_copies(i, 0):
            cp.start()
        for cp in y_copies(i, 0):
            cp.wait()


def _experts(block_e, n_used, xbuf, w1, b1, w2, b2):
    n_rows, slabs, lanes = xbuf.shape
    bm = EXPERT_BM
    grid_spec = pltpu.PrefetchScalarGridSpec(
        num_scalar_prefetch=2,
        grid=(n_rows // bm,),
        in_specs=[
            pl.BlockSpec(memory_space=pl.ANY),
            pl.BlockSpec((1, D_MODEL, 2 * D_FF), lambda i, be, nu: (be[i], 0, 0)),
            pl.BlockSpec((1, 1, 2 * D_FF), lambda i, be, nu: (be[i], 0, 0)),
            pl.BlockSpec((1, D_FF, D_MODEL), lambda i, be, nu: (be[i], 0, 0)),
            pl.BlockSpec((1, 1, D_MODEL), lambda i, be, nu: (be[i], 0, 0)),
        ],
        out_specs=pl.BlockSpec(memory_space=pl.ANY),
        scratch_shapes=[pltpu.VMEM((D_MODEL, 2 * D_FF), BF16), pltpu.VMEM((D_FF, D_MODEL), BF16),
                        pltpu.VMEM((2, bm, slabs * lanes), jnp.uint32), pltpu.SemaphoreType.DMA((2,)),
                        pltpu.VMEM((2, bm, D_MODEL), F32), pltpu.SemaphoreType.DMA((2,))],
    )
    return pl.pallas_call(
        _expert_kernel,
        out_shape=jax.ShapeDtypeStruct((n_rows, D_MODEL // LANES, LANES), F32),
        grid_spec=grid_spec,
        compiler_params=pltpu.CompilerParams(
            dimension_semantics=("arbitrary",), vmem_limit_bytes=VMEM_LIMIT),
        name="experts",
    )(block_e, n_used, xbuf, w1, b1, w2, b2)


def _combine_kernel(dest_ref, nxt_ref, w_ref, x1_hbm, ybuf_hbm, out_hbm, g_sc, x1_sc, res_sc,
                    g_sem, x_sem, o_sem):
    i = pl.program_id(0)
    n_steps = pl.num_programs(0)
    tm = g_sc.shape[2]
    chunks = D_MODEL // LANES
    slot = i % 2

    def gather(idx_ref, s):
        for r in range(tm):
            for k in range(TOP_K):
                pltpu.make_async_copy(ybuf_hbm.at[idx_ref[k * tm + r]], g_sc.at[s, k, r],
                                      g_sem.at[s]).start(priority=k % 2)

    def x1_copies(step, s):
        rows = pl.ds(pl.multiple_of(step * tm, tm), tm)
        return [pltpu.make_async_copy(x1_hbm.at[rows, c * LANES:(c + 1) * LANES], x1_sc.at[s, :, c, :],
                                      x_sem.at[s]) for c in range(chunks)]

    def out_copies(step, s):
        rows = pl.ds(pl.multiple_of(step * tm, tm), tm)
        return [pltpu.make_async_copy(res_sc.at[s, :, c, :], out_hbm.at[rows, c * LANES:(c + 1) * LANES],
                                      o_sem.at[s]) for c in range(chunks)]

    @pl.when(i == 0)
    def _():
        for cp in x1_copies(0, 0):
            cp.start()

    @pl.when(i + 1 < n_steps)
    def _():
        for cp in x1_copies(i + 1, 1 - slot):
            cp.start()

    @pl.when(i == 0)
    def _():
        gather(dest_ref, 0)

    for parity in range(2):
        @pl.when(jnp.logical_and(i + 1 < n_steps, slot == parity))
        def _():
            gather(nxt_ref, 1 - parity)

    for k in range(TOP_K):
        pltpu.make_async_copy(ybuf_hbm.at[pl.ds(0, tm)], g_sc.at[slot, k], g_sem.at[slot]).wait()
    for cp in x1_copies(i, slot):
        cp.wait()

    for parity in range(2):
        @pl.when(slot == parity)
        def _():
            for r in range(tm):
                acc = x1_sc[parity, r]
                for k in range(TOP_K):
                    acc = acc + w_ref[k * tm + r] * g_sc[parity, k, r]
                res_sc[parity, r] = acc
    for cp in out_copies(i, slot):
        cp.start()

    @pl.when(i >= 1)
    def _():
        for cp in out_copies(i - 1, 1 - slot):
            cp.wait()

    @pl.when(i == n_steps - 1)
    def _():
        for cp in out_copies(i, slot):
            cp.wait()


def _combine(dest_flat, x1, w_flat, ybuf):
    T = x1.shape[0]
    tm = ROUTE_TM
    tile = (D_MODEL // LANES, LANES)
    return pl.pallas_call(
        _combine_kernel,
        out_shape=jax.ShapeDtypeStruct((T, D_MODEL), F32),
        grid=(T // tm,),
        in_specs=[
            pl.BlockSpec((tm * TOP_K,), lambda i: (i,), memory_space=pltpu.SMEM),
            pl.BlockSpec((tm * TOP_K,), lambda i: (jnp.minimum(i + 1, T // tm - 1),),
                         memory_space=pltpu.SMEM),
            pl.BlockSpec((tm * TOP_K,), lambda i: (i,), memory_space=pltpu.SMEM),
            pl.BlockSpec(memory_space=pl.ANY),
            pl.BlockSpec(memory_space=pl.ANY),
        ],
        out_specs=pl.BlockSpec(memory_space=pl.ANY),
        scratch_shapes=[pltpu.VMEM((2, TOP_K, tm) + tile, F32), pltpu.VMEM((2, tm) + tile, F32),
                        pltpu.VMEM((2, tm) + tile, F32), pltpu.SemaphoreType.DMA((2,)),
                        pltpu.SemaphoreType.DMA((2,)), pltpu.SemaphoreType.DMA((2,))],
        compiler_params=pltpu.CompilerParams(
            dimension_semantics=("arbitrary",), vmem_limit_bytes=VMEM_LIMIT),
        name="combine",
    )(dest_flat, dest_flat, w_flat, x1, ybuf)


def _layer(x2, mem2, B, S, g_mix, w_in, g_sgu_v, w_spatial, b_spatial, g_q_moba, g_k_moba,
           g_mem, w_mem_kv, g_q_mem, g_k_mem, w_br_sgu, w_br_moba, w_br_mem, w_out,
           g_ffn, w_router, b_router, w_gate_up, b_gate_up, w_down, b_down):
    T = B * S
    row = lambda v: v.reshape(1, -1)
    proj = _inproj(x2, row(g_mix), w_in.astype(BF16), row(g_sgu_v))
    yb = _moba(proj, g_q_moba, g_k_moba, B, S)
    kvmem = _memkv(mem2, row(g_mem), w_mem_kv.astype(BF16), row(g_k_mem), B)

    causal = jnp.tril(jnp.ones((SGU_CHUNK, SGU_CHUNK), dtype=bool))
    ws = jnp.where(causal[None], w_spatial, 0.0).astype(BF16)
    bsp = jnp.repeat(b_spatial.T, LANES, axis=1)
    x1, h2, top_e, top_w, rank, counts = _merge(
        proj, yb, x2, kvmem, ws, bsp, row(g_q_mem), w_br_sgu.astype(BF16),
        w_br_moba.astype(BF16), w_br_mem.astype(BF16), w_out.astype(BF16), row(g_ffn),
        w_router, b_router, S)

    bm = EXPERT_BM
    n_blocks = -(-(T * TOP_K + N_EXPERTS * (bm - 1)) // bm)
    cnt = counts.reshape(N_EXPERTS).astype(jnp.int32)
    padded = (cnt + bm - 1) // bm * bm
    pad_ends = jnp.cumsum(padded)
    pad_starts = pad_ends - padded
    e_ids = jnp.arange(N_EXPERTS, dtype=jnp.int32)[:, None, None]
    start_of = jnp.sum(jnp.where(top_e[None] == e_ids, pad_starts[:, None, None], 0), axis=0)
    per_tile = lambda a: a.reshape(TOP_K, T // ROUTE_TM, ROUTE_TM).transpose(1, 0, 2).reshape(T * TOP_K)
    dest = per_tile(start_of + rank)
    top_w = per_tile(top_w)
    block_row0 = jnp.arange(n_blocks, dtype=jnp.int32) * bm
    block_e = jnp.minimum(
        jnp.sum((pad_ends[None, :] <= block_row0[:, None]).astype(jnp.int32), axis=1), N_EXPERTS - 1)
    n_used = (pad_ends[-1:] // bm).astype(jnp.int32)

    xbuf = _dispatch(pad_starts + cnt, padded - cnt, dest, h2, n_blocks * bm)
    ybuf = _experts(block_e, n_used, xbuf, w_gate_up, b_gate_up.reshape(N_EXPERTS, 1, 2 * D_FF),
                    w_down, b_down.reshape(N_EXPERTS, 1, D_MODEL))
    return _combine(dest, x1, top_w, ybuf)


def kernel(x, mem, g_mix, w_in, g_sgu_v, w_spatial, b_spatial, g_q_moba, g_k_moba, g_mem, w_mem_kv, g_q_mem, g_k_mem, w_br_sgu, w_br_moba, w_br_mem, w_out, g_ffn, w_router, b_router, w_gate_up, b_gate_up, w_down, b_down):
    B, S, D = x.shape
    x2 = x.reshape(B * S, D)
    mem2 = mem.reshape(B * MEM_LEN, D)
    params = (g_mix, w_in, g_sgu_v, w_spatial, b_spatial, g_q_moba, g_k_moba, g_mem, w_mem_kv,
              g_q_mem, g_k_mem, w_br_sgu, w_br_moba, w_br_mem, w_out, g_ffn, w_router, b_router,
              w_gate_up, b_gate_up, w_down, b_down)
    for l in range(g_mix.shape[0]):
        x2 = _layer(x2, mem2, B, S, *(p[l] for p in params))
    return x2.reshape(B, S, D)
```

```python
import functools

import numpy as np
import jax
import jax.numpy as jnp
from jax import lax
from jax.experimental import pallas as pl
from jax.experimental.pallas import tpu as pltpu

F32 = jnp.float32
BF16 = jnp.bfloat16

D_MODEL = 1024
MEM_LEN = 256
SGU_GROUPS = 4
SGU_CHUNK = 128
SGU_WIDTH = 512
MOBA_HEADS = 8
MOBA_HEAD_DIM = 64
MOBA_BLOCK = 256
MOBA_TOPK = 3
MEM_HEADS = 4
MEM_HEAD_DIM = 128
SPLIT = 512
IN_WIDTH = 6144
N_SPLITS = IN_WIDTH // SPLIT
N_EXPERTS = 32
TOP_K = 4
D_FF = 1024
SWIGLU_ALPHA = 1.702
SWIGLU_LIMIT = 7.0
EPS = 1e-6
NEG_INF = -1e30
LANES = 128

VMEM_LIMIT = 56 * 1024 * 1024

INPROJ_TM = 256
MERGE_TM = 512
ROUTE_TM = 256
EXPERT_BM = 512


def _gelu(x):
    return 0.5 * x * (1.0 + lax.erf(x * np.float32(np.sqrt(0.5))))


def _rms(x, g):
    return x * lax.rsqrt(jnp.mean(x * x, axis=-1, keepdims=True) + EPS) * g


def _inproj_kernel(x_ref, gmix_ref, w_ref, gsgu_ref, o_ref):
    h = _rms(x_ref[...], gmix_ref[...]).astype(BF16)
    for c in range(N_SPLITS):
        sl = slice(c * SPLIT, (c + 1) * SPLIT)
        acc = jnp.dot(h, w_ref[:, sl], preferred_element_type=F32)
        if c == 0:
            val = _gelu(acc)
        elif c == 1:
            val = _rms(_gelu(acc), gsgu_ref[...])
        elif c >= 6:
            val = jax.nn.sigmoid(acc)
        else:
            val = acc
        o_ref[:, sl] = val.astype(BF16)


def _inproj(x2, g_mix, w_in, g_sgu_v):
    T = x2.shape[0]
    tm = INPROJ_TM
    return pl.pallas_call(
        _inproj_kernel,
        out_shape=jax.ShapeDtypeStruct((T, IN_WIDTH), BF16),
        grid=(T // tm,),
        in_specs=[
            pl.BlockSpec((tm, D_MODEL), lambda i: (i, 0)),
            pl.BlockSpec((1, D_MODEL), lambda i: (0, 0)),
            pl.BlockSpec((D_MODEL, IN_WIDTH), lambda i: (0, 0)),
            pl.BlockSpec((1, SGU_WIDTH), lambda i: (0, 0)),
        ],
        out_specs=pl.BlockSpec((tm, IN_WIDTH), lambda i: (i, 0)),
        compiler_params=pltpu.CompilerParams(
            dimension_semantics=("arbitrary",), vmem_limit_bytes=VMEM_LIMIT),
        name="inproj",
    )(x2, g_mix, w_in, g_sgu_v)


def _pair_norm(x, g, head0):
    sq = x * x
    s0 = jnp.sum(jnp.where(head0, sq, 0.0), axis=-1, keepdims=True)
    s1 = jnp.sum(jnp.where(head0, 0.0, sq), axis=-1, keepdims=True)
    inv = jnp.where(head0, lax.rsqrt(s0 * (1.0 / MOBA_HEAD_DIM) + EPS),
                    lax.rsqrt(s1 * (1.0 / MOBA_HEAD_DIM) + EPS))
    return x * inv * g


VT_ROWS = MOBA_HEAD_DIM + 16
LOG2E = float(np.log2(np.e))


def _moba_kernel(slopes_ref, q_ref, k_ref, v_ref, gq_ref, gk_ref, bias_ref, o_ref,
                 kn_sc, vt_sc, kmean_sc, mrow_sc, qh_sc, m_sc, acc_sc, t_sc, cmax_sc, *, nb):
    qi = pl.program_id(1)
    bs = MOBA_BLOCK
    pairs = MOBA_HEADS // 2
    head0 = lax.broadcasted_iota(jnp.int32, (1, LANES), 1) < MOBA_HEAD_DIM
    nt = (((1,), (1,)), ((), ()))

    @pl.when(qi == 0)
    def _prepare_kv():
        ones_rows = jnp.where(
            lax.broadcasted_iota(jnp.int32, (VT_ROWS - MOBA_HEAD_DIM, bs), 0) == 0, 1.0, 0.0).astype(BF16)

        def body(j, carry):
            rows = pl.ds(pl.multiple_of(j * bs, bs), bs)
            means = []
            for g in range(pairs):
                cols = slice(g * LANES, (g + 1) * LANES)
                kn = _pair_norm(k_ref[rows, cols].astype(F32), gk_ref[...], head0)
                kn_sc[j, :, cols] = kn.astype(BF16)
                means.append(jnp.mean(kn, axis=0, keepdims=True))
                vt = v_ref[rows, cols].astype(F32).T.astype(BF16)
                for hh in range(2):
                    h = 2 * g + hh
                    vt_sc[j, h, :MOBA_HEAD_DIM, :] = vt[hh * MOBA_HEAD_DIM:(hh + 1) * MOBA_HEAD_DIM]
                    vt_sc[j, h, MOBA_HEAD_DIM:, :] = ones_rows
            kmean_sc[pl.ds(j, 1), :] = jnp.concatenate(means, axis=1)
            return carry
        lax.fori_loop(0, nb, body, 0)

    blk = lax.broadcasted_iota(jnp.int32, (nb, 1), 0)
    eligible = blk < qi
    dist = ((qi - blk) * bs).astype(F32)
    for g in range(pairs):
        cols = slice(g * LANES, (g + 1) * LANES)
        qn = _pair_norm(q_ref[:, cols].astype(F32), gq_ref[...], head0)
        qpair = jnp.concatenate([jnp.where(head0, qn, 0.0), jnp.where(head0, 0.0, qn)], axis=0)
        qh_sc[g] = (qpair * (LOG2E * MOBA_HEAD_DIM ** -0.5)).astype(BF16)
        q_hi = qpair.astype(BF16)
        q_lo = (qpair - q_hi.astype(F32)).astype(BF16)
        km = kmean_sc[:, cols]
        km_hi = km.astype(BF16)
        km_lo = (km - km_hi.astype(F32)).astype(BF16)
        gate = (lax.dot_general(km_hi, q_hi, nt, preferred_element_type=F32)
                + lax.dot_general(km_hi, q_lo, nt, preferred_element_type=F32)
                + lax.dot_general(km_lo, q_hi, nt, preferred_element_type=F32))
        work = jnp.where(eligible, gate, -jnp.inf)
        sel = jnp.zeros((nb, 2 * bs), jnp.bool_)
        for _ in range(MOBA_TOPK):
            best = jnp.max(work, axis=0, keepdims=True)
            first = jnp.min(jnp.where(work == best, blk, nb), axis=0, keepdims=True)
            pick = jnp.logical_and(blk == first, best > -jnp.inf)
            sel = jnp.logical_or(sel, pick)
            work = jnp.where(pick, -jnp.inf, work)
        for hh in range(2):
            h = 2 * g + hh
            mrow_sc[:, h * bs:(h + 1) * bs] = jnp.where(
                blk == qi, 0.0,
                jnp.where(sel[:, hh * bs:(hh + 1) * bs], (-LOG2E * slopes_ref[h]) * dist, NEG_INF))
    m_sc[...] = jnp.full(m_sc.shape, NEG_INF, F32)
    acc_sc[...] = jnp.zeros(acc_sc.shape, F32)

    def score(j, own, slot):
        scores = [lax.dot_general(kn_sc[j, :, g * LANES:(g + 1) * LANES], qh_sc[g], nt,
                                  preferred_element_type=F32) for g in range(pairs)]
        t = jnp.concatenate(scores, axis=1) + bias_ref[1 if own else 0]
        t_sc[slot] = t
        cmax_sc[slot] = jnp.max(t, axis=0, keepdims=True)

    def accumulate(j, slot):
        mrow = mrow_sc[pl.ds(j, 1), :]
        m_old = m_sc[...]
        m_new = jnp.maximum(m_old, cmax_sc[slot] + mrow)
        shift = m_new - mrow
        alpha = jnp.exp2(m_old - m_new)
        m_sc[...] = m_new
        for h in range(MOBA_HEADS):
            cols = slice(h * bs, (h + 1) * bs)
            p = jnp.exp2(t_sc[slot, :, cols] - shift[:, cols]).astype(BF16)
            pv = jnp.dot(vt_sc[j, h], p, preferred_element_type=F32)
            acc_sc[h] = alpha[:, cols] * acc_sc[h] + pv

    n_past = qi
    score(qi, True, 0)

    def two_blocks(p, carry):
        score(2 * p, False, 1)
        accumulate(jnp.where(p == 0, qi, 2 * p - 1), 0)
        score(2 * p + 1, False, 0)
        accumulate(2 * p, 1)
        return carry
    lax.fori_loop(0, n_past // 2, two_blocks, 0)

    @pl.when(n_past % 2 == 0)
    def _():
        accumulate(jnp.where(n_past == 0, qi, n_past - 1), 0)

    @pl.when(n_past % 2 == 1)
    def _():
        score(n_past - 1, False, 1)
        accumulate(jnp.where(n_past == 1, qi, n_past - 2), 0)
        accumulate(n_past - 1, 1)

    for g in range(pairs):
        halves = []
        for hh in range(2):
            acc = acc_sc[2 * g + hh]
            halves.append(acc[:MOBA_HEAD_DIM] / acc[MOBA_HEAD_DIM:MOBA_HEAD_DIM + 1])
        out_t = jnp.concatenate(halves, axis=0)
        o_ref[:, g * LANES:(g + 1) * LANES] = out_t.T.astype(BF16)


def _moba_bias_tiles():
    slopes = 2.0 ** (-8.0 * np.arange(1, MOBA_HEADS + 1, dtype=np.float32) / MOBA_HEADS)
    kk = np.arange(MOBA_BLOCK)[:, None]
    qq = np.arange(MOBA_BLOCK)[None, :]
    rel = (qq - kk).astype(np.float64)
    past = -slopes.astype(np.float64)[:, None, None] * rel[None] * LOG2E
    own = np.where((kk <= qq)[None], past, NEG_INF)
    side_by_side = lambda a: a.transpose(1, 0, 2).reshape(MOBA_BLOCK, MOBA_HEADS * MOBA_BLOCK)
    return slopes.astype(np.float32), np.stack([side_by_side(past), side_by_side(own)]).astype(np.float32)


def _moba(proj, g_q, g_k, B, S):
    nb = S // MOBA_BLOCK
    width = MOBA_HEADS * MOBA_HEAD_DIM
    slopes, bias = _moba_bias_tiles()
    gq2 = jnp.tile(g_q.reshape(1, MOBA_HEAD_DIM), (1, 2))
    gk2 = jnp.tile(g_k.reshape(1, MOBA_HEAD_DIM), (1, 2))
    full = lambda shape: pl.BlockSpec(shape, lambda b, q, s: (0,) * len(shape))
    grid_spec = pltpu.PrefetchScalarGridSpec(
        num_scalar_prefetch=1,
        grid=(B, nb),
        in_specs=[
            pl.BlockSpec((MOBA_BLOCK, width), lambda b, q, s: (b * nb + q, 2)),
            pl.BlockSpec((S, width), lambda b, q, s: (b, 3)),
            pl.BlockSpec((S, width), lambda b, q, s: (b, 4)),
            full((1, LANES)),
            full((1, LANES)),
            full((2, MOBA_BLOCK, MOBA_HEADS * MOBA_BLOCK)),
        ],
        out_specs=pl.BlockSpec((MOBA_BLOCK, width), lambda b, q, s: (b * nb + q, 0)),
        scratch_shapes=[
            pltpu.VMEM((nb, MOBA_BLOCK, width), BF16),
            pltpu.VMEM((nb, MOBA_HEADS, VT_ROWS, MOBA_BLOCK), BF16),
            pltpu.VMEM((nb, width), F32),
            pltpu.VMEM((nb, MOBA_HEADS * MOBA_BLOCK), F32),
            pltpu.VMEM((MOBA_HEADS // 2, 2 * MOBA_BLOCK, LANES), BF16),
            pltpu.VMEM((1, MOBA_HEADS * MOBA_BLOCK), F32),
            pltpu.VMEM((MOBA_HEADS, VT_ROWS, MOBA_BLOCK), F32),
            pltpu.VMEM((2, MOBA_BLOCK, MOBA_HEADS * MOBA_BLOCK), F32),
            pltpu.VMEM((2, 1, MOBA_HEADS * MOBA_BLOCK), F32),
        ],
    )
    return pl.pallas_call(
        functools.partial(_moba_kernel, nb=nb),
        out_shape=jax.ShapeDtypeStruct((B * S, width), BF16),
        grid_spec=grid_spec,
        compiler_params=pltpu.CompilerParams(
            dimension_semantics=("arbitrary", "arbitrary"), vmem_limit_bytes=VMEM_LIMIT),
        name="moba",
    )(jnp.asarray(slopes), proj, proj, proj, gq2, gk2, jnp.asarray(bias))


def _memkv_kernel(mem_ref, gmem_ref, w_ref, gk_ref, o_ref):
    mn = _rms(mem_ref[...], gmem_ref[...]).astype(BF16)
    kv = jnp.dot(mn, w_ref[...], preferred_element_type=F32)
    width = MEM_HEADS * MEM_HEAD_DIM
    for h in range(MEM_HEADS):
        sl = slice(h * MEM_HEAD_DIM, (h + 1) * MEM_HEAD_DIM)
        o_ref[:, sl] = _rms(kv[:, sl], gk_ref[...]).astype(BF16)
    o_ref[:, width:] = kv[:, width:].astype(BF16)


def _memkv(mem2, g_mem, w_mem_kv, g_k_mem, B):
    width = MEM_HEADS * MEM_HEAD_DIM
    return pl.pallas_call(
        _memkv_kernel,
        out_shape=jax.ShapeDtypeStruct((B * MEM_LEN, 2 * width), BF16),
        grid=(B,),
        in_specs=[
            pl.BlockSpec((MEM_LEN, D_MODEL), lambda b: (b, 0)),
            pl.BlockSpec((1, D_MODEL), lambda b: (0, 0)),
            pl.BlockSpec((D_MODEL, 2 * width), lambda b: (0, 0)),
            pl.BlockSpec((1, MEM_HEAD_DIM), lambda b: (0, 0)),
        ],
        out_specs=pl.BlockSpec((MEM_LEN, 2 * width), lambda b: (b, 0)),
        compiler_params=pltpu.CompilerParams(
            dimension_semantics=("arbitrary",), vmem_limit_bytes=VMEM_LIMIT),
        name="memkv",
    )(mem2, g_mem, w_mem_kv, g_k_mem)


def _merge_kernel(gu_ref, vn_ref, yb_ref, qm_ref, ga_ref, gb_ref, gc_ref, x_ref, kv_ref,
                  ws_ref, bsp_ref, gqm_ref, wa_ref, wb_ref, wc_ref, wo_ref, gffn_ref,
                  wrh_ref, wrl_ref, br_ref,
                  x1_ref, h2_hbm, tope_ref, topw_ref, rank_ref, cnt_ref, run_sc, hp_sc, hp_sem):
    i = pl.program_id(0)
    tm = x_ref.shape[0]

    @pl.when(i == 0)
    def _():
        run_sc[...] = jnp.zeros_like(run_sc)

    ya_chunks = []
    for c in range(tm // SGU_CHUNK):
        rows = slice(c * SGU_CHUNK, (c + 1) * SGU_CHUNK)
        groups = []
        for g in range(SGU_GROUPS):
            cols = slice(g * LANES, (g + 1) * LANES)
            groups.append(jnp.dot(ws_ref[g], vn_ref[rows, cols], preferred_element_type=F32))
        mixed = jnp.concatenate(groups, axis=1) + bsp_ref[...]
        ya_chunks.append(gu_ref[rows, :].astype(F32) * mixed)
    y_a = jnp.concatenate(ya_chunks, axis=0).astype(BF16)

    width = MEM_HEADS * MEM_HEAD_DIM
    yc_heads = []
    for h in range(MEM_HEADS):
        cols = slice(h * MEM_HEAD_DIM, (h + 1) * MEM_HEAD_DIM)
        qn = _rms(qm_ref[:, cols].astype(F32), gqm_ref[...]).astype(BF16)
        s = lax.dot_general(qn, kv_ref[:, cols], (((1,), (1,)), ((), ())),
                            preferred_element_type=F32) * (MEM_HEAD_DIM ** -0.5)
        s = s - jnp.max(s, axis=-1, keepdims=True)
        p = jnp.exp(s)
        p = p / jnp.sum(p, axis=-1, keepdims=True)
        vcols = slice(width + h * MEM_HEAD_DIM, width + (h + 1) * MEM_HEAD_DIM)
        yc_heads.append(jnp.dot(p.astype(BF16), kv_ref[:, vcols], preferred_element_type=F32))
    y_c = jnp.concatenate(yc_heads, axis=1).astype(BF16)

    merged = ga_ref[...].astype(F32) * jnp.dot(y_a, wa_ref[...], preferred_element_type=F32)
    merged = merged + gb_ref[...].astype(F32) * jnp.dot(yb_ref[...], wb_ref[...],
                                                       preferred_element_type=F32)
    merged = merged + gc_ref[...].astype(F32) * jnp.dot(y_c, wc_ref[...],
                                                       preferred_element_type=F32)
    x1 = x_ref[...] + jnp.dot(merged.astype(BF16), wo_ref[...], preferred_element_type=F32)
    x1_ref[...] = x1
    h2 = _rms(x1, gffn_ref[...])
    h2_hi = h2.astype(BF16)
    h2_hi32 = h2_hi.astype(F32)
    h2_lo = (h2 - h2_hi32).astype(BF16)
    bits = lax.bitcast_convert_type(h2_hi32, jnp.uint32)
    half = D_MODEL // 2
    slot = i % 2
    hp_sc[slot] = (bits[:, :half] >> 16) | (bits[:, half:] & jnp.uint32(0xFFFF0000))

    def h2_copies(step, s):
        rows = pl.ds(pl.multiple_of(step * tm, tm), tm)
        return [pltpu.make_async_copy(hp_sc.at[s, :, c * LANES:(c + 1) * LANES], h2_hbm.at[rows, c],
                                      hp_sem.at[s]) for c in range(half // LANES)]

    for cp in h2_copies(i, slot):
        cp.start()

    nt = (((1,), (1,)), ((), ()))
    logits = (lax.dot_general(wrh_ref[...], h2_hi, nt, preferred_element_type=F32)
              + lax.dot_general(wrh_ref[...], h2_lo, nt, preferred_element_type=F32)
              + lax.dot_general(wrl_ref[...], h2_hi, nt, preferred_element_type=F32)
              + br_ref[...])
    row = lax.broadcasted_iota(jnp.int32, (N_EXPERTS, tm), 0)
    work = logits
    top_e, top_l = [], []
    for _ in range(TOP_K):
        mx = jnp.max(work, axis=0, keepdims=True)
        idx = jnp.min(jnp.where(work == mx, row, N_EXPERTS), axis=0, keepdims=True)
        top_e.append(idx)
        top_l.append(mx)
        work = jnp.where(row == idx, -jnp.inf, work)
    exps = [jnp.exp(v - top_l[0]) for v in top_l]
    denom = exps[0] + exps[1] + exps[2] + exps[3]

    onehots = [row == e for e in top_e]
    multi = jnp.zeros((N_EXPERTS, tm), F32)
    for oh in onehots:
        multi = multi + jnp.where(oh, 1.0, 0.0)
    r_i = lax.broadcasted_iota(jnp.int32, (tm, tm), 0)
    c_i = lax.broadcasted_iota(jnp.int32, (tm, tm), 1)
    earlier = jnp.where(r_i < c_i, 1.0, 0.0).astype(BF16)
    pos = run_sc[...] + jnp.dot(multi.astype(BF16), earlier, preferred_element_type=F32)
    run_sc[...] = run_sc[...] + jnp.sum(multi, axis=1, keepdims=True)
    cnt_ref[...] = run_sc[...]

    for k in range(TOP_K):
        tope_ref[k:k + 1, :] = top_e[k]
        topw_ref[k:k + 1, :] = exps[k] / denom
        rank_ref[k:k + 1, :] = jnp.sum(jnp.where(onehots[k], pos, 0.0), axis=0,
                                       keepdims=True).astype(jnp.int32)

    @pl.when(i >= 1)
    def _():
        for cp in h2_copies(i - 1, 1 - slot):
            cp.wait()

    @pl.when(i == pl.num_programs(0) - 1)
    def _():
        for cp in h2_copies(i, slot):
            cp.wait()


def _merge(proj, yb, x2, kvmem, ws, bsp, g_q_mem, wa, wb, wc, wo, g_ffn, w_router, b_router, S):
    T = x2.shape[0]
    tm = MERGE_TM
    per_b = S // tm
    full = lambda shape: pl.BlockSpec(shape, lambda i: (0,) * len(shape))
    in_specs = [
        pl.BlockSpec((tm, SPLIT), lambda i: (i, 0)),
        pl.BlockSpec((tm, SPLIT), lambda i: (i, 1)),
        pl.BlockSpec((tm, SPLIT), lambda i: (i, 0)),
        pl.BlockSpec((tm, SPLIT), lambda i: (i, 5)),
        pl.BlockSpec((tm, D_MODEL), lambda i: (i, 3)),
        pl.BlockSpec((tm, D_MODEL), lambda i: (i, 4)),
        pl.BlockSpec((tm, D_MODEL), lambda i: (i, 5)),
        pl.BlockSpec((tm, D_MODEL), lambda i: (i, 0)),
        pl.BlockSpec((MEM_LEN, 2 * MEM_HEADS * MEM_HEAD_DIM), lambda i: (i // per_b, 0)),
        full((SGU_GROUPS, SGU_CHUNK, SGU_CHUNK)),
        full((SGU_CHUNK, SGU_WIDTH)),
        full((1, MEM_HEAD_DIM)),
        full((SGU_WIDTH, D_MODEL)),
        full((SGU_WIDTH, D_MODEL)),
        full((SGU_WIDTH, D_MODEL)),
        full((D_MODEL, D_MODEL)),
        full((1, D_MODEL)),
        full((N_EXPERTS, D_MODEL)),
        full((N_EXPERTS, D_MODEL)),
        full((N_EXPERTS, 1)),
    ]
    out_shape = (
        jax.ShapeDtypeStruct((T, D_MODEL), F32),
        jax.ShapeDtypeStruct((T, D_MODEL // 2 // LANES, LANES), jnp.uint32),
        jax.ShapeDtypeStruct((TOP_K, T), jnp.int32),
        jax.ShapeDtypeStruct((TOP_K, T), F32),
        jax.ShapeDtypeStruct((TOP_K, T), jnp.int32),
        jax.ShapeDtypeStruct((N_EXPERTS, 1), F32),
    )
    out_specs = (
        pl.BlockSpec((tm, D_MODEL), lambda i: (i, 0)),
        pl.BlockSpec(memory_space=pl.ANY),
        pl.BlockSpec((TOP_K, tm), lambda i: (0, i)),
        pl.BlockSpec((TOP_K, tm), lambda i: (0, i)),
        pl.BlockSpec((TOP_K, tm), lambda i: (0, i)),
        pl.BlockSpec((N_EXPERTS, 1), lambda i: (0, 0)),
    )
    wr_t = w_router.T
    wr_hi = wr_t.astype(BF16)
    wr_lo = (wr_t - wr_hi.astype(F32)).astype(BF16)
    return pl.pallas_call(
        _merge_kernel,
        out_shape=out_shape,
        grid=(T // tm,),
        in_specs=in_specs,
        out_specs=out_specs,
        scratch_shapes=[pltpu.VMEM((N_EXPERTS, 1), F32),
                        pltpu.VMEM((2, tm, D_MODEL // 2), jnp.uint32), pltpu.SemaphoreType.DMA((2,))],
        compiler_params=pltpu.CompilerParams(
            dimension_semantics=("arbitrary",), vmem_limit_bytes=VMEM_LIMIT),
        name="merge",
    )(proj, proj, yb, proj, proj, proj, proj, x2, kvmem, ws, bsp, g_q_mem,
      wa, wb, wc, wo, g_ffn, wr_hi, wr_lo, b_router.reshape(N_EXPERTS, 1))


def _dispatch_kernel(padlo_ref, padlen_ref, dest_ref, h_hbm, xbuf_ref, sem, z_sc, z_sem, h_sc, in_sem):
    i = pl.program_id(0)
    tm = h_sc.shape[1]

    @pl.when(i == 0)
    def _zero_padding_rows():
        z_sc[...] = jnp.zeros(z_sc.shape, z_sc.dtype)
        bits = range(EXPERT_BM.bit_length() - 2, -1, -1)

        def pieces(e):
            off, length = padlo_ref[e], padlen_ref[e]
            for b in bits:
                take = ((length >> b) & 1) == 1
                yield take, pltpu.make_async_copy(z_sc.at[pl.ds(0, 1 << b)],
                                                  xbuf_ref.at[pl.ds(off, 1 << b)], z_sem)
                off = off + jnp.where(take, 1 << b, 0)

        for e in range(N_EXPERTS):
            for take, cp in pieces(e):
                pl.when(take)(cp.start)
        for e in range(N_EXPERTS):
            for take, cp in pieces(e):
                pl.when(take)(cp.wait)

        zrows = z_sc.shape[0]
        first = (padlo_ref[N_EXPERTS - 1] + padlen_ref[N_EXPERTS - 1]) // zrows
        def tail_copy(j):
            return pltpu.make_async_copy(z_sc, xbuf_ref.at[pl.ds(pl.multiple_of(j * zrows, zrows), zrows)], z_sem)
        lax.fori_loop(first, xbuf_ref.shape[0] // zrows, lambda j, c: (tail_copy(j).start(), c)[1], 0)
        lax.fori_loop(first, xbuf_ref.shape[0] // zrows, lambda j, c: (tail_copy(j).wait(), c)[1], 0)

    n_steps = pl.num_programs(0)
    nbuf = h_sc.shape[0]

    def fetch(step, s):
        rows = pl.ds(pl.multiple_of(step * tm, tm), tm)
        return pltpu.make_async_copy(h_hbm.at[rows], h_sc.at[s], in_sem.at[s])

    def drain(s):
        for _ in range(TOP_K):
            pltpu.make_async_copy(h_sc.at[s], xbuf_ref.at[pl.ds(0, tm)], sem.at[s]).wait()

    @pl.when(i == 0)
    def _():
        fetch(0, 0).start()

    @pl.when(i + 1 < n_steps)
    def _():
        fetch(i + 1, (i + 1) % nbuf).start()

    fetch(i, i % nbuf).wait()
    for s in range(nbuf):
        @pl.when(i % nbuf == s)
        def _():
            for r in range(tm):
                for k in range(TOP_K):
                    pltpu.make_async_copy(h_sc.at[s, r], xbuf_ref.at[dest_ref[k * tm + r]],
                                          sem.at[s]).start(priority=k % 2)

            @pl.when(i >= 1)
            def _():
                drain((s - 1) % nbuf)

            @pl.when(i == n_steps - 1)
            def _():
                drain(s)


def _dispatch(pad_lo, pad_len, dest_flat, h2, n_rows):
    T, slabs, lanes = h2.shape
    tm = ROUTE_TM
    grid_spec = pltpu.PrefetchScalarGridSpec(
        num_scalar_prefetch=2,
        grid=(T // tm,),
        in_specs=[
            pl.BlockSpec((tm * TOP_K,), lambda i, lo, ln: (i,), memory_space=pltpu.SMEM),
            pl.BlockSpec(memory_space=pl.ANY),
        ],
        out_specs=pl.BlockSpec(memory_space=pl.ANY),
        scratch_shapes=[pltpu.SemaphoreType.DMA((3,)), pltpu.VMEM((EXPERT_BM // 2, slabs, lanes), h2.dtype),
                        pltpu.SemaphoreType.DMA, pltpu.VMEM((3, tm, slabs, lanes), h2.dtype),
                        pltpu.SemaphoreType.DMA((3,))],
    )
    return pl.pallas_call(
        _dispatch_kernel,
        out_shape=jax.ShapeDtypeStruct((n_rows, slabs, lanes), h2.dtype),
        grid_spec=grid_spec,
        compiler_params=pltpu.CompilerParams(
            dimension_semantics=("arbitrary",), vmem_limit_bytes=VMEM_LIMIT),
        name="dispatch",
    )(pad_lo, pad_len, dest_flat, h2)


FF_CHUNK = 512


def _expert_kernel(be_ref, nused_ref, x_hbm, w1_ref, b1_ref, w2_ref, b2_ref,
                   y_hbm, w1_sc, w2_sc, x_sc, x_sem, y_sc, y_sem):
    i = pl.program_id(0)
    bm = x_sc.shape[1]
    n_used = nused_ref[0]

    def x_copies(blk, slot):
        rows = pl.ds(pl.multiple_of(blk * bm, bm), bm)
        return [pltpu.make_async_copy(x_hbm.at[rows, c], x_sc.at[slot, :, c * LANES:(c + 1) * LANES],
                                      x_sem.at[slot]) for c in range(x_hbm.shape[1])]

    def y_copies(blk, slot):
        rows = pl.ds(pl.multiple_of(blk * bm, bm), bm)
        return [pltpu.make_async_copy(y_sc.at[slot, :, c * LANES:(c + 1) * LANES], y_hbm.at[rows, c],
                                      y_sem.at[slot]) for c in range(y_hbm.shape[1])]

    @pl.when(i < n_used)
    def _():
        slot = i % 2

        @pl.when(i == 0)
        def _():
            for cp in x_copies(0, 0):
                cp.start()

        @pl.when(i + 1 < n_used)
        def _():
            for cp in x_copies(i + 1, 1 - slot):
                cp.start()

        @pl.when(jnp.logical_or(i == 0, be_ref[i] != be_ref[jnp.maximum(i - 1, 0)]))
        def _():
            w1_sc[...] = w1_ref[0].astype(BF16)
            w2_sc[...] = w2_ref[0].astype(BF16)

        for cp in x_copies(i, slot):
            cp.wait()
        packed = x_sc[slot]
        lo = lax.bitcast_convert_type(packed << 16, F32)
        hi = lax.bitcast_convert_type(packed & jnp.uint32(0xFFFF0000), F32)
        xb = jnp.concatenate([lo, hi], axis=1).astype(BF16)
        y = b2_ref[0]
        for c in range(D_FF // FF_CHUNK):
            g_cols = slice(c * FF_CHUNK, (c + 1) * FF_CHUNK)
            l_cols = slice(D_FF + c * FF_CHUNK, D_FF + (c + 1) * FF_CHUNK)
            glu = jnp.dot(xb, w1_sc[:, g_cols], preferred_element_type=F32) + b1_ref[0, :, g_cols]
            lin = jnp.dot(xb, w1_sc[:, l_cols], preferred_element_type=F32) + b1_ref[0, :, l_cols]
            glu = jnp.minimum(glu, SWIGLU_LIMIT)
            lin = jnp.clip(lin, -SWIGLU_LIMIT, SWIGLU_LIMIT)
            act = glu * jax.nn.sigmoid(SWIGLU_ALPHA * glu) * (lin + 1.0)
            y = y + jnp.dot(act.astype(BF16), w2_sc[g_cols, :], preferred_element_type=F32)
        y_sc[slot] = y
        for cp in y_copies(i, slot):
            cp.start()

        @pl.when(i >= 1)
        def _():
            for cp in y_copies(i - 1, 1 - slot):
                cp.wait()

        @pl.when(i == n_used - 1)
        def _():
            for cp in y_copies(i, slot):
                cp.wait()

    @pl.when(i >= n_used)
    def _():
        @pl.when(i == n_used)
        def _():
            y_sc[0] = jnp.zeros(y_sc.shape[1:], F32)

        for cp in y_copies(i, 0):
            cp.start()

        @pl.when(i == pl.num_programs(0) - 1)
        def _():
            def drain(j, carry):
                for cp in y_copies(j, 0):
                    cp.wait()
                return carry
            lax.fori_loop(n_used, pl.num_programs(0), drain, 0)


def _experts(block_e, n_used, xbuf, w1, b1, w2, b2):
    n_rows, slabs, lanes = xbuf.shape
    bm = EXPERT_BM
    grid_spec = pltpu.PrefetchScalarGridSpec(
        num_scalar_prefetch=2,
        grid=(n_rows // bm,),
        in_specs=[
            pl.BlockSpec(memory_space=pl.ANY),
            pl.BlockSpec((1, D_MODEL, 2 * D_FF), lambda i, be, nu: (be[i], 0, 0)),
            pl.BlockSpec((1, 1, 2 * D_FF), lambda i, be, nu: (be[i], 0, 0)),
            pl.BlockSpec((1, D_FF, D_MODEL), lambda i, be, nu: (be[i], 0, 0)),
            pl.BlockSpec((1, 1, D_MODEL), lambda i, be, nu: (be[i], 0, 0)),
        ],
        out_specs=pl.BlockSpec(memory_space=pl.ANY),
        scratch_shapes=[pltpu.VMEM((D_MODEL, 2 * D_FF), BF16), pltpu.VMEM((D_FF, D_MODEL), BF16),
                        pltpu.VMEM((2, bm, slabs * lanes), jnp.uint32), pltpu.SemaphoreType.DMA((2,)),
                        pltpu.VMEM((2, bm, D_MODEL), F32), pltpu.SemaphoreType.DMA((2,))],
    )
    return pl.pallas_call(
        _expert_kernel,
        out_shape=jax.ShapeDtypeStruct((n_rows, D_MODEL // LANES, LANES), F32),
        grid_spec=grid_spec,
        compiler_params=pltpu.CompilerParams(
            dimension_semantics=("arbitrary",), vmem_limit_bytes=VMEM_LIMIT),
        name="experts",
    )(block_e, n_used, xbuf, w1, b1, w2, b2)


def _combine_kernel(dest_ref, nxt_ref, w_ref, x1_hbm, ybuf_hbm, out_hbm, g_sc, x1_sc, res_sc,
                    g_sem, x_sem, o_sem):
    i = pl.program_id(0)
    n_steps = pl.num_programs(0)
    tm = g_sc.shape[2]
    chunks = D_MODEL // LANES
    slot = i % 2

    def gather(idx_ref, s):
        for r in range(tm):
            for k in range(TOP_K):
                pltpu.make_async_copy(ybuf_hbm.at[idx_ref[k * tm + r]], g_sc.at[s, k, r],
                                      g_sem.at[s]).start(priority=k % 2)

    def x1_copies(step, s):
        rows = pl.ds(pl.multiple_of(step * tm, tm), tm)
        return [pltpu.make_async_copy(x1_hbm.at[rows, c * LANES:(c + 1) * LANES], x1_sc.at[s, :, c, :],
                                      x_sem.at[s]) for c in range(chunks)]

    def out_copies(step, s):
        rows = pl.ds(pl.multiple_of(step * tm, tm), tm)
        return [pltpu.make_async_copy(res_sc.at[s, :, c, :], out_hbm.at[rows, c * LANES:(c + 1) * LANES],
                                      o_sem.at[s]) for c in range(chunks)]

    @pl.when(i == 0)
    def _():
        for cp in x1_copies(0, 0):
            cp.start()

    @pl.when(i + 1 < n_steps)
    def _():
        for cp in x1_copies(i + 1, 1 - slot):
            cp.start()

    @pl.when(i == 0)
    def _():
        gather(dest_ref, 0)

    for parity in range(2):
        @pl.when(jnp.logical_and(i + 1 < n_steps, slot == parity))
        def _():
            gather(nxt_ref, 1 - parity)

    for k in range(TOP_K):
        pltpu.make_async_copy(ybuf_hbm.at[pl.ds(0, tm)], g_sc.at[slot, k], g_sem.at[slot]).wait()
    for cp in x1_copies(i, slot):
        cp.wait()

    for parity in range(2):
        @pl.when(slot == parity)
        def _():
            for r in range(tm):
                acc = x1_sc[parity, r]
                for k in range(TOP_K):
                    acc = acc + w_ref[k * tm + r] * g_sc[parity, k, r]
                res_sc[parity, r] = acc
    for cp in out_copies(i, slot):
        cp.start()

    @pl.when(i >= 1)
    def _():
        for cp in out_copies(i - 1, 1 - slot):
            cp.wait()

    @pl.when(i == n_steps - 1)
    def _():
        for cp in out_copies(i, slot):
            cp.wait()


def _combine(dest_flat, x1, w_flat, ybuf):
    T = x1.shape[0]
    tm = ROUTE_TM
    tile = (D_MODEL // LANES, LANES)
    return pl.pallas_call(
        _combine_kernel,
        out_shape=jax.ShapeDtypeStruct((T, D_MODEL), F32),
        grid=(T // tm,),
        in_specs=[
            pl.BlockSpec((tm * TOP_K,), lambda i: (i,), memory_space=pltpu.SMEM),
            pl.BlockSpec((tm * TOP_K,), lambda i: (jnp.minimum(i + 1, T // tm - 1),),
                         memory_space=pltpu.SMEM),
            pl.BlockSpec((tm * TOP_K,), lambda i: (i,), memory_space=pltpu.SMEM),
            pl.BlockSpec(memory_space=pl.ANY),
            pl.BlockSpec(memory_space=pl.ANY),
        ],
        out_specs=pl.BlockSpec(memory_space=pl.ANY),
        scratch_shapes=[pltpu.VMEM((2, TOP_K, tm) + tile, F32), pltpu.VMEM((2, tm) + tile, F32),
                        pltpu.VMEM((2, tm) + tile, F32), pltpu.SemaphoreType.DMA((2,)),
                        pltpu.SemaphoreType.DMA((2,)), pltpu.SemaphoreType.DMA((2,))],
        compiler_params=pltpu.CompilerParams(
            dimension_semantics=("arbitrary",), vmem_limit_bytes=VMEM_LIMIT),
        name="combine",
    )(dest_flat, dest_flat, w_flat, x1, ybuf)


def _layer(x2, mem2, B, S, g_mix, w_in, g_sgu_v, w_spatial, b_spatial, g_q_moba, g_k_moba,
           g_mem, w_mem_kv, g_q_mem, g_k_mem, w_br_sgu, w_br_moba, w_br_mem, w_out,
           g_ffn, w_router, b_router, w_gate_up, b_gate_up, w_down, b_down):
    T = B * S
    row = lambda v: v.reshape(1, -1)
    proj = _inproj(x2, row(g_mix), w_in.astype(BF16), row(g_sgu_v))
    yb = _moba(proj, g_q_moba, g_k_moba, B, S)
    kvmem = _memkv(mem2, row(g_mem), w_mem_kv.astype(BF16), row(g_k_mem), B)

    causal = jnp.tril(jnp.ones((SGU_CHUNK, SGU_CHUNK), dtype=bool))
    ws = jnp.where(causal[None], w_spatial, 0.0).astype(BF16)
    bsp = jnp.repeat(b_spatial.T, LANES, axis=1)
    x1, h2, top_e, top_w, rank, counts = _merge(
        proj, yb, x2, kvmem, ws, bsp, row(g_q_mem), w_br_sgu.astype(BF16),
        w_br_moba.astype(BF16), w_br_mem.astype(BF16), w_out.astype(BF16), row(g_ffn),
        w_router, b_router, S)

    bm = EXPERT_BM
    n_blocks = -(-(T * TOP_K + N_EXPERTS * (bm - 1)) // bm)
    cnt = counts.reshape(N_EXPERTS).astype(jnp.int32)
    padded = (cnt + bm - 1) // bm * bm
    pad_ends = jnp.cumsum(padded)
    pad_starts = pad_ends - padded
    e_ids = jnp.arange(N_EXPERTS, dtype=jnp.int32)[:, None, None]
    start_of = jnp.sum(jnp.where(top_e[None] == e_ids, pad_starts[:, None, None], 0), axis=0)
    per_tile = lambda a: a.reshape(TOP_K, T // ROUTE_TM, ROUTE_TM).transpose(1, 0, 2).reshape(T * TOP_K)
    dest = per_tile(start_of + rank)
    top_w = per_tile(top_w)
    block_row0 = jnp.arange(n_blocks, dtype=jnp.int32) * bm
    block_e = jnp.minimum(
        jnp.sum((pad_ends[None, :] <= block_row0[:, None]).astype(jnp.int32), axis=1), N_EXPERTS - 1)
    n_used = (pad_ends[-1:] // bm).astype(jnp.int32)

    xbuf = _dispatch(pad_starts + cnt, padded - cnt, dest, h2, n_blocks * bm)
    ybuf = _experts(block_e, n_used, xbuf, w_gate_up, b_gate_up.reshape(N_EXPERTS, 1, 2 * D_FF),
                    w_down, b_down.reshape(N_EXPERTS, 1, D_MODEL))
    return _combine(dest, x1, top_w, ybuf)


def kernel(x, mem, g_mix, w_in, g_sgu_v, w_spatial, b_spatial, g_q_moba, g_k_moba, g_mem, w_mem_kv, g_q_mem, g_k_mem, w_br_sgu, w_br_moba, w_br_mem, w_out, g_ffn, w_router, b_router, w_gate_up, b_gate_up, w_down, b_down):
    B, S, D = x.shape
    x2 = x.reshape(B * S, D)
    mem2 = mem.reshape(B * MEM_LEN, D)
    params = (g_mix, w_in, g_sgu_v, w_spatial, b_spatial, g_q_moba, g_k_moba, g_mem, w_mem_kv,
              g_q_mem, g_k_mem, w_br_sgu, w_br_moba, w_br_mem, w_out, g_ffn, w_router, b_router,
              w_gate_up, b_gate_up, w_down, b_down)
    for l in range(g_mix.shape[0]):
        x2 = _layer(x2, mem2, B, S, *(p[l] for p in params))
    return x2.reshape(B, S, D)
```

```python
import functools

import numpy as np
import jax
import jax.numpy as jnp
from jax import lax
from jax.experimental import pallas as pl
from jax.experimental.pallas import tpu as pltpu

F32 = jnp.float32
BF16 = jnp.bfloat16

D_MODEL = 1024
MEM_LEN = 256
SGU_GROUPS = 4
SGU_CHUNK = 128
SGU_WIDTH = 512
MOBA_HEADS = 8
MOBA_HEAD_DIM = 64
MOBA_BLOCK = 256
MOBA_TOPK = 3
MEM_HEADS = 4
MEM_HEAD_DIM = 128
SPLIT = 512
IN_WIDTH = 6144
N_SPLITS = IN_WIDTH // SPLIT
N_EXPERTS = 32
TOP_K = 4
D_FF = 1024
SWIGLU_ALPHA = 1.702
SWIGLU_LIMIT = 7.0
EPS = 1e-6
NEG_INF = -1e30
LANES = 128

VMEM_LIMIT = 56 * 1024 * 1024

INPROJ_TM = 256
MERGE_TM = 512
ROUTE_TM = 256
EXPERT_BM = 512


def _gelu(x):
    return 0.5 * x * (1.0 + lax.erf(x * np.float32(np.sqrt(0.5))))


def _rms(x, g):
    return x * lax.rsqrt(jnp.mean(x * x, axis=-1, keepdims=True) + EPS) * g


def _inproj_kernel(x_ref, gmix_ref, w_ref, gsgu_ref, o_ref):
    h = _rms(x_ref[...], gmix_ref[...]).astype(BF16)
    for c in (0, 1, 6, 7, 8, 9, 10, 11, 2, 3, 4, 5):
        sl = slice(c * SPLIT, (c + 1) * SPLIT)
        acc = jnp.dot(h, w_ref[:, sl], preferred_element_type=F32)
        if c == 0:
            val = _gelu(acc)
        elif c == 1:
            val = _rms(_gelu(acc), gsgu_ref[...])
        elif c >= 6:
            val = jax.nn.sigmoid(acc)
        else:
            val = acc
        o_ref[:, sl] = val.astype(BF16)


def _inproj(x2, g_mix, w_in, g_sgu_v):
    T = x2.shape[0]
    tm = INPROJ_TM
    return pl.pallas_call(
        _inproj_kernel,
        out_shape=jax.ShapeDtypeStruct((T, IN_WIDTH), BF16),
        grid=(T // tm,),
        in_specs=[
            pl.BlockSpec((tm, D_MODEL), lambda i: (i, 0)),
            pl.BlockSpec((1, D_MODEL), lambda i: (0, 0)),
            pl.BlockSpec((D_MODEL, IN_WIDTH), lambda i: (0, 0)),
            pl.BlockSpec((1, SGU_WIDTH), lambda i: (0, 0)),
        ],
        out_specs=pl.BlockSpec((tm, IN_WIDTH), lambda i: (i, 0)),
        compiler_params=pltpu.CompilerParams(
            dimension_semantics=("arbitrary",), vmem_limit_bytes=VMEM_LIMIT),
        name="inproj",
    )(x2, g_mix, w_in, g_sgu_v)


def _pair_norm(x, g, head0):
    sq = x * x
    s0 = jnp.sum(jnp.where(head0, sq, 0.0), axis=-1, keepdims=True)
    s1 = jnp.sum(jnp.where(head0, 0.0, sq), axis=-1, keepdims=True)
    inv = jnp.where(head0, lax.rsqrt(s0 * (1.0 / MOBA_HEAD_DIM) + EPS),
                    lax.rsqrt(s1 * (1.0 / MOBA_HEAD_DIM) + EPS))
    return x * inv * g


VT_ROWS = MOBA_HEAD_DIM + 16
LOG2E = float(np.log2(np.e))


def _moba_kernel(slopes_ref, q_ref, k_ref, v_ref, gq_ref, gk_ref, bias_ref, o_ref,
                 kn_sc, vt_sc, kmean_sc, mrow_sc, qh_sc, m_sc, acc_sc, t_sc, cmax_sc, *, nb):
    qi = pl.program_id(1)
    bs = MOBA_BLOCK
    pairs = MOBA_HEADS // 2
    head0 = lax.broadcasted_iota(jnp.int32, (1, LANES), 1) < MOBA_HEAD_DIM
    nt = (((1,), (1,)), ((), ()))

    @pl.when(qi == 0)
    def _prepare_kv():
        ones_rows = jnp.where(
            lax.broadcasted_iota(jnp.int32, (VT_ROWS - MOBA_HEAD_DIM, bs), 0) == 0, 1.0, 0.0).astype(BF16)

        def body(j, carry):
            rows = pl.ds(pl.multiple_of(j * bs, bs), bs)
            means = []
            for g in range(pairs):
                cols = slice(g * LANES, (g + 1) * LANES)
                kn = _pair_norm(k_ref[rows, cols].astype(F32), gk_ref[...], head0)
                kn_sc[j, :, cols] = kn.astype(BF16)
                means.append(jnp.mean(kn, axis=0, keepdims=True))
                vt = v_ref[rows, cols].astype(F32).T.astype(BF16)
                for hh in range(2):
                    h = 2 * g + hh
                    vt_sc[j, h, :MOBA_HEAD_DIM, :] = vt[hh * MOBA_HEAD_DIM:(hh + 1) * MOBA_HEAD_DIM]
                    vt_sc[j, h, MOBA_HEAD_DIM:, :] = ones_rows
            kmean_sc[pl.ds(j, 1), :] = jnp.concatenate(means, axis=1)
            return carry
        lax.fori_loop(0, nb, body, 0)

    blk = lax.broadcasted_iota(jnp.int32, (nb, 1), 0)
    eligible = blk < qi
    dist = ((qi - blk) * bs).astype(F32)
    for g in range(pairs):
        cols = slice(g * LANES, (g + 1) * LANES)
        qn = _pair_norm(q_ref[:, cols].astype(F32), gq_ref[...], head0)
        qpair = jnp.concatenate([jnp.where(head0, qn, 0.0), jnp.where(head0, 0.0, qn)], axis=0)
        qh_sc[g] = (qpair * (LOG2E * MOBA_HEAD_DIM ** -0.5)).astype(BF16)
        q_hi = qpair.astype(BF16)
        q_lo = (qpair - q_hi.astype(F32)).astype(BF16)
        km = kmean_sc[:, cols]
        km_hi = km.astype(BF16)
        km_lo = (km - km_hi.astype(F32)).astype(BF16)
        gate = (lax.dot_general(km_hi, q_hi, nt, preferred_element_type=F32)
                + lax.dot_general(km_hi, q_lo, nt, preferred_element_type=F32)
                + lax.dot_general(km_lo, q_hi, nt, preferred_element_type=F32))
        work = jnp.where(eligible, gate, -jnp.inf)
        sel = jnp.zeros((nb, 2 * bs), jnp.bool_)
        for _ in range(MOBA_TOPK):
            best = jnp.max(work, axis=0, keepdims=True)
            first = jnp.min(jnp.where(work == best, blk, nb), axis=0, keepdims=True)
            pick = jnp.logical_and(blk == first, best > -jnp.inf)
            sel = jnp.logical_or(sel, pick)
            work = jnp.where(pick, -jnp.inf, work)
        for hh in range(2):
            h = 2 * g + hh
            mrow_sc[:, h * bs:(h + 1) * bs] = jnp.where(
                blk == qi, 0.0,
                jnp.where(sel[:, hh * bs:(hh + 1) * bs], (-LOG2E * slopes_ref[h]) * dist, NEG_INF))
    m_sc[...] = jnp.full(m_sc.shape, NEG_INF, F32)
    acc_sc[...] = jnp.zeros(acc_sc.shape, F32)

    def score(j, own, slot):
        scores = [lax.dot_general(kn_sc[j, :, g * LANES:(g + 1) * LANES], qh_sc[g], nt,
                                  preferred_element_type=F32) for g in range(pairs)]
        t = jnp.concatenate(scores, axis=1) + bias_ref[1 if own else 0]
        t_sc[slot] = t
        cmax_sc[slot] = jnp.max(t, axis=0, keepdims=True)

    def accumulate(j, slot):
        mrow = mrow_sc[pl.ds(j, 1), :]
        m_old = m_sc[...]
        m_new = jnp.maximum(m_old, cmax_sc[slot] + mrow)
        shift = m_new - mrow
        alpha = jnp.exp2(m_old - m_new)
        m_sc[...] = m_new
        for h in range(MOBA_HEADS):
            cols = slice(h * bs, (h + 1) * bs)
            p = jnp.exp2(t_sc[slot, :, cols] - shift[:, cols]).astype(BF16)
            pv = jnp.dot(vt_sc[j, h], p, preferred_element_type=F32)
            acc_sc[h] = alpha[:, cols] * acc_sc[h] + pv

    n_past = qi
    score(qi, True, 0)

    def two_blocks(p, carry):
        score(2 * p, False, 1)
        accumulate(jnp.where(p == 0, qi, 2 * p - 1), 0)
        score(2 * p + 1, False, 0)
        accumulate(2 * p, 1)
        return carry
    lax.fori_loop(0, n_past // 2, two_blocks, 0)

    @pl.when(n_past % 2 == 0)
    def _():
        accumulate(jnp.where(n_past == 0, qi, n_past - 1), 0)

    @pl.when(n_past % 2 == 1)
    def _():
        score(n_past - 1, False, 1)
        accumulate(jnp.where(n_past == 1, qi, n_past - 2), 0)
        accumulate(n_past - 1, 1)

    for g in range(pairs):
        halves = []
        for hh in range(2):
            acc = acc_sc[2 * g + hh]
            halves.append(acc[:MOBA_HEAD_DIM] / acc[MOBA_HEAD_DIM:MOBA_HEAD_DIM + 1])
        out_t = jnp.concatenate(halves, axis=0)
        o_ref[:, g * LANES:(g + 1) * LANES] = out_t.T.astype(BF16)


def _moba_bias_tiles():
    slopes = 2.0 ** (-8.0 * np.arange(1, MOBA_HEADS + 1, dtype=np.float32) / MOBA_HEADS)
    kk = np.arange(MOBA_BLOCK)[:, None]
    qq = np.arange(MOBA_BLOCK)[None, :]
    rel = (qq - kk).astype(np.float64)
    past = -slopes.astype(np.float64)[:, None, None] * rel[None] * LOG2E
    own = np.where((kk <= qq)[None], past, NEG_INF)
    side_by_side = lambda a: a.transpose(1, 0, 2).reshape(MOBA_BLOCK, MOBA_HEADS * MOBA_BLOCK)
    return slopes.astype(np.float32), np.stack([side_by_side(past), side_by_side(own)]).astype(np.float32)


def _moba(proj, g_q, g_k, B, S):
    nb = S // MOBA_BLOCK
    width = MOBA_HEADS * MOBA_HEAD_DIM
    slopes, bias = _moba_bias_tiles()
    gq2 = jnp.tile(g_q.reshape(1, MOBA_HEAD_DIM), (1, 2))
    gk2 = jnp.tile(g_k.reshape(1, MOBA_HEAD_DIM), (1, 2))
    full = lambda shape: pl.BlockSpec(shape, lambda b, q, s: (0,) * len(shape))
    grid_spec = pltpu.PrefetchScalarGridSpec(
        num_scalar_prefetch=1,
        grid=(B, nb),
        in_specs=[
            pl.BlockSpec((MOBA_BLOCK, width), lambda b, q, s: (b * nb + q, 2)),
            pl.BlockSpec((S, width), lambda b, q, s: (b, 3)),
            pl.BlockSpec((S, width), lambda b, q, s: (b, 4)),
            full((1, LANES)),
            full((1, LANES)),
            full((2, MOBA_BLOCK, MOBA_HEADS * MOBA_BLOCK)),
        ],
        out_specs=pl.BlockSpec((MOBA_BLOCK, width), lambda b, q, s: (b * nb + q, 0)),
        scratch_shapes=[
            pltpu.VMEM((nb, MOBA_BLOCK, width), BF16),
            pltpu.VMEM((nb, MOBA_HEADS, VT_ROWS, MOBA_BLOCK), BF16),
            pltpu.VMEM((nb, width), F32),
            pltpu.VMEM((nb, MOBA_HEADS * MOBA_BLOCK), F32),
            pltpu.VMEM((MOBA_HEADS // 2, 2 * MOBA_BLOCK, LANES), BF16),
            pltpu.VMEM((1, MOBA_HEADS * MOBA_BLOCK), F32),
            pltpu.VMEM((MOBA_HEADS, VT_ROWS, MOBA_BLOCK), F32),
            pltpu.VMEM((2, MOBA_BLOCK, MOBA_HEADS * MOBA_BLOCK), F32),
            pltpu.VMEM((2, 1, MOBA_HEADS * MOBA_BLOCK), F32),
        ],
    )
    return pl.pallas_call(
        functools.partial(_moba_kernel, nb=nb),
        out_shape=jax.ShapeDtypeStruct((B * S, width), BF16),
        grid_spec=grid_spec,
        compiler_params=pltpu.CompilerParams(
            dimension_semantics=("arbitrary", "arbitrary"), vmem_limit_bytes=VMEM_LIMIT),
        name="moba",
    )(jnp.asarray(slopes), proj, proj, proj, gq2, gk2, jnp.asarray(bias))


def _memkv_kernel(mem_ref, gmem_ref, w_ref, gk_ref, o_ref):
    mn = _rms(mem_ref[...], gmem_ref[...]).astype(BF16)
    kv = jnp.dot(mn, w_ref[...], preferred_element_type=F32)
    width = MEM_HEADS * MEM_HEAD_DIM
    for h in range(MEM_HEADS):
        sl = slice(h * MEM_HEAD_DIM, (h + 1) * MEM_HEAD_DIM)
        o_ref[:, sl] = _rms(kv[:, sl], gk_ref[...]).astype(BF16)
    o_ref[:, width:] = kv[:, width:].astype(BF16)


def _memkv(mem2, g_mem, w_mem_kv, g_k_mem, B):
    width = MEM_HEADS * MEM_HEAD_DIM
    return pl.pallas_call(
        _memkv_kernel,
        out_shape=jax.ShapeDtypeStruct((B * MEM_LEN, 2 * width), BF16),
        grid=(B,),
        in_specs=[
            pl.BlockSpec((MEM_LEN, D_MODEL), lambda b: (b, 0)),
            pl.BlockSpec((1, D_MODEL), lambda b: (0, 0)),
            pl.BlockSpec((D_MODEL, 2 * width), lambda b: (0, 0)),
            pl.BlockSpec((1, MEM_HEAD_DIM), lambda b: (0, 0)),
        ],
        out_specs=pl.BlockSpec((MEM_LEN, 2 * width), lambda b: (b, 0)),
        compiler_params=pltpu.CompilerParams(
            dimension_semantics=("arbitrary",), vmem_limit_bytes=VMEM_LIMIT),
        name="memkv",
    )(mem2, g_mem, w_mem_kv, g_k_mem)


def _merge_kernel(gu_ref, vn_ref, yb_ref, qm_ref, ga_ref, gb_ref, gc_ref, x_ref, kv_ref,
                  ws_ref, bsp_ref, gqm_ref, wa_ref, wb_ref, wc_ref, wo_ref, gffn_ref,
                  wrh_ref, wrl_ref, br_ref,
                  x1_ref, h2_hbm, tope_ref, topw_ref, rank_ref, cnt_ref, run_sc, hp_sc, hp_sem):
    i = pl.program_id(0)
    tm = x_ref.shape[0]

    @pl.when(i == 0)
    def _():
        run_sc[...] = jnp.zeros_like(run_sc)

    ya_chunks = []
    for c in range(tm // SGU_CHUNK):
        rows = slice(c * SGU_CHUNK, (c + 1) * SGU_CHUNK)
        groups = []
        for g in range(SGU_GROUPS):
            cols = slice(g * LANES, (g + 1) * LANES)
            groups.append(jnp.dot(ws_ref[g], vn_ref[rows, cols], preferred_element_type=F32))
        mixed = jnp.concatenate(groups, axis=1) + bsp_ref[...]
        ya_chunks.append(gu_ref[rows, :].astype(F32) * mixed)
    y_a = jnp.concatenate(ya_chunks, axis=0).astype(BF16)

    width = MEM_HEADS * MEM_HEAD_DIM
    yc_heads = []
    for h in range(MEM_HEADS):
        cols = slice(h * MEM_HEAD_DIM, (h + 1) * MEM_HEAD_DIM)
        qn = _rms(qm_ref[:, cols].astype(F32), gqm_ref[...]).astype(BF16)
        s = lax.dot_general(qn, kv_ref[:, cols], (((1,), (1,)), ((), ())),
                            preferred_element_type=F32) * (MEM_HEAD_DIM ** -0.5)
        s = s - jnp.max(s, axis=-1, keepdims=True)
        p = jnp.exp(s)
        p = p / jnp.sum(p, axis=-1, keepdims=True)
        vcols = slice(width + h * MEM_HEAD_DIM, width + (h + 1) * MEM_HEAD_DIM)
        yc_heads.append(jnp.dot(p.astype(BF16), kv_ref[:, vcols], preferred_element_type=F32))
    y_c = jnp.concatenate(yc_heads, axis=1).astype(BF16)

    merged = ga_ref[...].astype(F32) * jnp.dot(y_a, wa_ref[...], preferred_element_type=F32)
    merged = merged + gb_ref[...].astype(F32) * jnp.dot(yb_ref[...], wb_ref[...],
                                                       preferred_element_type=F32)
    merged = merged + gc_ref[...].astype(F32) * jnp.dot(y_c, wc_ref[...],
                                                       preferred_element_type=F32)
    x1 = x_ref[...] + jnp.dot(merged.astype(BF16), wo_ref[...], preferred_element_type=F32)
    x1_ref[...] = x1
    h2 = _rms(x1, gffn_ref[...])
    h2_hi = h2.astype(BF16)
    h2_hi32 = h2_hi.astype(F32)
    h2_lo = (h2 - h2_hi32).astype(BF16)
    bits = lax.bitcast_convert_type(h2_hi32, jnp.uint32)
    half = D_MODEL // 2
    slot = i % 2
    hp_sc[slot] = (bits[:, :half] >> 16) | (bits[:, half:] & jnp.uint32(0xFFFF0000))

    def h2_copies(step, s):
        rows = pl.ds(pl.multiple_of(step * tm, tm), tm)
        return [pltpu.make_async_copy(hp_sc.at[s, :, c * LANES:(c + 1) * LANES], h2_hbm.at[rows, c],
                                      hp_sem.at[s]) for c in range(half // LANES)]

    for cp in h2_copies(i, slot):
        cp.start()

    nt = (((1,), (1,)), ((), ()))
    logits = (lax.dot_general(wrh_ref[...], h2_hi, nt, preferred_element_type=F32)
              + lax.dot_general(wrh_ref[...], h2_lo, nt, preferred_element_type=F32)
              + lax.dot_general(wrl_ref[...], h2_hi, nt, preferred_element_type=F32)
              + br_ref[...])
    row = lax.broadcasted_iota(jnp.int32, (N_EXPERTS, tm), 0)
    work = logits
    top_e, top_l = [], []
    for _ in range(TOP_K):
        mx = jnp.max(work, axis=0, keepdims=True)
        idx = jnp.min(jnp.where(work == mx, row, N_EXPERTS), axis=0, keepdims=True)
        top_e.append(idx)
        top_l.append(mx)
        work = jnp.where(row == idx, -jnp.inf, work)
    exps = [jnp.exp(v - top_l[0]) for v in top_l]
    denom = exps[0] + exps[1] + exps[2] + exps[3]

    onehots = [row == e for e in top_e]
    multi = jnp.zeros((N_EXPERTS, tm), F32)
    for oh in onehots:
        multi = multi + jnp.where(oh, 1.0, 0.0)
    r_i = lax.broadcasted_iota(jnp.int32, (tm, tm), 0)
    c_i = lax.broadcasted_iota(jnp.int32, (tm, tm), 1)
    earlier = jnp.where(r_i < c_i, 1.0, 0.0).astype(BF16)
    pos = run_sc[...] + jnp.dot(multi.astype(BF16), earlier, preferred_element_type=F32)
    run_sc[...] = run_sc[...] + jnp.sum(multi, axis=1, keepdims=True)
    cnt_ref[...] = run_sc[...]

    for k in range(TOP_K):
        tope_ref[k:k + 1, :] = top_e[k]
        topw_ref[k:k + 1, :] = exps[k] / denom
        rank_ref[k:k + 1, :] = jnp.sum(jnp.where(onehots[k], pos, 0.0), axis=0,
                                       keepdims=True).astype(jnp.int32)

    @pl.when(i >= 1)
    def _():
        for cp in h2_copies(i - 1, 1 - slot):
            cp.wait()

    @pl.when(i == pl.num_programs(0) - 1)
    def _():
        for cp in h2_copies(i, slot):
            cp.wait()


def _merge(proj, yb, x2, kvmem, ws, bsp, g_q_mem, wa, wb, wc, wo, g_ffn, w_router, b_router, S):
    T = x2.shape[0]
    tm = MERGE_TM
    per_b = S // tm
    full = lambda shape: pl.BlockSpec(shape, lambda i: (0,) * len(shape))
    in_specs = [
        pl.BlockSpec((tm, SPLIT), lambda i: (i, 0)),
        pl.BlockSpec((tm, SPLIT), lambda i: (i, 1)),
        pl.BlockSpec((tm, SPLIT), lambda i: (i, 0)),
        pl.BlockSpec((tm, SPLIT), lambda i: (i, 5)),
        pl.BlockSpec((tm, D_MODEL), lambda i: (i, 3)),
        pl.BlockSpec((tm, D_MODEL), lambda i: (i, 4)),
        pl.BlockSpec((tm, D_MODEL), lambda i: (i, 5)),
        pl.BlockSpec((tm, D_MODEL), lambda i: (i, 0)),
        pl.BlockSpec((MEM_LEN, 2 * MEM_HEADS * MEM_HEAD_DIM), lambda i: (i // per_b, 0)),
        full((SGU_GROUPS, SGU_CHUNK, SGU_CHUNK)),
        full((SGU_CHUNK, SGU_WIDTH)),
        full((1, MEM_HEAD_DIM)),
        full((SGU_WIDTH, D_MODEL)),
        full((SGU_WIDTH, D_MODEL)),
        full((SGU_WIDTH, D_MODEL)),
        full((D_MODEL, D_MODEL)),
        full((1, D_MODEL)),
        full((N_EXPERTS, D_MODEL)),
        full((N_EXPERTS, D_MODEL)),
        full((N_EXPERTS, 1)),
    ]
    out_shape = (
        jax.ShapeDtypeStruct((T, D_MODEL), F32),
        jax.ShapeDtypeStruct((T, D_MODEL // 2 // LANES, LANES), jnp.uint32),
        jax.ShapeDtypeStruct((TOP_K, T), jnp.int32),
        jax.ShapeDtypeStruct((TOP_K, T), F32),
        jax.ShapeDtypeStruct((TOP_K, T), jnp.int32),
        jax.ShapeDtypeStruct((N_EXPERTS, 1), F32),
    )
    out_specs = (
        pl.BlockSpec((tm, D_MODEL), lambda i: (i, 0)),
        pl.BlockSpec(memory_space=pl.ANY),
        pl.BlockSpec((TOP_K, tm), lambda i: (0, i)),
        pl.BlockSpec((TOP_K, tm), lambda i: (0, i)),
        pl.BlockSpec((TOP_K, tm), lambda i: (0, i)),
        pl.BlockSpec((N_EXPERTS, 1), lambda i: (0, 0)),
    )
    wr_t = w_router.T
    wr_hi = wr_t.astype(BF16)
    wr_lo = (wr_t - wr_hi.astype(F32)).astype(BF16)
    return pl.pallas_call(
        _merge_kernel,
        out_shape=out_shape,
        grid=(T // tm,),
        in_specs=in_specs,
        out_specs=out_specs,
        scratch_shapes=[pltpu.VMEM((N_EXPERTS, 1), F32),
                        pltpu.VMEM((2, tm, D_MODEL // 2), jnp.uint32), pltpu.SemaphoreType.DMA((2,))],
        compiler_params=pltpu.CompilerParams(
            dimension_semantics=("arbitrary",), vmem_limit_bytes=VMEM_LIMIT),
        name="merge",
    )(proj, proj, yb, proj, proj, proj, proj, x2, kvmem, ws, bsp, g_q_mem,
      wa, wb, wc, wo, g_ffn, wr_hi, wr_lo, b_router.reshape(N_EXPERTS, 1))


def _dispatch_kernel(padlo_ref, padlen_ref, dest_ref, h_hbm, xbuf_ref, sem, z_sc, z_sem, h_sc, in_sem):
    i = pl.program_id(0)
    tm = h_sc.shape[1]

    @pl.when(i == 0)
    def _zero_padding_rows():
        z_sc[...] = jnp.zeros(z_sc.shape, z_sc.dtype)
        bits = range(EXPERT_BM.bit_length() - 2, -1, -1)

        def pieces(e):
            off, length = padlo_ref[e], padlen_ref[e]
            for b in bits:
                take = ((length >> b) & 1) == 1
                yield take, pltpu.make_async_copy(z_sc.at[pl.ds(0, 1 << b)],
                                                  xbuf_ref.at[pl.ds(off, 1 << b)], z_sem)
                off = off + jnp.where(take, 1 << b, 0)

        for e in range(N_EXPERTS):
            for take, cp in pieces(e):
                pl.when(take)(cp.start)
        for e in range(N_EXPERTS):
            for take, cp in pieces(e):
                pl.when(take)(cp.wait)

        zrows = z_sc.shape[0]
        first = (padlo_ref[N_EXPERTS - 1] + padlen_ref[N_EXPERTS - 1]) // zrows
        def tail_copy(j):
            return pltpu.make_async_copy(z_sc, xbuf_ref.at[pl.ds(pl.multiple_of(j * zrows, zrows), zrows)], z_sem)
        lax.fori_loop(first, xbuf_ref.shape[0] // zrows, lambda j, c: (tail_copy(j).start(), c)[1], 0)
        lax.fori_loop(first, xbuf_ref.shape[0] // zrows, lambda j, c: (tail_copy(j).wait(), c)[1], 0)

    n_steps = pl.num_programs(0)
    nbuf = h_sc.shape[0]

    def fetch(step, s):
        rows = pl.ds(pl.multiple_of(step * tm, tm), tm)
        return pltpu.make_async_copy(h_hbm.at[rows], h_sc.at[s], in_sem.at[s])

    def drain(s):
        for _ in range(TOP_K):
            pltpu.make_async_copy(h_sc.at[s], xbuf_ref.at[pl.ds(0, tm)], sem.at[s]).wait()

    @pl.when(i == 0)
    def _():
        fetch(0, 0).start()

    @pl.when(i + 1 < n_steps)
    def _():
        fetch(i + 1, (i + 1) % nbuf).start()

    fetch(i, i % nbuf).wait()
    for s in range(nbuf):
        @pl.when(i % nbuf == s)
        def _():
            for r in range(tm):
                for k in range(TOP_K):
                    pltpu.make_async_copy(h_sc.at[s, r], xbuf_ref.at[dest_ref[k * tm + r]],
                                          sem.at[s]).start(priority=k % 2)

            @pl.when(i >= 1)
            def _():
                drain((s - 1) % nbuf)

            @pl.when(i == n_steps - 1)
            def _():
                drain(s)


def _dispatch(pad_lo, pad_len, dest_flat, h2, n_rows):
    T, slabs, lanes = h2.shape
    tm = ROUTE_TM
    grid_spec = pltpu.PrefetchScalarGridSpec(
        num_scalar_prefetch=2,
        grid=(T // tm,),
        in_specs=[
            pl.BlockSpec((tm * TOP_K,), lambda i, lo, ln: (i,), memory_space=pltpu.SMEM),
            pl.BlockSpec(memory_space=pl.ANY),
        ],
        out_specs=pl.BlockSpec(memory_space=pl.ANY),
        scratch_shapes=[pltpu.SemaphoreType.DMA((3,)), pltpu.VMEM((EXPERT_BM // 2, slabs, lanes), h2.dtype),
                        pltpu.SemaphoreType.DMA, pltpu.VMEM((3, tm, slabs, lanes), h2.dtype),
                        pltpu.SemaphoreType.DMA((3,))],
    )
    return pl.pallas_call(
        _dispatch_kernel,
        out_shape=jax.ShapeDtypeStruct((n_rows, slabs, lanes), h2.dtype),
        grid_spec=grid_spec,
        compiler_params=pltpu.CompilerParams(
            dimension_semantics=("arbitrary",), vmem_limit_bytes=VMEM_LIMIT),
        name="dispatch",
    )(pad_lo, pad_len, dest_flat, h2)


FF_CHUNK = 512


def _expert_kernel(be_ref, nused_ref, x_hbm, w1_ref, b1_ref, w2_ref, b2_ref,
                   y_hbm, w1_sc, w2_sc, x_sc, x_sem, y_sc, y_sem):
    i = pl.program_id(0)
    bm = x_sc.shape[1]
    n_used = nused_ref[0]

    def x_copies(blk, slot):
        rows = pl.ds(pl.multiple_of(blk * bm, bm), bm)
        return [pltpu.make_async_copy(x_hbm.at[rows, c], x_sc.at[slot, :, c * LANES:(c + 1) * LANES],
                                      x_sem.at[slot]) for c in range(x_hbm.shape[1])]

    def y_copies(blk, slot):
        rows = pl.ds(pl.multiple_of(blk * bm, bm), bm)
        return [pltpu.make_async_copy(y_sc.at[slot, :, c * LANES:(c + 1) * LANES], y_hbm.at[rows, c],
                                      y_sem.at[slot]) for c in range(y_hbm.shape[1])]

    @pl.when(i < n_used)
    def _():
        slot = i % 2

        @pl.when(i == 0)
        def _():
            for cp in x_copies(0, 0):
                cp.start()

        @pl.when(i + 1 < n_used)
        def _():
            for cp in x_copies(i + 1, 1 - slot):
                cp.start()

        @pl.when(jnp.logical_or(i == 0, be_ref[i] != be_ref[jnp.maximum(i - 1, 0)]))
        def _():
            w1_sc[...] = w1_ref[0].astype(BF16)
            w2_sc[...] = w2_ref[0].astype(BF16)

        for cp in x_copies(i, slot):
            cp.wait()
        packed = x_sc[slot]
        lo = lax.bitcast_convert_type(packed << 16, F32)
        hi = lax.bitcast_convert_type(packed & jnp.uint32(0xFFFF0000), F32)
        xb = jnp.concatenate([lo, hi], axis=1).astype(BF16)
        y = b2_ref[0]
        for c in range(D_FF // FF_CHUNK):
            g_cols = slice(c * FF_CHUNK, (c + 1) * FF_CHUNK)
            l_cols = slice(D_FF + c * FF_CHUNK, D_FF + (c + 1) * FF_CHUNK)
            glu = jnp.dot(xb, w1_sc[:, g_cols], preferred_element_type=F32) + b1_ref[0, :, g_cols]
            lin = jnp.dot(xb, w1_sc[:, l_cols], preferred_element_type=F32) + b1_ref[0, :, l_cols]
            glu = jnp.minimum(glu, SWIGLU_LIMIT)
            lin = jnp.clip(lin, -SWIGLU_LIMIT, SWIGLU_LIMIT)
            act = glu * jax.nn.sigmoid(SWIGLU_ALPHA * glu) * (lin + 1.0)
            y = y + jnp.dot(act.astype(BF16), w2_sc[g_cols, :], preferred_element_type=F32)
        y_sc[slot] = y
        for cp in y_copies(i, slot):
            cp.start()

        @pl.when(i >= 1)
        def _():
            for cp in y_copies(i - 1, 1 - slot):
                cp.wait()

        @pl.when(i == n_used - 1)
        def _():
            for cp in y_copies(i, slot):
                cp.wait()

    @pl.when(i >= n_used)
    def _():
        @pl.when(i == n_used)
        def _():
            y_sc[0] = jnp.zeros(y_sc.shape[1:], F32)

        for cp in y_copies(i, 0):
            cp.start()

        @pl.when(i == pl.num_programs(0) - 1)
        def _():
            def drain(j, carry):
                for cp in y_copies(j, 0):
                    cp.wait()
                return carry
            lax.fori_loop(n_used, pl.num_programs(0), drain, 0)


def _experts(block_e, n_used, xbuf, w1, b1, w2, b2):
    n_rows, slabs, lanes = xbuf.shape
    bm = EXPERT_BM
    grid_spec = pltpu.PrefetchScalarGridSpec(
        num_scalar_prefetch=2,
        grid=(n_rows // bm,),
        in_specs=[
            pl.BlockSpec(memory_space=pl.ANY),
            pl.BlockSpec((1, D_MODEL, 2 * D_FF), lambda i, be, nu: (be[i], 0, 0)),
            pl.BlockSpec((1, 1, 2 * D_FF), lambda i, be, nu: (be[i], 0, 0)),
            pl.BlockSpec((1, D_FF, D_MODEL), lambda i, be, nu: (be[i], 0, 0)),
            pl.BlockSpec((1, 1, D_MODEL), lambda i, be, nu: (be[i], 0, 0)),
        ],
        out_specs=pl.BlockSpec(memory_space=pl.ANY),
        scratch_shapes=[pltpu.VMEM((D_MODEL, 2 * D_FF), BF16), pltpu.VMEM((D_FF, D_MODEL), BF16),
                        pltpu.VMEM((2, bm, slabs * lanes), jnp.uint32), pltpu.SemaphoreType.DMA((2,)),
                        pltpu.VMEM((2, bm, D_MODEL), F32), pltpu.SemaphoreType.DMA((2,))],
    )
    return pl.pallas_call(
        _expert_kernel,
        out_shape=jax.ShapeDtypeStruct((n_rows, D_MODEL // LANES, LANES), F32),
        grid_spec=grid_spec,
        compiler_params=pltpu.CompilerParams(
            dimension_semantics=("arbitrary",), vmem_limit_bytes=VMEM_LIMIT),
        name="experts",
    )(block_e, n_used, xbuf, w1, b1, w2, b2)


def _combine_kernel(dest_ref, nxt_ref, w_ref, x1_hbm, ybuf_hbm, out_hbm, g_sc, x1_sc, res_sc,
                    g_sem, x_sem, o_sem):
    i = pl.program_id(0)
    n_steps = pl.num_programs(0)
    tm = g_sc.shape[2]
    chunks = D_MODEL // LANES
    slot = i % 2

    def gather(idx_ref, s):
        for r in range(tm):
            for k in range(TOP_K):
                pltpu.make_async_copy(ybuf_hbm.at[idx_ref[k * tm + r]], g_sc.at[s, k, r],
                                      g_sem.at[s]).start(priority=k % 2)

    def x1_copies(step, s):
        rows = pl.ds(pl.multiple_of(step * tm, tm), tm)
        return [pltpu.make_async_copy(x1_hbm.at[rows, c * LANES:(c + 1) * LANES], x1_sc.at[s, :, c, :],
                                      x_sem.at[s]) for c in range(chunks)]

    def out_copies(step, s):
        rows = pl.ds(pl.multiple_of(step * tm, tm), tm)
        return [pltpu.make_async_copy(res_sc.at[s, :, c, :], out_hbm.at[rows, c * LANES:(c + 1) * LANES],
                                      o_sem.at[s]) for c in range(chunks)]

    @pl.when(i == 0)
    def _():
        for cp in x1_copies(0, 0):
            cp.start()

    @pl.when(i + 1 < n_steps)
    def _():
        for cp in x1_copies(i + 1, 1 - slot):
            cp.start()

    @pl.when(i == 0)
    def _():
        gather(dest_ref, 0)

    for parity in range(2):
        @pl.when(jnp.logical_and(i + 1 < n_steps, slot == parity))
        def _():
            gather(nxt_ref, 1 - parity)

    for k in range(TOP_K):
        pltpu.make_async_copy(ybuf_hbm.at[pl.ds(0, tm)], g_sc.at[slot, k], g_sem.at[slot]).wait()
    for cp in x1_copies(i, slot):
        cp.wait()

    for parity in range(2):
        @pl.when(slot == parity)
        def _():
            for r in range(tm):
                acc = x1_sc[parity, r]
                for k in range(TOP_K):
                    acc = acc + w_ref[k * tm + r] * g_sc[parity, k, r]
                res_sc[parity, r] = acc
    for cp in out_copies(i, slot):
        cp.start()

    @pl.when(i >= 1)
    def _():
        for cp in out_copies(i - 1, 1 - slot):
            cp.wait()

    @pl.when(i == n_steps - 1)
    def _():
        for cp in out_copies(i, slot):
            cp.wait()


def _combine(dest_flat, x1, w_flat, ybuf):
    T = x1.shape[0]
    tm = ROUTE_TM
    tile = (D_MODEL // LANES, LANES)
    return pl.pallas_call(
        _combine_kernel,
        out_shape=jax.ShapeDtypeStruct((T, D_MODEL), F32),
        grid=(T // tm,),
        in_specs=[
            pl.BlockSpec((tm * TOP_K,), lambda i: (i,), memory_space=pltpu.SMEM),
            pl.BlockSpec((tm * TOP_K,), lambda i: (jnp.minimum(i + 1, T // tm - 1),),
                         memory_space=pltpu.SMEM),
            pl.BlockSpec((tm * TOP_K,), lambda i: (i,), memory_space=pltpu.SMEM),
            pl.BlockSpec(memory_space=pl.ANY),
            pl.BlockSpec(memory_space=pl.ANY),
        ],
        out_specs=pl.BlockSpec(memory_space=pl.ANY),
        scratch_shapes=[pltpu.VMEM((2, TOP_K, tm) + tile, F32), pltpu.VMEM((2, tm) + tile, F32),
                        pltpu.VMEM((2, tm) + tile, F32), pltpu.SemaphoreType.DMA((2,)),
                        pltpu.SemaphoreType.DMA((2,)), pltpu.SemaphoreType.DMA((2,))],
        compiler_params=pltpu.CompilerParams(
            dimension_semantics=("arbitrary",), vmem_limit_bytes=VMEM_LIMIT),
        name="combine",
    )(dest_flat, dest_flat, w_flat, x1, ybuf)


def _layer(x2, mem2, B, S, g_mix, w_in, g_sgu_v, w_spatial, b_spatial, g_q_moba, g_k_moba,
           g_mem, w_mem_kv, g_q_mem, g_k_mem, w_br_sgu, w_br_moba, w_br_mem, w_out,
           g_ffn, w_router, b_router, w_gate_up, b_gate_up, w_down, b_down):
    T = B * S
    row = lambda v: v.reshape(1, -1)
    proj = _inproj(x2, row(g_mix), w_in.astype(BF16), row(g_sgu_v))
    yb = _moba(proj, g_q_moba, g_k_moba, B, S)
    kvmem = _memkv(mem2, row(g_mem), w_mem_kv.astype(BF16), row(g_k_mem), B)

    causal = jnp.tril(jnp.ones((SGU_CHUNK, SGU_CHUNK), dtype=bool))
    ws = jnp.where(causal[None], w_spatial, 0.0).astype(BF16)
    bsp = jnp.repeat(b_spatial.T, LANES, axis=1)
    x1, h2, top_e, top_w, rank, counts = _merge(
        proj, yb, x2, kvmem, ws, bsp, row(g_q_mem), w_br_sgu.astype(BF16),
        w_br_moba.astype(BF16), w_br_mem.astype(BF16), w_out.astype(BF16), row(g_ffn),
        w_router, b_router, S)

    bm = EXPERT_BM
    n_blocks = -(-(T * TOP_K + N_EXPERTS * (bm - 1)) // bm)
    cnt = counts.reshape(N_EXPERTS).astype(jnp.int32)
    padded = (cnt + bm - 1) // bm * bm
    pad_ends = jnp.cumsum(padded)
    pad_starts = pad_ends - padded
    e_ids = jnp.arange(N_EXPERTS, dtype=jnp.int32)[:, None, None]
    start_of = jnp.sum(jnp.where(top_e[None] == e_ids, pad_starts[:, None, None], 0), axis=0)
    per_tile = lambda a: a.reshape(TOP_K, T // ROUTE_TM, ROUTE_TM).transpose(1, 0, 2).reshape(T * TOP_K)
    dest = per_tile(start_of + rank)
    top_w = per_tile(top_w)
    block_row0 = jnp.arange(n_blocks, dtype=jnp.int32) * bm
    block_e = jnp.minimum(
        jnp.sum((pad_ends[None, :] <= block_row0[:, None]).astype(jnp.int32), axis=1), N_EXPERTS - 1)
    n_used = (pad_ends[-1:] // bm).astype(jnp.int32)

    xbuf = _dispatch(pad_starts + cnt, padded - cnt, dest, h2, n_blocks * bm)
    ybuf = _experts(block_e, n_used, xbuf, w_gate_up, b_gate_up.reshape(N_EXPERTS, 1, 2 * D_FF),
                    w_down, b_down.reshape(N_EXPERTS, 1, D_MODEL))
    return _combine(dest, x1, top_w, ybuf)


def kernel(x, mem, g_mix, w_in, g_sgu_v, w_spatial, b_spatial, g_q_moba, g_k_moba, g_mem, w_mem_kv, g_q_mem, g_k_mem, w_br_sgu, w_br_moba, w_br_mem, w_out, g_ffn, w_router, b_router, w_gate_up, b_gate_up, w_down, b_down):
    B, S, D = x.shape
    x2 = x.reshape(B * S, D)
    mem2 = mem.reshape(B * MEM_LEN, D)
    params = (g_mix, w_in, g_sgu_v, w_spatial, b_spatial, g_q_moba, g_k_moba, g_mem, w_mem_kv,
              g_q_mem, g_k_mem, w_br_sgu, w_br_moba, w_br_mem, w_out, g_ffn, w_router, b_router,
              w_gate_up, b_gate_up, w_down, b_down)
    for l in range(g_mix.shape[0]):
        x2 = _layer(x2, mem2, B, S, *(p[l] for p in params))
    return x2.reshape(B, S, D)
```

```python
import functools

import numpy as np
import jax
import jax.numpy as jnp
from jax import lax
from jax.experimental import pallas as pl
from jax.experimental.pallas import tpu as pltpu

F32 = jnp.float32
BF16 = jnp.bfloat16

D_MODEL = 1024
MEM_LEN = 256
SGU_GROUPS = 4
SGU_CHUNK = 128
SGU_WIDTH = 512
MOBA_HEADS = 8
MOBA_HEAD_DIM = 64
MOBA_BLOCK = 256
MOBA_TOPK = 3
MEM_HEADS = 4
MEM_HEAD_DIM = 128
SPLIT = 512
IN_WIDTH = 6144
N_SPLITS = IN_WIDTH // SPLIT
N_EXPERTS = 32
TOP_K = 4
D_FF = 1024
SWIGLU_ALPHA = 1.702
SWIGLU_LIMIT = 7.0
EPS = 1e-6
NEG_INF = -1e30
LANES = 128

VMEM_LIMIT = 56 * 1024 * 1024

INPROJ_TM = 256
MERGE_TM = 512
ROUTE_TM = 256
EXPERT_BM = 512


def _gelu(x):
    return 0.5 * x * (1.0 + lax.erf(x * np.float32(np.sqrt(0.5))))


def _rms(x, g):
    return x * lax.rsqrt(jnp.mean(x * x, axis=-1, keepdims=True) + EPS) * g


def _inproj_kernel(x_ref, gmix_ref, w_ref, gsgu_ref, o_ref):
    h = _rms(x_ref[...], gmix_ref[...]).astype(BF16)
    for c in (0, 1, 6, 7, 8, 9, 10, 11, 2, 3, 4, 5):
        sl = slice(c * SPLIT, (c + 1) * SPLIT)
        acc = jnp.dot(h, w_ref[:, sl], preferred_element_type=F32)
        if c == 0:
            val = _gelu(acc)
        elif c == 1:
            val = _rms(_gelu(acc), gsgu_ref[...])
        elif c >= 6:
            val = jax.nn.sigmoid(acc)
        else:
            val = acc
        o_ref[:, sl] = val.astype(BF16)


def _inproj(x2, g_mix, w_in, g_sgu_v):
    T = x2.shape[0]
    tm = INPROJ_TM
    return pl.pallas_call(
        _inproj_kernel,
        out_shape=jax.ShapeDtypeStruct((T, IN_WIDTH), BF16),
        grid=(T // tm,),
        in_specs=[
            pl.BlockSpec((tm, D_MODEL), lambda i: (i, 0)),
            pl.BlockSpec((1, D_MODEL), lambda i: (0, 0)),
            pl.BlockSpec((D_MODEL, IN_WIDTH), lambda i: (0, 0)),
            pl.BlockSpec((1, SGU_WIDTH), lambda i: (0, 0)),
        ],
        out_specs=pl.BlockSpec((tm, IN_WIDTH), lambda i: (i, 0)),
        compiler_params=pltpu.CompilerParams(
            dimension_semantics=("arbitrary",), vmem_limit_bytes=VMEM_LIMIT),
        name="inproj",
    )(x2, g_mix, w_in, g_sgu_v)


def _pair_norm(x, g, head0):
    sq = x * x
    s0 = jnp.sum(jnp.where(head0, sq, 0.0), axis=-1, keepdims=True)
    s1 = jnp.sum(jnp.where(head0, 0.0, sq), axis=-1, keepdims=True)
    inv = jnp.where(head0, lax.rsqrt(s0 * (1.0 / MOBA_HEAD_DIM) + EPS),
                    lax.rsqrt(s1 * (1.0 / MOBA_HEAD_DIM) + EPS))
    return x * inv * g


VT_ROWS = MOBA_HEAD_DIM + 16
LOG2E = float(np.log2(np.e))


def _moba_kernel(slopes_ref, q_ref, k_ref, v_ref, gq_ref, gk_ref, bias_ref, o_ref,
                 kn_sc, vt_sc, kmean_sc, mrow_sc, qh_sc, m_sc, acc_sc, t_sc, cmax_sc, *, nb):
    qi = pl.program_id(1)
    bs = MOBA_BLOCK
    pairs = MOBA_HEADS // 2
    head0 = lax.broadcasted_iota(jnp.int32, (1, LANES), 1) < MOBA_HEAD_DIM
    nt = (((1,), (1,)), ((), ()))

    @pl.when(qi == 0)
    def _prepare_kv():
        ones_rows = jnp.where(
            lax.broadcasted_iota(jnp.int32, (VT_ROWS - MOBA_HEAD_DIM, bs), 0) == 0, 1.0, 0.0).astype(BF16)

        def body(j, carry):
            rows = pl.ds(pl.multiple_of(j * bs, bs), bs)
            means = []
            for g in range(pairs):
                cols = slice(g * LANES, (g + 1) * LANES)
                kn = _pair_norm(k_ref[rows, cols].astype(F32), gk_ref[...], head0)
                kn_sc[j, :, cols] = kn.astype(BF16)
                means.append(jnp.mean(kn, axis=0, keepdims=True))
                vt = v_ref[rows, cols].T
                for hh in range(2):
                    h = 2 * g + hh
                    vt_sc[j, h, :MOBA_HEAD_DIM, :] = vt[hh * MOBA_HEAD_DIM:(hh + 1) * MOBA_HEAD_DIM]
                    vt_sc[j, h, MOBA_HEAD_DIM:, :] = ones_rows
            kmean_sc[pl.ds(j, 1), :] = jnp.concatenate(means, axis=1)
            return carry
        lax.fori_loop(0, nb, body, 0)

    blk = lax.broadcasted_iota(jnp.int32, (nb, 1), 0)
    eligible = blk < qi
    dist = ((qi - blk) * bs).astype(F32)
    for g in range(pairs):
        cols = slice(g * LANES, (g + 1) * LANES)
        qn = _pair_norm(q_ref[:, cols].astype(F32), gq_ref[...], head0)
        qpair = jnp.concatenate([jnp.where(head0, qn, 0.0), jnp.where(head0, 0.0, qn)], axis=0)
        qh_sc[g] = (qpair * (LOG2E * MOBA_HEAD_DIM ** -0.5)).astype(BF16)
        q_hi = qpair.astype(BF16)
        q_lo = (qpair - q_hi.astype(F32)).astype(BF16)
        km = kmean_sc[:, cols]
        km_hi = km.astype(BF16)
        km_lo = (km - km_hi.astype(F32)).astype(BF16)
        gate = (lax.dot_general(km_hi, q_hi, nt, preferred_element_type=F32)
                + lax.dot_general(km_hi, q_lo, nt, preferred_element_type=F32)
                + lax.dot_general(km_lo, q_hi, nt, preferred_element_type=F32))
        work = jnp.where(eligible, gate, -jnp.inf)
        sel = jnp.zeros((nb, 2 * bs), jnp.bool_)
        for _ in range(MOBA_TOPK):
            best = jnp.max(work, axis=0, keepdims=True)
            first = jnp.min(jnp.where(work == best, blk, nb), axis=0, keepdims=True)
            pick = jnp.logical_and(blk == first, best > -jnp.inf)
            sel = jnp.logical_or(sel, pick)
            work = jnp.where(pick, -jnp.inf, work)
        for hh in range(2):
            h = 2 * g + hh
            mrow_sc[:, h * bs:(h + 1) * bs] = jnp.where(
                blk == qi, 0.0,
                jnp.where(sel[:, hh * bs:(hh + 1) * bs], (-LOG2E * slopes_ref[h]) * dist, NEG_INF))
    m_sc[...] = jnp.full(m_sc.shape, NEG_INF, F32)
    acc_sc[...] = jnp.zeros(acc_sc.shape, F32)

    def score(j, own, slot):
        scores = [lax.dot_general(kn_sc[j, :, g * LANES:(g + 1) * LANES], qh_sc[g], nt,
                                  preferred_element_type=F32) for g in range(pairs)]
        t = jnp.concatenate(scores, axis=1) + bias_ref[1 if own else 0]
        t_sc[slot] = t
        cmax_sc[slot] = jnp.max(t, axis=0, keepdims=True)

    def accumulate(j, slot):
        mrow = mrow_sc[pl.ds(j, 1), :]
        m_old = m_sc[...]
        m_new = jnp.maximum(m_old, cmax_sc[slot] + mrow)
        shift = m_new - mrow
        alpha = jnp.exp2(m_old - m_new)
        m_sc[...] = m_new
        for h in range(MOBA_HEADS):
            cols = slice(h * bs, (h + 1) * bs)
            p = jnp.exp2(t_sc[slot, :, cols] - shift[:, cols]).astype(BF16)
            pv = jnp.dot(vt_sc[j, h], p, preferred_element_type=F32)
            acc_sc[h] = alpha[:, cols] * acc_sc[h] + pv

    n_past = qi
    score(qi, True, 0)

    def two_blocks(p, carry):
        score(2 * p, False, 1)
        accumulate(jnp.where(p == 0, qi, 2 * p - 1), 0)
        score(2 * p + 1, False, 0)
        accumulate(2 * p, 1)
        return carry
    lax.fori_loop(0, n_past // 2, two_blocks, 0)

    @pl.when(n_past % 2 == 0)
    def _():
        accumulate(jnp.where(n_past == 0, qi, n_past - 1), 0)

    @pl.when(n_past % 2 == 1)
    def _():
        score(n_past - 1, False, 1)
        accumulate(jnp.where(n_past == 1, qi, n_past - 2), 0)
        accumulate(n_past - 1, 1)

    for g in range(pairs):
        halves = []
        for hh in range(2):
            acc = acc_sc[2 * g + hh]
            halves.append(acc[:MOBA_HEAD_DIM] / acc[MOBA_HEAD_DIM:MOBA_HEAD_DIM + 1])
        out_t = jnp.concatenate(halves, axis=0)
        o_ref[:, g * LANES:(g + 1) * LANES] = out_t.T.astype(BF16)


def _moba_bias_tiles():
    slopes = 2.0 ** (-8.0 * np.arange(1, MOBA_HEADS + 1, dtype=np.float32) / MOBA_HEADS)
    kk = np.arange(MOBA_BLOCK)[:, None]
    qq = np.arange(MOBA_BLOCK)[None, :]
    rel = (qq - kk).astype(np.float64)
    past = -slopes.astype(np.float64)[:, None, None] * rel[None] * LOG2E
    own = np.where((kk <= qq)[None], past, NEG_INF)
    side_by_side = lambda a: a.transpose(1, 0, 2).reshape(MOBA_BLOCK, MOBA_HEADS * MOBA_BLOCK)
    return slopes.astype(np.float32), np.stack([side_by_side(past), side_by_side(own)]).astype(np.float32)


def _moba(proj, g_q, g_k, B, S):
    nb = S // MOBA_BLOCK
    width = MOBA_HEADS * MOBA_HEAD_DIM
    slopes, bias = _moba_bias_tiles()
    gq2 = jnp.tile(g_q.reshape(1, MOBA_HEAD_DIM), (1, 2))
    gk2 = jnp.tile(g_k.reshape(1, MOBA_HEAD_DIM), (1, 2))
    full = lambda shape: pl.BlockSpec(shape, lambda b, q, s: (0,) * len(shape))
    grid_spec = pltpu.PrefetchScalarGridSpec(
        num_scalar_prefetch=1,
        grid=(B, nb),
        in_specs=[
            pl.BlockSpec((MOBA_BLOCK, width), lambda b, q, s: (b * nb + q, 2)),
            pl.BlockSpec((S, width), lambda b, q, s: (b, 3)),
            pl.BlockSpec((S, width), lambda b, q, s: (b, 4)),
            full((1, LANES)),
            full((1, LANES)),
            full((2, MOBA_BLOCK, MOBA_HEADS * MOBA_BLOCK)),
        ],
        out_specs=pl.BlockSpec((MOBA_BLOCK, width), lambda b, q, s: (b * nb + q, 0)),
        scratch_shapes=[
            pltpu.VMEM((nb, MOBA_BLOCK, width), BF16),
            pltpu.VMEM((nb, MOBA_HEADS, VT_ROWS, MOBA_BLOCK), BF16),
            pltpu.VMEM((nb, width), F32),
            pltpu.VMEM((nb, MOBA_HEADS * MOBA_BLOCK), F32),
            pltpu.VMEM((MOBA_HEADS // 2, 2 * MOBA_BLOCK, LANES), BF16),
            pltpu.VMEM((1, MOBA_HEADS * MOBA_BLOCK), F32),
            pltpu.VMEM((MOBA_HEADS, VT_ROWS, MOBA_BLOCK), F32),
            pltpu.VMEM((2, MOBA_BLOCK, MOBA_HEADS * MOBA_BLOCK), F32),
            pltpu.VMEM((2, 1, MOBA_HEADS * MOBA_BLOCK), F32),
        ],
    )
    return pl.pallas_call(
        functools.partial(_moba_kernel, nb=nb),
        out_shape=jax.ShapeDtypeStruct((B * S, width), BF16),
        grid_spec=grid_spec,
        compiler_params=pltpu.CompilerParams(
            dimension_semantics=("arbitrary", "arbitrary"), vmem_limit_bytes=VMEM_LIMIT),
        name="moba",
    )(jnp.asarray(slopes), proj, proj, proj, gq2, gk2, jnp.asarray(bias))


def _memkv_kernel(mem_ref, gmem_ref, w_ref, gk_ref, o_ref):
    mn = _rms(mem_ref[...], gmem_ref[...]).astype(BF16)
    kv = jnp.dot(mn, w_ref[...], preferred_element_type=F32)
    width = MEM_HEADS * MEM_HEAD_DIM
    for h in range(MEM_HEADS):
        sl = slice(h * MEM_HEAD_DIM, (h + 1) * MEM_HEAD_DIM)
        o_ref[:, sl] = _rms(kv[:, sl], gk_ref[...]).astype(BF16)
    o_ref[:, width:] = kv[:, width:].astype(BF16)


def _memkv(mem2, g_mem, w_mem_kv, g_k_mem, B):
    width = MEM_HEADS * MEM_HEAD_DIM
    return pl.pallas_call(
        _memkv_kernel,
        out_shape=jax.ShapeDtypeStruct((B * MEM_LEN, 2 * width), BF16),
        grid=(B,),
        in_specs=[
            pl.BlockSpec((MEM_LEN, D_MODEL), lambda b: (b, 0)),
            pl.BlockSpec((1, D_MODEL), lambda b: (0, 0)),
            pl.BlockSpec((D_MODEL, 2 * width), lambda b: (0, 0)),
            pl.BlockSpec((1, MEM_HEAD_DIM), lambda b: (0, 0)),
        ],
        out_specs=pl.BlockSpec((MEM_LEN, 2 * width), lambda b: (b, 0)),
        compiler_params=pltpu.CompilerParams(
            dimension_semantics=("arbitrary",), vmem_limit_bytes=VMEM_LIMIT),
        name="memkv",
    )(mem2, g_mem, w_mem_kv, g_k_mem)


def _merge_kernel(gu_ref, vn_ref, yb_ref, qm_ref, ga_ref, gb_ref, gc_ref, x_ref, kv_ref,
                  ws_ref, bsp_ref, gqm_ref, wa_ref, wb_ref, wc_ref, wo_ref, gffn_ref,
                  wrh_ref, wrl_ref, br_ref,
                  x1_ref, h2_hbm, tope_ref, topw_ref, rank_ref, cnt_ref, run_sc, hp_sc, hp_sem):
    i = pl.program_id(0)
    tm = x_ref.shape[0]

    @pl.when(i == 0)
    def _():
        run_sc[...] = jnp.zeros_like(run_sc)

    ya_chunks = []
    for c in range(tm // SGU_CHUNK):
        rows = slice(c * SGU_CHUNK, (c + 1) * SGU_CHUNK)
        groups = []
        for g in range(SGU_GROUPS):
            cols = slice(g * LANES, (g + 1) * LANES)
            groups.append(jnp.dot(ws_ref[g], vn_ref[rows, cols], preferred_element_type=F32))
        mixed = jnp.concatenate(groups, axis=1) + bsp_ref[...]
        ya_chunks.append(gu_ref[rows, :].astype(F32) * mixed)
    y_a = jnp.concatenate(ya_chunks, axis=0).astype(BF16)

    width = MEM_HEADS * MEM_HEAD_DIM
    yc_heads = []
    for h in range(MEM_HEADS):
        cols = slice(h * MEM_HEAD_DIM, (h + 1) * MEM_HEAD_DIM)
        qn = _rms(qm_ref[:, cols].astype(F32), gqm_ref[...]).astype(BF16)
        s = lax.dot_general(qn, kv_ref[:, cols], (((1,), (1,)), ((), ())),
                            preferred_element_type=F32) * (MEM_HEAD_DIM ** -0.5)
        s = s - jnp.max(s, axis=-1, keepdims=True)
        p = jnp.exp(s)
        p = p / jnp.sum(p, axis=-1, keepdims=True)
        vcols = slice(width + h * MEM_HEAD_DIM, width + (h + 1) * MEM_HEAD_DIM)
        yc_heads.append(jnp.dot(p.astype(BF16), kv_ref[:, vcols], preferred_element_type=F32))
    y_c = jnp.concatenate(yc_heads, axis=1).astype(BF16)

    merged = ga_ref[...].astype(F32) * jnp.dot(y_a, wa_ref[...], preferred_element_type=F32)
    merged = merged + gb_ref[...].astype(F32) * jnp.dot(yb_ref[...], wb_ref[...],
                                                       preferred_element_type=F32)
    merged = merged + gc_ref[...].astype(F32) * jnp.dot(y_c, wc_ref[...],
                                                       preferred_element_type=F32)
    x1 = x_ref[...] + jnp.dot(merged.astype(BF16), wo_ref[...], preferred_element_type=F32)
    x1_ref[...] = x1
    h2 = _rms(x1, gffn_ref[...])
    h2_hi = h2.astype(BF16)
    h2_hi32 = h2_hi.astype(F32)
    h2_lo = (h2 - h2_hi32).astype(BF16)
    bits = lax.bitcast_convert_type(h2_hi32, jnp.uint32)
    half = D_MODEL // 2
    slot = i % 2
    hp_sc[slot] = (bits[:, :half] >> 16) | (bits[:, half:] & jnp.uint32(0xFFFF0000))

    def h2_copies(step, s):
        rows = pl.ds(pl.multiple_of(step * tm, tm), tm)
        return [pltpu.make_async_copy(hp_sc.at[s, :, c * LANES:(c + 1) * LANES], h2_hbm.at[rows, c],
                                      hp_sem.at[s]) for c in range(half // LANES)]

    for cp in h2_copies(i, slot):
        cp.start()

    nt = (((1,), (1,)), ((), ()))
    logits = (lax.dot_general(wrh_ref[...], h2_hi, nt, preferred_element_type=F32)
              + lax.dot_general(wrh_ref[...], h2_lo, nt, preferred_element_type=F32)
              + lax.dot_general(wrl_ref[...], h2_hi, nt, preferred_element_type=F32)
              + br_ref[...])
    row = lax.broadcasted_iota(jnp.int32, (N_EXPERTS, tm), 0)
    work = logits
    top_e, top_l = [], []
    for _ in range(TOP_K):
        mx = jnp.max(work, axis=0, keepdims=True)
        idx = jnp.min(jnp.where(work == mx, row, N_EXPERTS), axis=0, keepdims=True)
        top_e.append(idx)
        top_l.append(mx)
        work = jnp.where(row == idx, -jnp.inf, work)
    exps = [jnp.exp(v - top_l[0]) for v in top_l]
    denom = exps[0] + exps[1] + exps[2] + exps[3]

    onehots = [row == e for e in top_e]
    multi = jnp.zeros((N_EXPERTS, tm), F32)
    for oh in onehots:
        multi = multi + jnp.where(oh, 1.0, 0.0)
    r_i = lax.broadcasted_iota(jnp.int32, (tm, tm), 0)
    c_i = lax.broadcasted_iota(jnp.int32, (tm, tm), 1)
    earlier = jnp.where(r_i < c_i, 1.0, 0.0).astype(BF16)
    pos = run_sc[...] + jnp.dot(multi.astype(BF16), earlier, preferred_element_type=F32)
    run_sc[...] = run_sc[...] + jnp.sum(multi, axis=1, keepdims=True)
    cnt_ref[...] = run_sc[...]

    for k in range(TOP_K):
        tope_ref[k:k + 1, :] = top_e[k]
        topw_ref[k:k + 1, :] = exps[k] / denom
        rank_ref[k:k + 1, :] = jnp.sum(jnp.where(onehots[k], pos, 0.0), axis=0,
                                       keepdims=True).astype(jnp.int32)

    @pl.when(i >= 1)
    def _():
        for cp in h2_copies(i - 1, 1 - slot):
            cp.wait()

    @pl.when(i == pl.num_programs(0) - 1)
    def _():
        for cp in h2_copies(i, slot):
            cp.wait()


def _merge(proj, yb, x2, kvmem, ws, bsp, g_q_mem, wa, wb, wc, wo, g_ffn, w_router, b_router, S):
    T = x2.shape[0]
    tm = MERGE_TM
    per_b = S // tm
    full = lambda shape: pl.BlockSpec(shape, lambda i: (0,) * len(shape))
    in_specs = [
        pl.BlockSpec((tm, SPLIT), lambda i: (i, 0)),
        pl.BlockSpec((tm, SPLIT), lambda i: (i, 1)),
        pl.BlockSpec((tm, SPLIT), lambda i: (i, 0)),
        pl.BlockSpec((tm, SPLIT), lambda i: (i, 5)),
        pl.BlockSpec((tm, D_MODEL), lambda i: (i, 3)),
        pl.BlockSpec((tm, D_MODEL), lambda i: (i, 4)),
        pl.BlockSpec((tm, D_MODEL), lambda i: (i, 5)),
        pl.BlockSpec((tm, D_MODEL), lambda i: (i, 0)),
        pl.BlockSpec((MEM_LEN, 2 * MEM_HEADS * MEM_HEAD_DIM), lambda i: (i // per_b, 0)),
        full((SGU_GROUPS, SGU_CHUNK, SGU_CHUNK)),
        full((SGU_CHUNK, SGU_WIDTH)),
        full((1, MEM_HEAD_DIM)),
        full((SGU_WIDTH, D_MODEL)),
        full((SGU_WIDTH, D_MODEL)),
        full((SGU_WIDTH, D_MODEL)),
        full((D_MODEL, D_MODEL)),
        full((1, D_MODEL)),
        full((N_EXPERTS, D_MODEL)),
        full((N_EXPERTS, D_MODEL)),
        full((N_EXPERTS, 1)),
    ]
    out_shape = (
        jax.ShapeDtypeStruct((T, D_MODEL), F32),
        jax.ShapeDtypeStruct((T, D_MODEL // 2 // LANES, LANES), jnp.uint32),
        jax.ShapeDtypeStruct((TOP_K, T), jnp.int32),
        jax.ShapeDtypeStruct((TOP_K, T), F32),
        jax.ShapeDtypeStruct((TOP_K, T), jnp.int32),
        jax.ShapeDtypeStruct((N_EXPERTS, 1), F32),
    )
    out_specs = (
        pl.BlockSpec((tm, D_MODEL), lambda i: (i, 0)),
        pl.BlockSpec(memory_space=pl.ANY),
        pl.BlockSpec((TOP_K, tm), lambda i: (0, i)),
        pl.BlockSpec((TOP_K, tm), lambda i: (0, i)),
        pl.BlockSpec((TOP_K, tm), lambda i: (0, i)),
        pl.BlockSpec((N_EXPERTS, 1), lambda i: (0, 0)),
    )
    wr_t = w_router.T
    wr_hi = wr_t.astype(BF16)
    wr_lo = (wr_t - wr_hi.astype(F32)).astype(BF16)
    return pl.pallas_call(
        _merge_kernel,
        out_shape=out_shape,
        grid=(T // tm,),
        in_specs=in_specs,
        out_specs=out_specs,
        scratch_shapes=[pltpu.VMEM((N_EXPERTS, 1), F32),
                        pltpu.VMEM((2, tm, D_MODEL // 2), jnp.uint32), pltpu.SemaphoreType.DMA((2,))],
        compiler_params=pltpu.CompilerParams(
            dimension_semantics=("arbitrary",), vmem_limit_bytes=VMEM_LIMIT),
        name="merge",
    )(proj, proj, yb, proj, proj, proj, proj, x2, kvmem, ws, bsp, g_q_mem,
      wa, wb, wc, wo, g_ffn, wr_hi, wr_lo, b_router.reshape(N_EXPERTS, 1))


def _dispatch_kernel(padlo_ref, padlen_ref, dest_ref, h_hbm, xbuf_ref, sem, z_sc, z_sem, h_sc, in_sem):
    i = pl.program_id(0)
    tm = h_sc.shape[1]

    @pl.when(i == 0)
    def _zero_padding_rows():
        z_sc[...] = jnp.zeros(z_sc.shape, z_sc.dtype)
        bits = range(EXPERT_BM.bit_length() - 2, -1, -1)

        def pieces(e):
            off, length = padlo_ref[e], padlen_ref[e]
            for b in bits:
                take = ((length >> b) & 1) == 1
                yield take, pltpu.make_async_copy(z_sc.at[pl.ds(0, 1 << b)],
                                                  xbuf_ref.at[pl.ds(off, 1 << b)], z_sem)
                off = off + jnp.where(take, 1 << b, 0)

        for e in range(N_EXPERTS):
            for take, cp in pieces(e):
                pl.when(take)(cp.start)
        for e in range(N_EXPERTS):
            for take, cp in pieces(e):
                pl.when(take)(cp.wait)

        zrows = z_sc.shape[0]
        first = (padlo_ref[N_EXPERTS - 1] + padlen_ref[N_EXPERTS - 1]) // zrows
        def tail_copy(j):
            return pltpu.make_async_copy(z_sc, xbuf_ref.at[pl.ds(pl.multiple_of(j * zrows, zrows), zrows)], z_sem)
        lax.fori_loop(first, xbuf_ref.shape[0] // zrows, lambda j, c: (tail_copy(j).start(), c)[1], 0)
        lax.fori_loop(first, xbuf_ref.shape[0] // zrows, lambda j, c: (tail_copy(j).wait(), c)[1], 0)

    n_steps = pl.num_programs(0)
    nbuf = h_sc.shape[0]

    def fetch(step, s):
        rows = pl.ds(pl.multiple_of(step * tm, tm), tm)
        return pltpu.make_async_copy(h_hbm.at[rows], h_sc.at[s], in_sem.at[s])

    def drain(s):
        for _ in range(TOP_K):
            pltpu.make_async_copy(h_sc.at[s], xbuf_ref.at[pl.ds(0, tm)], sem.at[s]).wait()

    @pl.when(i == 0)
    def _():
        fetch(0, 0).start()

    @pl.when(i + 1 < n_steps)
    def _():
        fetch(i + 1, (i + 1) % nbuf).start()

    fetch(i, i % nbuf).wait()
    for s in range(nbuf):
        @pl.when(i % nbuf == s)
        def _():
            for r in range(tm):
                for k in range(TOP_K):
                    pltpu.make_async_copy(h_sc.at[s, r], xbuf_ref.at[dest_ref[k * tm + r]],
                                          sem.at[s]).start(priority=k % 2)

            @pl.when(i >= 1)
            def _():
                drain((s - 1) % nbuf)

            @pl.when(i == n_steps - 1)
            def _():
                drain(s)


def _dispatch(pad_lo, pad_len, dest_flat, h2, n_rows):
    T, slabs, lanes = h2.shape
    tm = ROUTE_TM
    grid_spec = pltpu.PrefetchScalarGridSpec(
        num_scalar_prefetch=2,
        grid=(T // tm,),
        in_specs=[
            pl.BlockSpec((tm * TOP_K,), lambda i, lo, ln: (i,), memory_space=pltpu.SMEM),
            pl.BlockSpec(memory_space=pl.ANY),
        ],
        out_specs=pl.BlockSpec(memory_space=pl.ANY),
        scratch_shapes=[pltpu.SemaphoreType.DMA((3,)), pltpu.VMEM((EXPERT_BM // 2, slabs, lanes), h2.dtype),
                        pltpu.SemaphoreType.DMA, pltpu.VMEM((3, tm, slabs, lanes), h2.dtype),
                        pltpu.SemaphoreType.DMA((3,))],
    )
    return pl.pallas_call(
        _dispatch_kernel,
        out_shape=jax.ShapeDtypeStruct((n_rows, slabs, lanes), h2.dtype),
        grid_spec=grid_spec,
        compiler_params=pltpu.CompilerParams(
            dimension_semantics=("arbitrary",), vmem_limit_bytes=VMEM_LIMIT),
        name="dispatch",
    )(pad_lo, pad_len, dest_flat, h2)


FF_CHUNK = 512


def _expert_kernel(be_ref, nused_ref, x_hbm, w1_ref, b1_ref, w2_ref, b2_ref,
                   y_hbm, w1_sc, w2_sc, x_sc, x_sem, y_sc, y_sem):
    i = pl.program_id(0)
    bm = x_sc.shape[1]
    n_used = nused_ref[0]

    def x_copies(blk, slot):
        rows = pl.ds(pl.multiple_of(blk * bm, bm), bm)
        return [pltpu.make_async_copy(x_hbm.at[rows, c], x_sc.at[slot, :, c * LANES:(c + 1) * LANES],
                                      x_sem.at[slot]) for c in range(x_hbm.shape[1])]

    def y_copies(blk, slot):
        rows = pl.ds(pl.multiple_of(blk * bm, bm), bm)
        return [pltpu.make_async_copy(y_sc.at[slot, :, c * LANES:(c + 1) * LANES], y_hbm.at[rows, c],
                                      y_sem.at[slot]) for c in range(y_hbm.shape[1])]

    @pl.when(i < n_used)
    def _():
        slot = i % 2

        @pl.when(i == 0)
        def _():
            for cp in x_copies(0, 0):
                cp.start()

        @pl.when(i + 1 < n_used)
        def _():
            for cp in x_copies(i + 1, 1 - slot):
                cp.start()

        @pl.when(jnp.logical_or(i == 0, be_ref[i] != be_ref[jnp.maximum(i - 1, 0)]))
        def _():
            w1_sc[...] = w1_ref[0].astype(BF16)
            w2_sc[...] = w2_ref[0].astype(BF16)

        for cp in x_copies(i, slot):
            cp.wait()
        packed = x_sc[slot]
        lo = lax.bitcast_convert_type(packed << 16, F32)
        hi = lax.bitcast_convert_type(packed & jnp.uint32(0xFFFF0000), F32)
        xb = jnp.concatenate([lo, hi], axis=1).astype(BF16)
        y = b2_ref[0]
        for c in range(D_FF // FF_CHUNK):
            g_cols = slice(c * FF_CHUNK, (c + 1) * FF_CHUNK)
            l_cols = slice(D_FF + c * FF_CHUNK, D_FF + (c + 1) * FF_CHUNK)
            glu = jnp.dot(xb, w1_sc[:, g_cols], preferred_element_type=F32) + b1_ref[0, :, g_cols]
            lin = jnp.dot(xb, w1_sc[:, l_cols], preferred_element_type=F32) + b1_ref[0, :, l_cols]
            glu = jnp.minimum(glu, SWIGLU_LIMIT)
            lin = jnp.clip(lin, -SWIGLU_LIMIT, SWIGLU_LIMIT)
            act = glu * jax.nn.sigmoid(SWIGLU_ALPHA * glu) * (lin + 1.0)
            y = y + jnp.dot(act.astype(BF16), w2_sc[g_cols, :], preferred_element_type=F32)
        y_sc[slot] = y
        for cp in y_copies(i, slot):
            cp.start()

        @pl.when(i >= 1)
        def _():
            for cp in y_copies(i - 1, 1 - slot):
                cp.wait()

        @pl.when(i == n_used - 1)
        def _():
            for cp in y_copies(i, slot):
                cp.wait()

    @pl.when(i >= n_used)
    def _():
        @pl.when(i == n_used)
        def _():
            y_sc[0] = jnp.zeros(y_sc.shape[1:], F32)

        for cp in y_copies(i, 0):
            cp.start()

        @pl.when(i == pl.num_programs(0) - 1)
        def _():
            def drain(j, carry):
                for cp in y_copies(j, 0):
                    cp.wait()
                return carry
            lax.fori_loop(n_used, pl.num_programs(0), drain, 0)


def _experts(block_e, n_used, xbuf, w1, b1, w2, b2):
    n_rows, slabs, lanes = xbuf.shape
    bm = EXPERT_BM
    grid_spec = pltpu.PrefetchScalarGridSpec(
        num_scalar_prefetch=2,
        grid=(n_rows // bm,),
        in_specs=[
            pl.BlockSpec(memory_space=pl.ANY),
            pl.BlockSpec((1, D_MODEL, 2 * D_FF), lambda i, be, nu: (be[i], 0, 0)),
            pl.BlockSpec((1, 1, 2 * D_FF), lambda i, be, nu: (be[i], 0, 0)),
            pl.BlockSpec((1, D_FF, D_MODEL), lambda i, be, nu: (be[i], 0, 0)),
            pl.BlockSpec((1, 1, D_MODEL), lambda i, be, nu: (be[i], 0, 0)),
        ],
        out_specs=pl.BlockSpec(memory_space=pl.ANY),
        scratch_shapes=[pltpu.VMEM((D_MODEL, 2 * D_FF), BF16), pltpu.VMEM((D_FF, D_MODEL), BF16),
                        pltpu.VMEM((2, bm, slabs * lanes), jnp.uint32), pltpu.SemaphoreType.DMA((2,)),
                        pltpu.VMEM((2, bm, D_MODEL), F32), pltpu.SemaphoreType.DMA((2,))],
    )
    return pl.pallas_call(
        _expert_kernel,
        out_shape=jax.ShapeDtypeStruct((n_rows, D_MODEL // LANES, LANES), F32),
        grid_spec=grid_spec,
        compiler_params=pltpu.CompilerParams(
            dimension_semantics=("arbitrary",), vmem_limit_bytes=VMEM_LIMIT),
        name="experts",
    )(block_e, n_used, xbuf, w1, b1, w2, b2)


def _combine_kernel(dest_ref, nxt_ref, w_ref, x1_hbm, ybuf_hbm, out_hbm, g_sc, x1_sc, res_sc,
                    g_sem, x_sem, o_sem):
    i = pl.program_id(0)
    n_steps = pl.num_programs(0)
    tm = g_sc.shape[2]
    chunks = D_MODEL // LANES
    slot = i % 2

    def gather(idx_ref, s):
        for r in range(tm):
            for k in range(TOP_K):
                pltpu.make_async_copy(ybuf_hbm.at[idx_ref[k * tm + r]], g_sc.at[s, k, r],
                                      g_sem.at[s]).start(priority=k % 2)

    def x1_copies(step, s):
        rows = pl.ds(pl.multiple_of(step * tm, tm), tm)
        return [pltpu.make_async_copy(x1_hbm.at[rows, c * LANES:(c + 1) * LANES], x1_sc.at[s, :, c, :],
                                      x_sem.at[s]) for c in range(chunks)]

    def out_copies(step, s):
        rows = pl.ds(pl.multiple_of(step * tm, tm), tm)
        return [pltpu.make_async_copy(res_sc.at[s, :, c, :], out_hbm.at[rows, c * LANES:(c + 1) * LANES],
                                      o_sem.at[s]) for c in range(chunks)]

    @pl.when(i == 0)
    def _():
        for cp in x1_copies(0, 0):
            cp.start()

    @pl.when(i + 1 < n_steps)
    def _():
        for cp in x1_copies(i + 1, 1 - slot):
            cp.start()

    @pl.when(i == 0)
    def _():
        gather(dest_ref, 0)

    for parity in range(2):
        @pl.when(jnp.logical_and(i + 1 < n_steps, slot == parity))
        def _():
            gather(nxt_ref, 1 - parity)

    for k in range(TOP_K):
        pltpu.make_async_copy(ybuf_hbm.at[pl.ds(0, tm)], g_sc.at[slot, k], g_sem.at[slot]).wait()
    for cp in x1_copies(i, slot):
        cp.wait()

    for parity in range(2):
        @pl.when(slot == parity)
        def _():
            for r in range(tm):
                acc = x1_sc[parity, r]
                for k in range(TOP_K):
                    acc = acc + w_ref[k * tm + r] * g_sc[parity, k, r]
                res_sc[parity, r] = acc
    for cp in out_copies(i, slot):
        cp.start()

    @pl.when(i >= 1)
    def _():
        for cp in out_copies(i - 1, 1 - slot):
            cp.wait()

    @pl.when(i == n_steps - 1)
    def _():
        for cp in out_copies(i, slot):
            cp.wait()


def _combine(dest_flat, x1, w_flat, ybuf):
    T = x1.shape[0]
    tm = ROUTE_TM
    tile = (D_MODEL // LANES, LANES)
    return pl.pallas_call(
        _combine_kernel,
        out_shape=jax.ShapeDtypeStruct((T, D_MODEL), F32),
        grid=(T // tm,),
        in_specs=[
            pl.BlockSpec((tm * TOP_K,), lambda i: (i,), memory_space=pltpu.SMEM),
            pl.BlockSpec((tm * TOP_K,), lambda i: (jnp.minimum(i + 1, T // tm - 1),),
                         memory_space=pltpu.SMEM),
            pl.BlockSpec((tm * TOP_K,), lambda i: (i,), memory_space=pltpu.SMEM),
            pl.BlockSpec(memory_space=pl.ANY),
            pl.BlockSpec(memory_space=pl.ANY),
        ],
        out_specs=pl.BlockSpec(memory_space=pl.ANY),
        scratch_shapes=[pltpu.VMEM((2, TOP_K, tm) + tile, F32), pltpu.VMEM((2, tm) + tile, F32),
                        pltpu.VMEM((2, tm) + tile, F32), pltpu.SemaphoreType.DMA((2,)),
                        pltpu.SemaphoreType.DMA((2,)), pltpu.SemaphoreType.DMA((2,))],
        compiler_params=pltpu.CompilerParams(
            dimension_semantics=("arbitrary",), vmem_limit_bytes=VMEM_LIMIT),
        name="combine",
    )(dest_flat, dest_flat, w_flat, x1, ybuf)


def _layer(x2, mem2, B, S, g_mix, w_in, g_sgu_v, w_spatial, b_spatial, g_q_moba, g_k_moba,
           g_mem, w_mem_kv, g_q_mem, g_k_mem, w_br_sgu, w_br_moba, w_br_mem, w_out,
           g_ffn, w_router, b_router, w_gate_up, b_gate_up, w_down, b_down):
    T = B * S
    row = lambda v: v.reshape(1, -1)
    proj = _inproj(x2, row(g_mix), w_in.astype(BF16), row(g_sgu_v))
    yb = _moba(proj, g_q_moba, g_k_moba, B, S)
    kvmem = _memkv(mem2, row(g_mem), w_mem_kv.astype(BF16), row(g_k_mem), B)

    causal = jnp.tril(jnp.ones((SGU_CHUNK, SGU_CHUNK), dtype=bool))
    ws = jnp.where(causal[None], w_spatial, 0.0).astype(BF16)
    bsp = jnp.repeat(b_spatial.T, LANES, axis=1)
    x1, h2, top_e, top_w, rank, counts = _merge(
        proj, yb, x2, kvmem, ws, bsp, row(g_q_mem), w_br_sgu.astype(BF16),
        w_br_moba.astype(BF16), w_br_mem.astype(BF16), w_out.astype(BF16), row(g_ffn),
        w_router, b_router, S)

    bm = EXPERT_BM
    n_blocks = -(-(T * TOP_K + N_EXPERTS * (bm - 1)) // bm)
    cnt = counts.reshape(N_EXPERTS).astype(jnp.int32)
    padded = (cnt + bm - 1) // bm * bm
    pad_ends = jnp.cumsum(padded)
    pad_starts = pad_ends - padded
    e_ids = jnp.arange(N_EXPERTS, dtype=jnp.int32)[:, None, None]
    start_of = jnp.sum(jnp.where(top_e[None] == e_ids, pad_starts[:, None, None], 0), axis=0)
    per_tile = lambda a: a.reshape(TOP_K, T // ROUTE_TM, ROUTE_TM).transpose(1, 0, 2).reshape(T * TOP_K)
    dest = per_tile(start_of + rank)
    top_w = per_tile(top_w)
    block_row0 = jnp.arange(n_blocks, dtype=jnp.int32) * bm
    block_e = jnp.minimum(
        jnp.sum((pad_ends[None, :] <= block_row0[:, None]).astype(jnp.int32), axis=1), N_EXPERTS - 1)
    n_used = (pad_ends[-1:] // bm).astype(jnp.int32)

    xbuf = _dispatch(pad_starts + cnt, padded - cnt, dest, h2, n_blocks * bm)
    ybuf = _experts(block_e, n_used, xbuf, w_gate_up, b_gate_up.reshape(N_EXPERTS, 1, 2 * D_FF),
                    w_down, b_down.reshape(N_EXPERTS, 1, D_MODEL))
    return _combine(dest, x1, top_w, ybuf)


def kernel(x, mem, g_mix, w_in, g_sgu_v, w_spatial, b_spatial, g_q_moba, g_k_moba, g_mem, w_mem_kv, g_q_mem, g_k_mem, w_br_sgu, w_br_moba, w_br_mem, w_out, g_ffn, w_router, b_router, w_gate_up, b_gate_up, w_down, b_down):
    B, S, D = x.shape
    x2 = x.reshape(B * S, D)
    mem2 = mem.reshape(B * MEM_LEN, D)
    params = (g_mix, w_in, g_sgu_v, w_spatial, b_spatial, g_q_moba, g_k_moba, g_mem, w_mem_kv,
              g_q_mem, g_k_mem, w_br_sgu, w_br_moba, w_br_mem, w_out, g_ffn, w_router, b_router,
              w_gate_up, b_gate_up, w_down, b_down)
    for l in range(g_mix.shape[0]):
        x2 = _layer(x2, mem2, B, S, *(p[l] for p in params))
    return x2.reshape(B, S, D)
```
